```python
import math, functools
import jax, jax.numpy as jnp
from jax import lax
import numpy as np

D_MODEL = 1024
BATCH = 16
SEQ = 2048
DEPTH = 1
DEC_BATCH = 32
DEC_SEQ = 8
PAST_LEN = 16384
PAGE_SIZE = 128

N_HEADS = 4
HEAD_DIM = 64
V_DIM = 2 * HEAD_DIM
ATTN_W = N_HEADS * V_DIM
QK_W = N_HEADS * 2 * HEAD_DIM
SCALE = HEAD_DIM ** -0.5
Q_BLOCK = 128
SSM_W = 512
GROUP_CH = 16
N_GROUPS = SSM_W // GROUP_CH
STATE_DIM = 64
DT_MIN = 0.001
DT_MAX = 0.1
IN_W = QK_W + QK_W + ATTN_W + SSM_W + 2 * D_MODEL
N_BUCKETS = 32
MAX_EXACT = N_BUCKETS // 2
MAX_DISTANCE = 128
N_EXPERTS = 32
TOP_K = 4
EXPERT_FF = D_MODEL
SWIGLU_LIMIT = 7.0
SWIGLU_ALPHA = 1.702
EPS = 1e-6
NEG_INF = -1e30

kernel_name = 'hybrid_diffattn_s5_moe_step'


def rms_norm(x, g):
    xf = x.astype(jnp.float32)
    xf = xf * lax.rsqrt(jnp.mean(xf * xf, axis=-1, keepdims=True) + EPS)
    return (xf * g.astype(jnp.float32)).astype(x.dtype)


def rel_bias_matrix(q_pos, k_pos, rel_bias):
    n = jnp.maximum(q_pos[:, None] - k_pos[None, :], 0)
    nf = jnp.maximum(n, 1).astype(jnp.float32)
    large = MAX_EXACT + (jnp.log(nf / MAX_EXACT) / math.log(MAX_DISTANCE / MAX_EXACT)
                         * (N_BUCKETS - MAX_EXACT)).astype(jnp.int32)
    bucket = jnp.where(n < MAX_EXACT, n, jnp.minimum(large, N_BUCKETS - 1))
    return jnp.transpose(rel_bias[bucket].astype(jnp.float32), (2, 0, 1))


def diff_weights(scores, lam):
    p = jax.nn.softmax(scores, axis=-1)
    return p[:, :, 0] - lam * p[:, :, 1]


def attend_prompt(q, k, v, lam, rel_bias):
    b, s = q.shape[0], q.shape[1]
    blk = min(Q_BLOCK, s)
    n_blk = s // blk
    k_pos = jnp.arange(s)

    def one_block(i):
        q_blk = lax.dynamic_slice_in_dim(q, i * blk, blk, axis=1)
        q_pos = i * blk + jnp.arange(blk)
        sc = jnp.einsum('bqhmd,bkhmd->bhmqk', q_blk, k).astype(jnp.float32) * SCALE
        sc = sc + rel_bias_matrix(q_pos, k_pos, rel_bias)[None, :, None]
        sc = jnp.where((k_pos[None, :] <= q_pos[:, None])[None, None, None], sc, NEG_INF)
        w = diff_weights(sc, lam)
        return jnp.einsum('bhqk,bkhe->bqhe', w.astype(v.dtype), v)

    out = lax.map(one_block, jnp.arange(n_blk))
    return jnp.transpose(out, (1, 0, 2, 3, 4)).reshape(b, s, N_HEADS, V_DIM)


def attend_sample(q, k, v, lam, k_past, v_past, rel_bias):
    lq, p_len = q.shape[1], k_past.shape[1]
    q_pos = p_len + jnp.arange(lq)
    sp = jnp.einsum('bqhmd,bkhmd->bhmqk', q, k_past).astype(jnp.float32) * SCALE
    sp = sp + rel_bias_matrix(q_pos, jnp.arange(p_len), rel_bias)[None, :, None]
    sn = jnp.einsum('bqhmd,bkhmd->bhmqk', q, k).astype(jnp.float32) * SCALE
    sn = sn + rel_bias_matrix(q_pos, q_pos, rel_bias)[None, :, None]
    sn = jnp.where((q_pos[None, :] <= q_pos[:, None])[None, None, None], sn, NEG_INF)
    w = diff_weights(jnp.concatenate([sp, sn], axis=-1), lam).astype(v.dtype)
    return (jnp.einsum('bhqk,bkhe->bqhe', w[..., :p_len], v_past)
            + jnp.einsum('bhqk,bkhe->bqhe', w[..., p_len:], v))


def _cmul_combine(e1, e2):
    a1r, a1i, b1r, b1i = e1
    a2r, a2i, b2r, b2i = e2
    return (a2r * a1r - a2i * a1i,
            a2r * a1i + a2i * a1r,
            a2r * b1r - a2i * b1i + b2r,
            a2r * b1i + a2i * b1r + b2i)


def s5_branch(u, s0_re, s0_im, pl):
    b, L = u.shape[0], u.shape[1]
    dt = jnp.exp(pl['ssm_log_dt'])[:, None]
    a_re = jnp.minimum(pl['ssm_a_re'], -1e-4)
    a_im = pl['ssm_a_im']
    mag = jnp.exp(a_re * dt)
    lb_re = mag * jnp.cos(a_im * dt)
    lb_im = mag * jnp.sin(a_im * dt)
    den = a_re * a_re + a_im * a_im
    nr = lb_re - 1.0
    co_re = (nr * a_re + lb_im * a_im) / den
    co_im = (lb_im * a_re - nr * a_im) / den
    b_re, b_im = pl['ssm_b_re'], pl['ssm_b_im']
    bb_re = co_re[..., None] * b_re - co_im[..., None] * b_im
    bb_im = co_re[..., None] * b_im + co_im[..., None] * b_re
    bu_re = jnp.einsum('blgc,gpc->blgp', u, bb_re)
    bu_im = jnp.einsum('blgc,gpc->blgp', u, bb_im)
    bu_re = bu_re.at[:, 0].add(lb_re * s0_re - lb_im * s0_im)
    bu_im = bu_im.at[:, 0].add(lb_re * s0_im + lb_im * s0_re)
    A_re = jnp.broadcast_to(lb_re, bu_re.shape)
    A_im = jnp.broadcast_to(lb_im, bu_im.shape)
    _, _, h_re, h_im = lax.associative_scan(_cmul_combine, (A_re, A_im, bu_re, bu_im), axis=1)
    y = (jnp.einsum('blgp,gcp->blgc', h_re, pl['ssm_c_re'])
         - jnp.einsum('blgp,gcp->blgc', h_im, pl['ssm_c_im'])
         + pl['ssm_d'] * u)
    y = jax.nn.gelu(y.reshape(b, L, SSM_W))
    y = y * jax.nn.sigmoid(y @ pl['w_glu'] + pl['b_glu'])
    return y, h_re[:, -1], h_im[:, -1]


def moe(x, pl):
    b, L, d = x.shape
    xt = x.reshape(b * L, d)
    logits = (xt @ pl['w_router'] + pl['b_router']).astype(jnp.float32)
    top_v, top_i = lax.top_k(logits, TOP_K)
    probs = jax.nn.softmax(top_v, axis=-1)
    comb = jnp.sum(jax.nn.one_hot(top_i, N_EXPERTS, dtype=jnp.float32) * probs[..., None], axis=1)
    out = jnp.zeros((b * L, d), jnp.float32)
    for e in range(N_EXPERTS):
        gt = xt @ pl['w_e_gate'][e] + pl['b_e_gate'][e]
        up = xt @ pl['w_e_up'][e] + pl['b_e_up'][e]
        gt = jnp.minimum(gt, SWIGLU_LIMIT)
        up = jnp.clip(up, -SWIGLU_LIMIT, SWIGLU_LIMIT)
        h = (up + 1.0) * (gt * jax.nn.sigmoid(SWIGLU_ALPHA * gt))
        out = out + comb[:, e:e + 1] * (h @ pl['w_e_down'][e] + pl['b_e_down'][e])
    return out.astype(x.dtype).reshape(b, L, d)


def block(x, attend, s0_re, s0_im, pl, lambda_init):
    b, L, _ = x.shape
    xn = rms_norm(x, pl['norm_mix_g'])
    proj = xn @ pl['w_in']
    o = 0
    q = proj[..., o:o + QK_W].reshape(b, L, N_HEADS, 2, HEAD_DIM); o += QK_W
    k = proj[..., o:o + QK_W].reshape(b, L, N_HEADS, 2, HEAD_DIM); o += QK_W
    v = proj[..., o:o + ATTN_W].reshape(b, L, N_HEADS, V_DIM); o += ATTN_W
    u = proj[..., o:o + SSM_W].reshape(b, L, N_GROUPS, GROUP_CH); o += SSM_W
    g_attn = jax.nn.sigmoid(proj[..., o:o + D_MODEL]); o += D_MODEL
    g_ssm = jax.nn.sigmoid(proj[..., o:o + D_MODEL])
    f32 = jnp.float32
    lam = (jnp.exp(jnp.sum(pl['lambda_q1'].astype(f32) * pl['lambda_k1'].astype(f32)))
           - jnp.exp(jnp.sum(pl['lambda_q2'].astype(f32) * pl['lambda_k2'].astype(f32)))
           + lambda_init)
    oa = attend(q, k, v, lam)
    oa = rms_norm(oa, pl['subln_g']) * (1.0 - lambda_init)
    oa = oa.reshape(b, L, ATTN_W)
    ys, s_re, s_im = s5_branch(u, s0_re, s0_im, pl)
    merged = g_attn * (oa @ pl['w_attn_up']) + g_ssm * (ys @ pl['w_ssm_up'])
    h = x + merged @ pl['w_out']
    out = h + moe(rms_norm(h, pl['norm_ffn_g']), pl)
    return out, k, v, s_re, s_im


def setup_inputs(seed: int = 0) -> dict:
    key = jax.random.key(seed)
    ks = iter(jax.random.split(key, 48))
    f32 = jnp.float32

    def nrm(shape, scale):
        return scale * jax.random.normal(next(ks), shape, f32)

    n_pages = PAST_LEN // PAGE_SIZE
    n_used = DEC_BATCH * n_pages
    n_pool = n_used + max(1, n_used // 4)
    inp = {}
    inp['x_prompt'] = nrm((BATCH, SEQ, D_MODEL), 1.0)
    inp['x_sample'] = nrm((DEC_BATCH, DEC_SEQ, D_MODEL), 1.0)
    inp['cache_k'] = nrm((DEPTH, n_pool, PAGE_SIZE, N_HEADS, 2, HEAD_DIM), 1.0)
    inp['cache_v'] = nrm((DEPTH, n_pool, PAGE_SIZE, N_HEADS, V_DIM), 1.0)
    inp['state_ssm_re'] = nrm((DEPTH, DEC_BATCH, N_GROUPS, STATE_DIM), 0.3)
    inp['state_ssm_im'] = nrm((DEPTH, DEC_BATCH, N_GROUPS, STATE_DIM), 0.3)
    inp['page_table'] = jax.random.permutation(next(ks), n_pool)[:n_used].reshape(DEC_BATCH, n_pages).astype(jnp.int32)
    inp['norm_mix_g'] = 1.0 + nrm((DEPTH, D_MODEL), 0.02)
    inp['w_in'] = nrm((DEPTH, D_MODEL, IN_W), D_MODEL ** -0.5)
    inp['lambda_q1'] = nrm((DEPTH, HEAD_DIM), 0.1)
    inp['lambda_k1'] = nrm((DEPTH, HEAD_DIM), 0.1)
    inp['lambda_q2'] = nrm((DEPTH, HEAD_DIM), 0.1)
    inp['lambda_k2'] = nrm((DEPTH, HEAD_DIM), 0.1)
    inp['subln_g'] = 1.0 + nrm((DEPTH, V_DIM), 0.02)
    inp['rel_bias'] = nrm((N_BUCKETS, N_HEADS), 0.5)
    inp['ssm_a_re'] = -0.5 + nrm((DEPTH, N_GROUPS, STATE_DIM), 0.01)
    inp['ssm_a_im'] = math.pi * jnp.arange(STATE_DIM, dtype=f32)[None, None, :] + nrm((DEPTH, N_GROUPS, STATE_DIM), 0.01)
    inp['ssm_log_dt'] = jax.random.uniform(next(ks), (DEPTH, N_GROUPS), f32, math.log(DT_MIN), math.log(DT_MAX))
    inp['ssm_b_re'] = nrm((DEPTH, N_GROUPS, STATE_DIM, GROUP_CH), (2 * GROUP_CH) ** -0.5)
    inp['ssm_b_im'] = nrm((DEPTH, N_GROUPS, STATE_DIM, GROUP_CH), (2 * GROUP_CH) ** -0.5)
    inp['ssm_c_re'] = nrm((DEPTH, N_GROUPS, GROUP_CH, STATE_DIM), (2 * STATE_DIM) ** -0.5)
    inp['ssm_c_im'] = nrm((DEPTH, N_GROUPS, GROUP_CH, STATE_DIM), (2 * STATE_DIM) ** -0.5)
    inp['ssm_d'] = nrm((DEPTH, N_GROUPS, GROUP_CH), 1.0)
    inp['w_glu'] = nrm((DEPTH, SSM_W, SSM_W), SSM_W ** -0.5)
    inp['b_glu'] = nrm((DEPTH, SSM_W), 0.01)
    inp['w_attn_up'] = nrm((DEPTH, ATTN_W, D_MODEL), ATTN_W ** -0.5)
    inp['w_ssm_up'] = nrm((DEPTH, SSM_W, D_MODEL), SSM_W ** -0.5)
    inp['w_out'] = nrm((DEPTH, D_MODEL, D_MODEL), D_MODEL ** -0.5)
    inp['norm_ffn_g'] = 1.0 + nrm((DEPTH, D_MODEL), 0.02)
    inp['w_router'] = nrm((DEPTH, D_MODEL, N_EXPERTS), D_MODEL ** -0.5)
    inp['b_router'] = nrm((DEPTH, N_EXPERTS), 0.01)
    inp['w_e_gate'] = nrm((DEPTH, N_EXPERTS, D_MODEL, EXPERT_FF), D_MODEL ** -0.5)
    inp['b_e_gate'] = nrm((DEPTH, N_EXPERTS, EXPERT_FF), 0.01)
    inp['w_e_up'] = nrm((DEPTH, N_EXPERTS, D_MODEL, EXPERT_FF), D_MODEL ** -0.5)
    inp['b_e_up'] = nrm((DEPTH, N_EXPERTS, EXPERT_FF), 0.01)
    inp['w_e_down'] = nrm((DEPTH, N_EXPERTS, EXPERT_FF, D_MODEL), EXPERT_FF ** -0.5)
    inp['b_e_down'] = nrm((DEPTH, N_EXPERTS, D_MODEL), 0.01)
    inp['norm_final_g'] = 1.0 + nrm((D_MODEL,), 0.02)
    return inp


def reference(x_prompt, x_sample, cache_k, cache_v, state_ssm_re, state_ssm_im, page_table,
              norm_mix_g, w_in, lambda_q1, lambda_k1, lambda_q2, lambda_k2, subln_g, rel_bias,
              ssm_a_re, ssm_a_im, ssm_log_dt, ssm_b_re, ssm_b_im, ssm_c_re, ssm_c_im, ssm_d,
              w_glu, b_glu, w_attn_up, w_ssm_up, w_out, norm_ffn_g, w_router, b_router,
              w_e_gate, b_e_gate, w_e_up, b_e_up, w_e_down, b_e_down, norm_final_g):
    hp, hs = x_prompt, x_sample
    bp, sp_len = x_prompt.shape[0], x_prompt.shape[1]
    db = x_sample.shape[0]
    kp_l, vp_l, ksm_l, vsm_l = [], [], [], []
    srp_l, sip_l, srs_l, sis_l = [], [], [], []
    for l in range(DEPTH):
        pl = dict(norm_mix_g=norm_mix_g[l], w_in=w_in[l], lambda_q1=lambda_q1[l], lambda_k1=lambda_k1[l],
                  lambda_q2=lambda_q2[l], lambda_k2=lambda_k2[l], subln_g=subln_g[l],
                  ssm_a_re=ssm_a_re[l], ssm_a_im=ssm_a_im[l], ssm_log_dt=ssm_log_dt[l],
                  ssm_b_re=ssm_b_re[l], ssm_b_im=ssm_b_im[l], ssm_c_re=ssm_c_re[l], ssm_c_im=ssm_c_im[l],
                  ssm_d=ssm_d[l], w_glu=w_glu[l], b_glu=b_glu[l], w_attn_up=w_attn_up[l],
                  w_ssm_up=w_ssm_up[l], w_out=w_out[l], norm_ffn_g=norm_ffn_g[l],
                  w_router=w_router[l], b_router=b_router[l], w_e_gate=w_e_gate[l], b_e_gate=b_e_gate[l],
                  w_e_up=w_e_up[l], b_e_up=b_e_up[l], w_e_down=w_e_down[l], b_e_down=b_e_down[l])
        lambda_init = 0.8 - 0.6 * math.exp(-0.3 * l)
        k_past = cache_k[l][page_table].reshape(db, -1, N_HEADS, 2, HEAD_DIM)
        v_past = cache_v[l][page_table].reshape(db, -1, N_HEADS, V_DIM)
        zeros = jnp.zeros((bp, N_GROUPS, STATE_DIM), hp.dtype)
        hp, k1, v1, r1, i1 = block(hp, functools.partial(attend_prompt, rel_bias=rel_bias),
                                   zeros, zeros, pl, lambda_init)
        hs, k2, v2, r2, i2 = block(hs, functools.partial(attend_sample, k_past=k_past, v_past=v_past,
                                                         rel_bias=rel_bias),
                                   state_ssm_re[l], state_ssm_im[l], pl, lambda_init)
        kp_l.append(k1.reshape(bp, sp_len // PAGE_SIZE, PAGE_SIZE, N_HEADS, 2, HEAD_DIM))
        vp_l.append(v1.reshape(bp, sp_len // PAGE_SIZE, PAGE_SIZE, N_HEADS, V_DIM))
        ksm_l.append(k2)
        vsm_l.append(v2)
        srp_l.append(r1)
        sip_l.append(i1)
        srs_l.append(r2)
        sis_l.append(i2)
    y_prompt = rms_norm(hp, norm_final_g)
    y_sample = rms_norm(hs, norm_final_g)
    return (y_prompt, y_sample, jnp.stack(kp_l), jnp.stack(vp_l), jnp.stack(ksm_l), jnp.stack(vsm_l),
            jnp.stack(srp_l), jnp.stack(sip_l), jnp.stack(srs_l), jnp.stack(sis_l))
```

```python
import functools
import math

import numpy as np
import jax
import jax.numpy as jnp
from jax import lax
from jax.experimental import pallas as pl
from jax.experimental.pallas import tpu as pltpu

F32 = jnp.float32
BF16 = jnp.bfloat16
I32 = jnp.int32

EPS = 1e-6
NEG_INF = -1e30
TOP_K = 4
N_BUCKETS = 32
MAX_EXACT = N_BUCKETS // 2
MAX_DISTANCE = 128
SWIGLU_LIMIT = 7.0
SWIGLU_ALPHA = 1.702

SUBLANES = 8
LANES = 128
VMEM_LIMIT = 48 * 1024 * 1024

ATTN_BLOCK = 256
PAGES_PER_STEP = 8
TOKEN_TILE = 512
MOE_TILE = 256
COMBINE_TILE = 128
S5_CHUNK = 512


def _cparams(sem):
    return pltpu.CompilerParams(dimension_semantics=sem, vmem_limit_bytes=VMEM_LIMIT)


def _rms(x, g):
    return x * lax.rsqrt(jnp.mean(x * x, axis=-1, keepdims=True) + EPS) * g


def _in_proj_kernel(x_ref, g_ref, w_ref, q_ref, k_ref, v_ref, u_ref, kb_ref, vb_ref, *, scale):
    xn = _rms(x_ref[...], g_ref[...])
    p = jnp.dot(xn.astype(BF16), w_ref[...], preferred_element_type=F32)
    w4 = p.shape[1] // 4
    q_ref[...] = p[:, :w4] * scale
    k = p[:, w4:2 * w4]
    v = p[:, 2 * w4:3 * w4]
    k_ref[...] = k
    v_ref[...] = v
    u_ref[...] = p[:, 3 * w4:]
    kb_ref[...] = k.astype(BF16)
    vb_ref[...] = v.astype(BF16)


def _in_proj(x, g, w_bf, scale, tm):
    t, d = x.shape
    wn = w_bf.shape[1]
    w4 = wn // 4
    row = lambda i: (i, 0)
    fix = lambda i: (0, 0)
    f32o = jax.ShapeDtypeStruct((t, w4), F32)
    bfo = jax.ShapeDtypeStruct((t, w4), BF16)
    return pl.pallas_call(
        functools.partial(_in_proj_kernel, scale=scale),
        grid=(t // tm,),
        in_specs=[pl.BlockSpec((tm, d), row), pl.BlockSpec((1, d), fix), pl.BlockSpec((d, wn), fix)],
        out_specs=[pl.BlockSpec((tm, w4), row)] * 6,
        out_shape=[f32o, f32o, f32o, f32o, bfo, bfo],
        compiler_params=_cparams(("parallel",)),
        name="in_proj",
    )(x, g, w_bf)


def _bucket_np(n):
    nf = np.maximum(n, 1).astype(np.float64)
    large = MAX_EXACT + np.trunc(
        np.log(nf / MAX_EXACT) / math.log(MAX_DISTANCE / MAX_EXACT) * (N_BUCKETS - MAX_EXACT)).astype(np.int64)
    return np.where(n < MAX_EXACT, n, np.minimum(large, N_BUCKETS - 1)).astype(np.int32)


def _bucket_tile(dist):
    return np.where(dist < 0, -1, _bucket_np(np.maximum(dist, 0))).astype(np.int32)


def _rel_bias_kernel(rb_ref, bkt_ref, o_ref):
    h = pl.program_id(0)
    far = rb_ref[N_BUCKETS - 1, h]
    bkt = bkt_ref[...]
    acc = jnp.zeros(bkt.shape, F32)
    for b in range(N_BUCKETS - 1):
        acc = jnp.where(bkt == b, rb_ref[b, h] - far, acc)
    o_ref[0] = jnp.where(bkt < 0, NEG_INF, acc)


def _rel_bias_tiles(rel_bias, buckets):
    n_heads = rel_bias.shape[1]
    n, r, c = buckets.shape
    return pl.pallas_call(
        _rel_bias_kernel,
        grid=(n_heads,),
        in_specs=[pl.BlockSpec(memory_space=pltpu.SMEM), pl.BlockSpec((n, r, c), lambda h: (0, 0, 0))],
        out_specs=pl.BlockSpec((1, n, r, c), lambda h: (h, 0, 0, 0)),
        out_shape=jax.ShapeDtypeStruct((n_heads, n, r, c), F32),
        compiler_params=_cparams(("arbitrary",)),
        name="rel_bias",
    )(rel_bias, jnp.asarray(buckets))


def _lambda(lq1, lk1, lq2, lk2, lambda_init):
    a = jnp.sum(lq1[...] * lk1[...], axis=-1, keepdims=True)
    b = jnp.sum(lq2[...] * lk2[...], axis=-1, keepdims=True)
    return jnp.exp(a) - jnp.exp(b) + lambda_init


def _attn_p_kernel(lq1, lk1, lq2, lk2, g_ref, q_ref, k_ref, v_ref, b_ref, o_ref, *, blk, hd, lambda_init):
    i = pl.program_id(2)
    lam = _lambda(lq1, lk1, lq2, lk2, lambda_init)
    q = q_ref[...].astype(BF16)
    lane = lax.broadcasted_iota(I32, q.shape, 1)
    zero = jnp.zeros_like(q)
    qq = jnp.concatenate([jnp.where(lane < hd, q, zero), jnp.where(lane >= hd, q, zero)], axis=0)

    def step(j, carry, bias):
        m, l, acc = carry
        off = pl.multiple_of(j * blk, blk)
        kj = k_ref[pl.ds(off, blk), :]
        vj = v_ref[pl.ds(off, blk), :]
        s = lax.dot_general(qq, kj, (((1,), (1,)), ((), ())), preferred_element_type=F32)
        if bias is not None:
            s = (s.reshape(2, blk, blk) + bias[None]).reshape(2 * blk, blk)
        m_new = jnp.maximum(m, jnp.max(s, axis=-1, keepdims=True))
        alpha = jnp.exp(m - m_new)
        p = jnp.exp(s - m_new)
        l = alpha * l + jnp.sum(p, axis=-1, keepdims=True)
        acc = alpha * acc + jnp.dot(p.astype(BF16), vj, preferred_element_type=F32)
        return m_new, l, acc

    carry = (jnp.full((2 * blk, 1), NEG_INF, F32), jnp.zeros((2 * blk, 1), F32),
             jnp.zeros((2 * blk, v_ref.shape[1]), F32))
    carry = lax.fori_loop(0, i - 1, lambda j, c: step(j, c, None), carry)
    carry = lax.cond(i >= 1, lambda c: step(i - 1, c, b_ref[0, 1]), lambda c: c, carry)
    m, l, acc = step(i, carry, b_ref[0, 0])
    o = acc / l
    o = o[:blk] - lam * o[blk:]
    o_ref[...] = (_rms(o, g_ref[...]) * (1.0 - lambda_init)).astype(o_ref.dtype)


def _attn_prompt(lams, subln_g, q, kb, vb, bias, n_batch, seq, n_heads, lambda_init):
    t, w = q.shape
    hw = w // n_heads
    blk = min(ATTN_BLOCK, seq)
    nq = seq // blk
    fix = lambda b, h, i: (0, 0)
    return pl.pallas_call(
        functools.partial(_attn_p_kernel, blk=blk, hd=hw // 2, lambda_init=lambda_init),
        grid=(n_batch, n_heads, nq),
        in_specs=[pl.BlockSpec((1, hw // 2), fix)] * 4 + [
            pl.BlockSpec((1, hw), fix),
            pl.BlockSpec((blk, hw), lambda b, h, i: (b * nq + i, h)),
            pl.BlockSpec((seq, hw), lambda b, h, i: (b, h)),
            pl.BlockSpec((seq, hw), lambda b, h, i: (b, h)),
            pl.BlockSpec((1, 2, blk, blk), lambda b, h, i: (h, 0, 0, 0)),
        ],
        out_specs=pl.BlockSpec((blk, hw), lambda b, h, i: (b * nq + i, h)),
        out_shape=jax.ShapeDtypeStruct((t, w), BF16),
        compiler_params=_cparams(("parallel", "parallel", "arbitrary")),
        name="attn_p",
    )(*lams, subln_g, q, kb, vb, bias)


def _attn_s_kernel(pt_ref, lq1, lk1, lq2, lk2, g_ref, q_ref, kn_ref, vn_ref, b_ref, *rest,
                   pps, n_chunks, n_heads, hd, lambda_init):
    k_refs = rest[:pps]
    v_refs = rest[pps:2 * pps]
    o_ref = rest[2 * pps]
    m_ref, l_ref, acc_ref = rest[2 * pps + 1:]
    j = pl.program_id(1)
    last = n_chunks - 1
    dl = q_ref.shape[0]
    rows = 2 * n_heads * dl
    vd = 2 * hd
    page = k_refs[0].shape[2]

    @pl.when(j == 0)
    def _():
        m_ref[...] = jnp.full(m_ref.shape, NEG_INF, F32)
        l_ref[...] = jnp.zeros(l_ref.shape, F32)
        acc_ref[...] = jnp.zeros(acc_ref.shape, F32)

    q8 = q_ref[...]
    qt = jnp.concatenate([q8] * (2 * n_heads), axis=0)
    lane = lax.broadcasted_iota(I32, qt.shape, 1)
    row = lax.broadcasted_iota(I32, qt.shape, 0)
    qbd = jnp.where(lane // hd == row // dl, qt, 0.0).astype(BF16)

    def scores(k_f32):
        return lax.dot_general(qbd, k_f32.astype(BF16), (((1,), (1,)), ((), ())), preferred_element_type=F32)

    def update(s_list, v_list):
        m_old = m_ref[...]
        m_new = m_old
        for s in s_list:
            m_new = jnp.maximum(m_new, jnp.max(s, axis=-1, keepdims=True))
        alpha = jnp.exp(m_old - m_new)
        l_new = alpha * l_ref[...]
        acc = alpha * acc_ref[...]
        hr = 2 * dl
        parts = [acc[h * hr:(h + 1) * hr] for h in range(n_heads)]
        for s, v in zip(s_list, v_list):
            p = jnp.exp(s - m_new)
            l_new = l_new + jnp.sum(p, axis=-1, keepdims=True)
            pb = p.astype(BF16)
            vb = v.astype(BF16)
            for h in range(n_heads):
                parts[h] = parts[h] + jnp.dot(pb[h * hr:(h + 1) * hr], vb[:, h * vd:(h + 1) * vd],
                                              preferred_element_type=F32)
        m_ref[...] = m_new
        l_ref[...] = l_new
        acc_ref[...] = jnp.concatenate(parts, axis=0)

    s_list = [scores(k_refs[p][0, 0]) for p in range(pps)]
    s_list[-1] = s_list[-1] + jnp.where(j == last, b_ref[0], 0.0)
    update(s_list, [v_refs[p][0, 0] for p in range(pps)])

    @pl.when(j == last)
    def _():
        pad = jnp.zeros((page - dl, kn_ref.shape[1]), F32)
        kn = jnp.concatenate([kn_ref[...], pad], axis=0)
        vn = jnp.concatenate([vn_ref[...], pad], axis=0)
        update([scores(kn) + b_ref[1]], [vn])
        lam = _lambda(lq1, lk1, lq2, lk2, lambda_init)
        o = acc_ref[...] / l_ref[...]
        hr = 2 * dl
        for h in range(n_heads):
            oh = o[h * hr:h * hr + dl] - lam * o[h * hr + dl:(h + 1) * hr]
            o_ref[:, h * vd:(h + 1) * vd] = _rms(oh, g_ref[...]) * (1.0 - lambda_init)


def _attn_sample(page_table, lams, subln_g, q, k_new, v_new, bias, cache_k, cache_v, layer, n_heads,
                 lambda_init):
    db, n_pages = page_table.shape
    t, w = q.shape
    dl = t // db
    page = cache_k.shape[2]
    hd = w // (2 * n_heads)
    pps = math.gcd(PAGES_PER_STEP, n_pages)
    rows = 2 * n_heads * dl
    fix = lambda b, j, pt: (0, 0)
    seq = lambda b, j, pt: (b, 0)

    def page_spec(p):
        return pl.BlockSpec((1, 1, page, w), lambda b, j, pt: (layer, pt[b, j * pps + p], 0, 0))

    grid_spec = pltpu.PrefetchScalarGridSpec(
        num_scalar_prefetch=1,
        grid=(db, n_pages // pps),
        in_specs=[pl.BlockSpec((1, hd), fix)] * 4 + [
            pl.BlockSpec((1, 2 * hd), fix),
            pl.BlockSpec((dl, w), seq), pl.BlockSpec((dl, w), seq), pl.BlockSpec((dl, w), seq),
            pl.BlockSpec((2, rows, page), lambda b, j, pt: (0, 0, 0)),
        ] + [page_spec(p) for p in range(pps)] * 2,
        out_specs=pl.BlockSpec((dl, w), seq),
        scratch_shapes=[pltpu.VMEM((rows, 1), F32), pltpu.VMEM((rows, 1), F32), pltpu.VMEM((rows, 2 * hd), F32)],
    )
    return pl.pallas_call(
        functools.partial(_attn_s_kernel, pps=pps, n_chunks=n_pages // pps, n_heads=n_heads, hd=hd,
                          lambda_init=lambda_init),
        grid_spec=grid_spec,
        out_shape=jax.ShapeDtypeStruct((t, w), F32),
        compiler_params=_cparams(("parallel", "arbitrary")),
        name="attn_s",
    )(page_table, *lams, subln_g, q, k_new, v_new, bias, *([cache_k] * pps), *([cache_v] * pps))


def _s5_prep_kernel(are_ref, aim_ref, ldt_ref, bre_ref, bim_ref, lb_ref, bb_ref, pw_ref, pwc_ref, *, seg):
    dt = jnp.exp(ldt_ref[...])
    a_re = jnp.minimum(are_ref[...], -1e-4)
    a_im = aim_ref[...]

    def lam_pow(n):
        mag = jnp.exp(a_re * dt * n)
        ang = a_im * dt * n
        return mag * jnp.cos(ang), mag * jnp.sin(ang)

    lb_re, lb_im = lam_pow(1.0)
    lb_ref[0] = lb_re
    lb_ref[1] = lb_im
    den = a_re * a_re + a_im * a_im
    nr = lb_re - 1.0
    co_re = (nr * a_re + lb_im * a_im) / den
    co_im = (lb_im * a_re - nr * a_im) / den
    bb_ref[0] = co_re * bre_ref[...] - co_im * bim_ref[...]
    bb_ref[1] = co_re * bim_ref[...] + co_im * bre_ref[...]
    steps = (lax.broadcasted_iota(I32, (seg, 1), 0) + 1).astype(F32)
    p_re, p_im = lam_pow(steps)
    pw_ref[0] = p_re
    pw_ref[1] = p_im
    n_dbl = pwc_ref.shape[1]
    for s in range(n_dbl):
        c_re, c_im = lam_pow(float(seg * (1 << s)))
        pwc_ref[0, s:s + 1, :] = c_re
        pwc_ref[1, s:s + 1, :] = c_im


def _s5_prep(a_re, a_im, log_dt, b_re, b_im, seg):
    g, p = a_re.shape
    c = b_re.shape[-1]
    gp = g * p
    flat = lambda a: a.reshape(1, gp)
    ldt = jnp.broadcast_to(log_dt[:, None], (g, p)).reshape(1, gp)
    chan = lambda b: jnp.transpose(b, (2, 0, 1)).reshape(c, gp)
    n_dbl = 3
    full = lambda shape: pl.BlockSpec(shape, lambda: (0,) * len(shape))
    return pl.pallas_call(
        functools.partial(_s5_prep_kernel, seg=seg),
        in_specs=[full((1, gp))] * 3 + [full((c, gp))] * 2,
        out_specs=[full((2, 1, gp)), full((2, c, gp)), full((2, seg, gp)), full((2, n_dbl, gp))],
        out_shape=[jax.ShapeDtypeStruct((2, 1, gp), F32), jax.ShapeDtypeStruct((2, c, gp), F32),
                   jax.ShapeDtypeStruct((2, seg, gp), F32), jax.ShapeDtypeStruct((2, n_dbl, gp), F32)],
        compiler_params=pltpu.CompilerParams(vmem_limit_bytes=VMEM_LIMIT),
        name="s5_prep",
    )(flat(a_re), flat(a_im), ldt, chan(b_re), chan(b_im))


def _cmul(ar, ai, br, bi):
    return ar * br - ai * bi, ar * bi + ai * br


def _s5_kernel(u_ref, bb_ref, cw_ref, d_ref, lb_ref, pw_ref, pwc_ref, s0_ref, y_ref, st_ref, up_ref, h_ref,
               *, seg, stride, chain):
    cw = lb_ref.shape[2]
    n = SUBLANES
    for i in range(seg):
        up_ref[i * n:(i + 1) * n, :] = u_ref[pl.ds(i, n, stride=stride), :]
    h_ref[...] = jnp.dot(up_ref[...].astype(BF16), bb_ref[0], preferred_element_type=F32)
    lr = jnp.broadcast_to(lb_ref[0], (n, cw))
    li = jnp.broadcast_to(lb_ref[1], (n, cw))

    def scan(i, carry):
        hr, hi = carry
        r = pl.multiple_of(i * n, n)
        tr, ti = _cmul(lr, li, hr, hi)
        hr = tr + h_ref[pl.ds(r, n), :cw]
        hi = ti + h_ref[pl.ds(r, n), cw:]
        h_ref[pl.ds(r, n), :cw] = hr
        h_ref[pl.ds(r, n), cw:] = hi
        return hr, hi

    end_r, end_i = lax.fori_loop(0, seg, scan, (s0_ref[0, 0], s0_ref[0, 1]))

    if chain:
        sub = lax.broadcasted_iota(I32, (n, cw), 0)

        def shift(x, d):
            return jnp.where(sub >= d, pltpu.roll(x, d, axis=0), 0.0)

        tr, ti = end_r, end_i
        for s in range(pwc_ref.shape[1]):
            d = 1 << s
            pr = jnp.broadcast_to(pwc_ref[0, s:s + 1, :], (n, cw))
            pi = jnp.broadcast_to(pwc_ref[1, s:s + 1, :], (n, cw))
            ar, ai = _cmul(pr, pi, shift(tr, d), shift(ti, d))
            tr, ti = tr + ar, ti + ai
        end_r, end_i = tr, ti
        cr, ci = shift(tr, 1), shift(ti, 1)

        def fix(i, _):
            r = pl.multiple_of(i * n, n)
            pr = jnp.broadcast_to(pw_ref[0, pl.ds(i, 1), :], (n, cw))
            pi = jnp.broadcast_to(pw_ref[1, pl.ds(i, 1), :], (n, cw))
            ar, ai = _cmul(pr, pi, cr, ci)
            h_ref[pl.ds(r, n), :cw] = h_ref[pl.ds(r, n), :cw] + ar
            h_ref[pl.ds(r, n), cw:] = h_ref[pl.ds(r, n), cw:] + ai
            return 0

        lax.fori_loop(0, seg, fix, 0)

    st_ref[0, 0, 0] = end_r
    st_ref[0, 0, 1] = end_i
    y = jnp.dot(h_ref[...].astype(BF16), cw_ref[0], preferred_element_type=F32) + d_ref[...] * up_ref[...]
    up_ref[...] = jax.nn.gelu(y)
    for i in range(seg):
        y_ref[pl.ds(i, n, stride=stride), :] = up_ref[i * n:(i + 1) * n, :]


def _s5(u, bb_bd, cw_bd, d_vec, lb, pw, pwc, s0, n_seq, seg, stride, chain):
    t, w = u.shape
    ncc, uw, cw2 = bb_bd.shape
    cw = cw2 // 2
    rows = SUBLANES * seg
    return pl.pallas_call(
        functools.partial(_s5_kernel, seg=seg, stride=stride, chain=chain),
        grid=(n_seq, ncc),
        in_specs=[
            pl.BlockSpec((rows, uw), lambda b, c: (b, c)),
            pl.BlockSpec((1, uw, cw2), lambda b, c: (c, 0, 0)),
            pl.BlockSpec((1, cw2, uw), lambda b, c: (c, 0, 0)),
            pl.BlockSpec((1, uw), lambda b, c: (0, c)),
            pl.BlockSpec((2, 1, cw), lambda b, c: (0, 0, c)),
            pl.BlockSpec((2, pw.shape[1], cw), lambda b, c: (0, 0, c)),
            pl.BlockSpec((2, pwc.shape[1], cw), lambda b, c: (0, 0, c)),
            pl.BlockSpec((1, 2, SUBLANES, cw), lambda b, c: (b, 0, 0, c)),
        ],
        out_specs=[
            pl.BlockSpec((rows, uw), lambda b, c: (b, c)),
            pl.BlockSpec((1, 1, 2, SUBLANES, cw), lambda b, c: (b, c, 0, 0, 0)),
        ],
        out_shape=[jax.ShapeDtypeStruct((t, w), F32),
                   jax.ShapeDtypeStruct((n_seq, ncc, 2, SUBLANES, cw), F32)],
        scratch_shapes=[pltpu.VMEM((rows, uw), F32), pltpu.VMEM((rows, cw2), F32)],
        compiler_params=_cparams(("parallel", "parallel")),
        name="s5_chain" if chain else "s5_step",
    )(u, bb_bd, cw_bd, d_vec, lb, pw, pwc, s0)


def _block_diag_weights(bb, c_re, c_im, cw):
    _, c, gp = bb.shape
    g = c_re.shape[0]
    p = gp // g
    gpc = cw // p
    ncc = g // gpc
    eye = jnp.eye(gpc, dtype=F32)
    b4 = jnp.transpose(bb.reshape(2, c, ncc, gpc, p), (2, 0, 3, 1, 4))
    bbd = b4[:, :, :, :, None, :] * eye[None, None, :, None, :, None]
    bbd = jnp.transpose(bbd.reshape(ncc, 2, gpc * c, cw), (0, 2, 1, 3)).reshape(ncc, gpc * c, 2 * cw)
    cc = jnp.stack([c_re, -c_im]).reshape(2, ncc, gpc, c, p)
    c4 = jnp.transpose(cc, (1, 0, 2, 4, 3))
    cbd = c4[:, :, :, :, None, :] * eye[None, None, :, None, :, None]
    cbd = cbd.reshape(ncc, 2 * cw, gpc * c)
    return bbd.astype(BF16), cbd.astype(BF16)


def _merge_kernel(x_ref, oa_ref, yg_ref, gm_ref, wg_ref, wglu_ref, bglu_ref, wa_ref, ws_ref, wo_ref, gf_ref,
                  wr_ref, br_ref, h_ref, hn_ref, ti_ref, tp_ref):
    x = x_ref[...]
    d = x.shape[1]
    dot = functools.partial(jnp.dot, preferred_element_type=F32)
    xn = _rms(x, gm_ref[...]).astype(BF16)
    gates = jax.nn.sigmoid(dot(xn, wg_ref[...]))
    yg = yg_ref[...]
    ys = yg * jax.nn.sigmoid(dot(yg.astype(BF16), wglu_ref[...]) + bglu_ref[...])
    merged = (gates[:, :d] * dot(oa_ref[...].astype(BF16), wa_ref[...])
              + gates[:, d:] * dot(ys.astype(BF16), ws_ref[...]))
    h = x + dot(merged.astype(BF16), wo_ref[...])
    h_ref[...] = h
    hn = _rms(h, gf_ref[...])
    hn_ref[...] = hn
    logit = lax.dot_general(wr_ref[...], hn.astype(BF16), (((1,), (1,)), ((), ())),
                            preferred_element_type=F32) + br_ref[...]
    n_e = logit.shape[0]
    eid = lax.broadcasted_iota(I32, logit.shape, 0)
    vals, idxs = [], []
    for _ in range(TOP_K):
        mx = jnp.max(logit, axis=0, keepdims=True)
        ix = jnp.min(jnp.where(logit == mx, eid, n_e), axis=0, keepdims=True)
        vals.append(mx)
        idxs.append(ix)
        logit = jnp.where(eid == ix, -jnp.inf, logit)
    ex = [jnp.exp(v - vals[0]) for v in vals]
    tot = ex[0]
    for e in ex[1:]:
        tot = tot + e
    ti_ref[...] = jnp.concatenate(idxs, axis=0)
    tp_ref[...] = jnp.concatenate(ex, axis=0) / tot


def _merge(x, oa, yg, w, tm):
    t, d = x.shape
    aw = oa.shape[1]
    sw = yg.shape[1]
    n_e = w["wr"].shape[0]
    row = lambda i: (i, 0)
    col = lambda i: (0, i)
    fix = lambda i: (0, 0)
    full = lambda a: pl.BlockSpec(a.shape, fix)
    names = ["gm", "wg", "wglu", "bglu", "wa", "ws", "wo", "gf", "wr", "br"]
    return pl.pallas_call(
        _merge_kernel,
        grid=(t // tm,),
        in_specs=[pl.BlockSpec((tm, d), row), pl.BlockSpec((tm, aw), row), pl.BlockSpec((tm, sw), row)]
        + [full(w[n]) for n in names],
        out_specs=[pl.BlockSpec((tm, d), row), pl.BlockSpec((tm, d), row),
                   pl.BlockSpec((TOP_K, tm), col), pl.BlockSpec((TOP_K, tm), col)],
        out_shape=[jax.ShapeDtypeStruct((t, d), F32), jax.ShapeDtypeStruct((t, d), F32),
                   jax.ShapeDtypeStruct((TOP_K, t), I32), jax.ShapeDtypeStruct((TOP_K, t), F32)],
        compiler_params=_cparams(("parallel",)),
        name="merge",
    )(x, oa, yg, *[w[n] for n in names])


def _row_copy(src_hbm, dst_ref, sem, src_row, dst_row):
    return pltpu.make_async_copy(src_hbm.at[pl.ds(src_row, 1), :], dst_ref.at[pl.ds(dst_row, 1), :], sem)


def _start_gather(idx_ref, n, src_hbm, dst_ref, sem):
    def body(r, _):
        _row_copy(src_hbm, dst_ref, sem, idx_ref[0, 0, r], r).start()
        return 0
    lax.fori_loop(0, n, body, 0, unroll=8)


def _wait_gather(n, src_hbm, dst_ref, sem):
    def body(r, _):
        _row_copy(src_hbm, dst_ref, sem, 0, r).wait()
        return 0
    lax.fori_loop(0, n, body, 0, unroll=8)


def _moe_kernel(te_ref, nu_ref, idx_ref, nxt_ref, hn_hbm, wg_ref, bg_ref, wu_ref, bu_ref, wd_ref, bd_ref, o_ref,
                xbuf, sems, wgb, wub, wdb, *, n):
    i = pl.program_id(0)
    tm = xbuf.shape[1]
    slot = lax.rem(i, 2)
    used = nu_ref[0]

    @pl.when(i == 0)
    def _():
        _start_gather(idx_ref, tm, hn_hbm, xbuf.at[0], sems.at[0])

    @pl.when(i + 1 < n)
    def _():
        _start_gather(nxt_ref, tm, hn_hbm, xbuf.at[1 - slot], sems.at[1 - slot])

    e = te_ref[i]
    prev = te_ref[jnp.maximum(i - 1, 0)]

    @pl.when((i == 0) | (e != prev))
    def _():
        wgb[...] = wg_ref[0].astype(BF16)
        wub[...] = wu_ref[0].astype(BF16)
        wdb[...] = wd_ref[0].astype(BF16)

    _wait_gather(tm, hn_hbm, xbuf.at[slot], sems.at[slot])

    @pl.when(i < used)
    def _():
        x = xbuf[slot].astype(BF16)
        gt = jnp.dot(x, wgb[...], preferred_element_type=F32) + bg_ref[0]
        up = jnp.dot(x, wub[...], preferred_element_type=F32) + bu_ref[0]
        gt = jnp.minimum(gt, SWIGLU_LIMIT)
        up = jnp.clip(up, -SWIGLU_LIMIT, SWIGLU_LIMIT)
        hh = (up + 1.0) * (gt * jax.nn.sigmoid(SWIGLU_ALPHA * gt))
        o_ref[...] = jnp.dot(hh.astype(BF16), wdb[...], preferred_element_type=F32) + bd_ref[0]

    @pl.when(i >= used)
    def _():
        o_ref[...] = jnp.zeros(o_ref.shape, F32)


def _moe(tile_expert, n_used, tok_of_slot, hn, wg, bg, wu, bu, wd, bd, tm):
    n_e, d, f = wg.shape
    s_pad = tok_of_slot.shape[0]
    n_tiles = s_pad // tm
    idx3 = tok_of_slot.reshape(n_tiles, 1, tm)
    wmap = lambda i, te, nu: (te[i], 0, 0)
    grid_spec = pltpu.PrefetchScalarGridSpec(
        num_scalar_prefetch=2,
        grid=(n_tiles,),
        in_specs=[
            pl.BlockSpec((1, 1, tm), lambda i, te, nu: (i, 0, 0), memory_space=pltpu.SMEM),
            pl.BlockSpec((1, 1, tm), lambda i, te, nu: (jnp.minimum(i + 1, n_tiles - 1), 0, 0),
                         memory_space=pltpu.SMEM),
            pl.BlockSpec(memory_space=pl.ANY),
            pl.BlockSpec((1, d, f), wmap), pl.BlockSpec((1, 1, f), wmap),
            pl.BlockSpec((1, d, f), wmap), pl.BlockSpec((1, 1, f), wmap),
            pl.BlockSpec((1, f, d), wmap), pl.BlockSpec((1, 1, d), wmap),
        ],
        out_specs=pl.BlockSpec((tm, d), lambda i, te, nu: (i, 0)),
        scratch_shapes=[pltpu.VMEM((2, tm, d), F32), pltpu.SemaphoreType.DMA((2,)),
                        pltpu.VMEM((d, f), BF16), pltpu.VMEM((d, f), BF16), pltpu.VMEM((f, d), BF16)],
    )
    return pl.pallas_call(
        functools.partial(_moe_kernel, n=n_tiles),
        grid_spec=grid_spec,
        out_shape=jax.ShapeDtypeStruct((s_pad, d), F32),
        compiler_params=pltpu.CompilerParams(dimension_semantics=("arbitrary",),
                                             vmem_limit_bytes=56 * 1024 * 1024),
        name="moe",
    )(tile_expert, n_used, idx3, idx3, hn, wg, bg.reshape(n_e, 1, f), wu, bu.reshape(n_e, 1, f),
      wd, bd.reshape(n_e, 1, d))


def _combine_kernel(idx_ref, nxt_ref, h_ref, p_ref, g_ref, o_hbm, y_ref, obuf, sems, *, n):
    i = pl.program_id(0)
    rows = obuf.shape[1]
    tm = h_ref.shape[0]
    slot = lax.rem(i, 2)

    @pl.when(i == 0)
    def _():
        _start_gather(idx_ref, rows, o_hbm, obuf.at[0], sems.at[0])

    @pl.when(i + 1 < n)
    def _():
        _start_gather(nxt_ref, rows, o_hbm, obuf.at[1 - slot], sems.at[1 - slot])

    _wait_gather(rows, o_hbm, obuf.at[slot], sems.at[slot])
    p = p_ref[...]
    acc = h_ref[...]
    for k in range(TOP_K):
        acc = acc + p[:, k:k + 1] * obuf[slot, k * tm:(k + 1) * tm, :]
    y_ref[...] = _rms(acc, g_ref[...])


def _combine(slots, h, probs_t, g, o_sorted, tm):
    t, d = h.shape
    n_tiles = t // tm
    rows = TOP_K * tm
    grid_spec = pltpu.PrefetchScalarGridSpec(
        num_scalar_prefetch=0,
        grid=(n_tiles,),
        in_specs=[
            pl.BlockSpec((1, 1, rows), lambda i: (i, 0, 0), memory_space=pltpu.SMEM),
            pl.BlockSpec((1, 1, rows), lambda i: (jnp.minimum(i + 1, n_tiles - 1), 0, 0),
                         memory_space=pltpu.SMEM),
            pl.BlockSpec((tm, d), lambda i: (i, 0)),
            pl.BlockSpec((tm, TOP_K), lambda i: (i, 0)),
            pl.BlockSpec((1, d), lambda i: (0, 0)),
            pl.BlockSpec(memory_space=pl.ANY),
        ],
        out_specs=pl.BlockSpec((tm, d), lambda i: (i, 0)),
        scratch_shapes=[pltpu.VMEM((2, rows, d), F32), pltpu.SemaphoreType.DMA((2,))],
    )
    return pl.pallas_call(
        functools.partial(_combine_kernel, n=n_tiles),
        grid_spec=grid_spec,
        out_shape=jax.ShapeDtypeStruct((t, d), F32),
        compiler_params=_cparams(("arbitrary",)),
        name="combine",
    )(slots, slots, h, probs_t, g, o_sorted)


def _routing(top_i, n_experts, tm):
    k, t = top_i.shape
    e_flat = top_i.reshape(-1)
    n_pairs = k * t
    counts = jnp.zeros((n_experts,), I32).at[e_flat].add(1)
    padded = ((counts + tm - 1) // tm) * tm
    pad_end = jnp.cumsum(padded)
    pad_start = pad_end - padded
    cnt_start = jnp.cumsum(counts) - counts
    order = jnp.argsort(e_flat, stable=True)
    pos = jnp.zeros((n_pairs,), I32).at[order].set(jnp.arange(n_pairs, dtype=I32))
    slot = pad_start[e_flat] + pos - cnt_start[e_flat]
    s_pad = n_pairs + n_experts * tm
    n_tiles = s_pad // tm
    tok = jnp.tile(jnp.arange(t, dtype=I32), k)
    tok_of_slot = jnp.zeros((s_pad,), I32).at[slot].set(tok)
    tile_start = jnp.arange(n_tiles, dtype=I32) * tm
    tile_expert = jnp.minimum(jnp.searchsorted(pad_end, tile_start, side="right"), n_experts - 1).astype(I32)
    n_used = (pad_end[-1] // tm).astype(I32).reshape(1)
    return slot.reshape(k, t), tok_of_slot, tile_expert, n_used


def _tile_slots(slot_kt, tm):
    k, t = slot_kt.shape
    return jnp.transpose(slot_kt.reshape(k, t // tm, tm), (1, 0, 2)).reshape(t // tm, 1, k * tm)


def kernel(x_prompt, x_sample, cache_k, cache_v, state_ssm_re, state_ssm_im, page_table, norm_mix_g, w_in, lambda_q1, lambda_k1, lambda_q2, lambda_k2, subln_g, rel_bias, ssm_a_re, ssm_a_im, ssm_log_dt, ssm_b_re, ssm_b_im, ssm_c_re, ssm_c_im, ssm_d, w_glu, b_glu, w_attn_up, w_ssm_up, w_out, norm_ffn_g, w_router, b_router, w_e_gate, b_e_gate, w_e_up, b_e_up, w_e_down, b_e_down, norm_final_g):
    bp, seq, d = x_prompt.shape
    db, dl, _ = x_sample.shape
    depth, n_pool, page, n_heads, _, hd = cache_k.shape
    vd = cache_v.shape[-1]
    n_groups, n_state = ssm_a_re.shape[1:]
    gch = ssm_b_re.shape[-1]
    n_experts = w_router.shape[-1]
    qk_w = n_heads * 2 * hd
    attn_w = n_heads * vd
    ssm_w = n_groups * gch
    mix_w = 2 * qk_w + attn_w + ssm_w
    assert qk_w == attn_w == ssm_w and vd == 2 * hd
    assert seq % SUBLANES == 0 and db % SUBLANES == 0 and dl == SUBLANES
    tp, ts = bp * seq, db * dl
    scale = hd ** -0.5
    seg = seq // SUBLANES
    cw = min(S5_CHUNK, n_groups * n_state)

    ck = cache_k.reshape(depth, n_pool, page, qk_w)
    cv = cache_v.reshape(depth, n_pool, page, attn_w)

    blk = min(ATTN_BLOCK, seq)
    qi = np.arange(blk)[:, None]
    ki = np.arange(blk)[None, :]
    bkt_p = np.stack([_bucket_tile(qi - ki), _bucket_tile(blk + qi - ki)])
    sq = np.tile(np.arange(dl), 2)[:, None]
    sk = np.arange(page)[None, :]
    new_dist = np.where(sk < dl, sq - sk, -1)
    bkt_s = np.stack([_bucket_tile(page + sq - sk), _bucket_tile(new_dist)])

    hp = x_prompt.reshape(tp, d)
    hs = x_sample.reshape(ts, d)
    outs = {n: [] for n in ("kp", "vp", "ks", "vs", "rp", "ip", "rs", "is")}
    tm_p = math.gcd(TOKEN_TILE, tp)
    tm_s = math.gcd(TOKEN_TILE, ts)
    for l in range(depth):
        lambda_init = 0.8 - 0.6 * math.exp(-0.3 * l)
        w_l = w_in[l].astype(BF16)
        g_mix = norm_mix_g[l].reshape(1, d)
        lams = [a[l].reshape(1, hd) for a in (lambda_q1, lambda_k1, lambda_q2, lambda_k2)]
        sub_g = subln_g[l].reshape(1, vd)

        qp, kp, vp, up, kpb, vpb = _in_proj(hp, g_mix, w_l[:, :mix_w], scale, tm_p)
        qs, ks, vs, us, _, _ = _in_proj(hs, g_mix, w_l[:, :mix_w], scale, tm_s)

        bias_p = _rel_bias_tiles(rel_bias, bkt_p)
        bias_s = _rel_bias_tiles(rel_bias, bkt_s)
        bias_s = jnp.transpose(bias_s, (1, 0, 2, 3)).reshape(2, n_heads * 2 * dl, page)

        oa_p = _attn_prompt(lams, sub_g, qp, kpb, vpb, bias_p, bp, seq, n_heads, lambda_init)
        oa_s = _attn_sample(page_table, lams, sub_g, qs, ks, vs, bias_s, ck, cv, l, n_heads, lambda_init)

        lb, bb, pw, pwc = _s5_prep(ssm_a_re[l], ssm_a_im[l], ssm_log_dt[l], ssm_b_re[l], ssm_b_im[l], seg)
        bb_bd, cw_bd = _block_diag_weights(bb, ssm_c_re[l], ssm_c_im[l], cw)
        d_vec = ssm_d[l].reshape(1, ssm_w)
        zeros = jnp.zeros((bp, 2, SUBLANES, n_groups * n_state), F32)
        yg_p, st_p = _s5(up, bb_bd, cw_bd, d_vec, lb, pw, pwc, zeros, bp, seg, seg, True)
        s0 = jnp.stack([state_ssm_re[l], state_ssm_im[l]], axis=1).reshape(
            db // SUBLANES, SUBLANES, 2, n_groups * n_state)
        s0 = jnp.transpose(s0, (0, 2, 1, 3))
        yg_s, st_s = _s5(us, bb_bd, cw_bd, d_vec, lb, pw, pwc, s0, db // SUBLANES, dl, dl, False)

        wts = dict(
            gm=g_mix, wg=w_l[:, mix_w:], wglu=w_glu[l].astype(BF16), bglu=b_glu[l].reshape(1, ssm_w),
            wa=w_attn_up[l].astype(BF16), ws=w_ssm_up[l].astype(BF16), wo=w_out[l].astype(BF16),
            gf=norm_ffn_g[l].reshape(1, d), wr=jnp.transpose(w_router[l]).astype(BF16),
            br=b_router[l].reshape(n_experts, 1))
        h_p, hn_p, ti_p, pr_p = _merge(hp, oa_p, yg_p, wts, tm_p)
        h_s, hn_s, ti_s, pr_s = _merge(hs, oa_s, yg_s, wts, tm_s)

        hn = jnp.concatenate([hn_p, hn_s], axis=0)
        top_i = jnp.concatenate([ti_p, ti_s], axis=1)
        slot_kt, tok_of_slot, tile_expert, n_used = _routing(top_i, n_experts, MOE_TILE)
        o_sorted = _moe(tile_expert, n_used, tok_of_slot, hn, w_e_gate[l], b_e_gate[l], w_e_up[l], b_e_up[l],
                        w_e_down[l], b_e_down[l], MOE_TILE)

        last = l == depth - 1
        g_out = norm_final_g.reshape(1, d) if last else None
        assert last, "multi-layer stacks need a combine variant without the final norm"
        ct_p = math.gcd(COMBINE_TILE, tp)
        ct_s = math.gcd(COMBINE_TILE, ts)
        hp = _combine(_tile_slots(slot_kt[:, :tp], ct_p), h_p, jnp.transpose(pr_p), g_out, o_sorted, ct_p)
        hs = _combine(_tile_slots(slot_kt[:, tp:], ct_s), h_s, jnp.transpose(pr_s), g_out, o_sorted, ct_s)

        gp = n_groups * n_state
        st_p = jnp.transpose(st_p, (0, 2, 3, 1, 4)).reshape(bp, 2, SUBLANES, gp)[:, :, SUBLANES - 1]
        st_s = jnp.transpose(st_s, (0, 3, 2, 1, 4)).reshape(db, 2, gp)
        outs["kp"].append(kp.reshape(bp, seq // page, page, n_heads, 2, hd))
        outs["vp"].append(vp.reshape(bp, seq // page, page, n_heads, vd))
        outs["ks"].append(ks.reshape(db, dl, n_heads, 2, hd))
        outs["vs"].append(vs.reshape(db, dl, n_heads, vd))
        outs["rp"].append(st_p[:, 0].reshape(bp, n_groups, n_state))
        outs["ip"].append(st_p[:, 1].reshape(bp, n_groups, n_state))
        outs["rs"].append(st_s[:, 0].reshape(db, n_groups, n_state))
        outs["is"].append(st_s[:, 1].reshape(db, n_groups, n_state))

    st = lambda n: jnp.stack(outs[n])
    return (hp.reshape(bp, seq, d), hs.reshape(db, dl, d), st("kp"), st("vp"), st("ks"), st("vs"),
            st("rp"), st("ip"), st("rs"), st("is"))
```

```python
import functools
import math

import numpy as np
import jax
import jax.numpy as jnp
from jax import lax
from jax.experimental import pallas as pl
from jax.experimental.pallas import tpu as pltpu

F32 = jnp.float32
BF16 = jnp.bfloat16
I32 = jnp.int32

EPS = 1e-6
NEG_INF = -1e30
TOP_K = 4
N_BUCKETS = 32
MAX_EXACT = N_BUCKETS // 2
MAX_DISTANCE = 128
SWIGLU_LIMIT = 7.0
SWIGLU_ALPHA = 1.702

SUBLANES = 8
LANES = 128
VMEM_LIMIT = 48 * 1024 * 1024

ATTN_BLOCK = 256
PAGES_PER_STEP = 16
TOKEN_TILE = 512
MOE_TILE = 256
ROUTE_TILE = 256
S5_CHUNK = 512


def _cparams(sem):
    return pltpu.CompilerParams(dimension_semantics=sem, vmem_limit_bytes=VMEM_LIMIT)


def _rms(x, g):
    return x * lax.rsqrt(jnp.mean(x * x, axis=-1, keepdims=True) + EPS) * g


def _in_proj_kernel(x_ref, g_ref, w_ref, wkt_ref, q_ref, k_ref, v_ref, u_ref, kb_ref, vb_ref, *, scale, page):
    xn = _rms(x_ref[...], g_ref[...]).astype(BF16)
    p = jnp.dot(xn, w_ref[...], preferred_element_type=F32)
    w3 = p.shape[1] // 3
    q_ref[...] = p[:, :w3] * scale
    v = p[:, w3:2 * w3]
    v_ref[...] = v
    vb_ref[...] = v.astype(BF16)
    u_ref[...] = p[:, 2 * w3:]
    kt = lax.dot_general(wkt_ref[...], xn, (((1,), (1,)), ((), ())), preferred_element_type=F32)
    if page is None:
        k_ref[...] = kt
        kb_ref[...] = kt.astype(BF16)
    else:
        for pg in range(kt.shape[1] // page):
            tile = kt[:, pg * page:(pg + 1) * page]
            k_ref[pg] = tile
            kb_ref[pg] = tile.astype(BF16)


def _in_proj(x, g, w_qvu, w_kt, scale, tm, page):
    t, d = x.shape
    w3 = w_qvu.shape[1] // 3
    row = lambda i: (i, 0)
    fix = lambda i: (0, 0)
    f32o = jax.ShapeDtypeStruct((t, w3), F32)
    if page is None:
        kt_shape, kt_spec = (w3, t), pl.BlockSpec((w3, tm), lambda i: (0, i))
    else:
        kt_shape, kt_spec = (t // page, w3, page), pl.BlockSpec((tm // page, w3, page), lambda i: (i, 0, 0))
    tok = pl.BlockSpec((tm, w3), row)
    return pl.pallas_call(
        functools.partial(_in_proj_kernel, scale=scale, page=page),
        grid=(t // tm,),
        in_specs=[pl.BlockSpec((tm, d), row), pl.BlockSpec((1, d), fix), pl.BlockSpec((d, 3 * w3), fix),
                  pl.BlockSpec((w3, d), fix)],
        out_specs=[tok, kt_spec, tok, tok, kt_spec, tok],
        out_shape=[f32o, jax.ShapeDtypeStruct(kt_shape, F32), f32o, f32o,
                   jax.ShapeDtypeStruct(kt_shape, BF16), jax.ShapeDtypeStruct((t, w3), BF16)],
        compiler_params=_cparams(("parallel",)),
        name="in_proj",
    )(x, g, w_qvu, w_kt)


def _bucket_np(n):
    nf = np.maximum(n, 1).astype(np.float64)
    large = MAX_EXACT + np.trunc(
        np.log(nf / MAX_EXACT) / math.log(MAX_DISTANCE / MAX_EXACT) * (N_BUCKETS - MAX_EXACT)).astype(np.int64)
    return np.where(n < MAX_EXACT, n, np.minimum(large, N_BUCKETS - 1)).astype(np.int32)


def _bucket_tile(dist):
    return np.where(dist < 0, -1, _bucket_np(np.maximum(dist, 0))).astype(np.int32)


def _rel_bias_kernel(rb_ref, bkt_ref, o_ref):
    h = pl.program_id(0)
    far = rb_ref[N_BUCKETS - 1, h]
    bkt = bkt_ref[...]
    acc = jnp.zeros(bkt.shape, F32)
    for b in range(N_BUCKETS - 1):
        acc = jnp.where(bkt == b, rb_ref[b, h] - far, acc)
    o_ref[0] = jnp.where(bkt < 0, NEG_INF, acc)


def _rel_bias_tiles(rel_bias, buckets):
    n_heads = rel_bias.shape[1]
    n, r, c = buckets.shape
    return pl.pallas_call(
        _rel_bias_kernel,
        grid=(n_heads,),
        in_specs=[pl.BlockSpec(memory_space=pltpu.SMEM), pl.BlockSpec((n, r, c), lambda h: (0, 0, 0))],
        out_specs=pl.BlockSpec((1, n, r, c), lambda h: (h, 0, 0, 0)),
        out_shape=jax.ShapeDtypeStruct((n_heads, n, r, c), F32),
        compiler_params=_cparams(("arbitrary",)),
        name="rel_bias",
    )(rel_bias, jnp.asarray(buckets))


def _lambda(lq1, lk1, lq2, lk2, lambda_init):
    a = jnp.sum(lq1[...] * lk1[...], axis=-1, keepdims=True)
    b = jnp.sum(lq2[...] * lk2[...], axis=-1, keepdims=True)
    return jnp.exp(a) - jnp.exp(b) + lambda_init


def _attn_p_kernel(lq1, lk1, lq2, lk2, g_ref, q_ref, k_ref, v_ref, b_ref, o_ref, qq_ref, m_ref, l_ref, acc_ref,
                   *, blk, page, n_heads, hd, lambda_init):
    i = pl.program_id(1)
    hw = 2 * hd
    ppb = blk // page
    lam = _lambda(lq1, lk1, lq2, lk2, lambda_init)
    lane = lax.broadcasted_iota(I32, (blk, hw), 1)
    for h in range(n_heads):
        q = q_ref[:, h * hw:(h + 1) * hw].astype(BF16)
        zero = jnp.zeros_like(q)
        qq_ref[h, :blk] = jnp.where(lane < hd, q, zero)
        qq_ref[h, blk:] = jnp.where(lane >= hd, q, zero)
    m_ref[...] = jnp.full(m_ref.shape, NEG_INF, F32)
    l_ref[...] = jnp.zeros(l_ref.shape, F32)
    acc_ref[...] = jnp.zeros(acc_ref.shape, F32)

    def step(j, bias_idx):
        off = pl.multiple_of(j * blk, blk)
        for h in range(n_heads):
            kt = jnp.concatenate([k_ref[j * ppb + p, h * hw:(h + 1) * hw, :] for p in range(ppb)], axis=1)
            vj = v_ref[pl.ds(off, blk), h * hw:(h + 1) * hw]
            s = jnp.dot(qq_ref[h], kt, preferred_element_type=F32)
            if bias_idx is not None:
                s = (s.reshape(2, blk, blk) + b_ref[h, bias_idx][None]).reshape(2 * blk, blk)
            m = m_ref[h]
            m_new = jnp.maximum(m, jnp.max(s, axis=-1, keepdims=True))
            alpha = jnp.exp(m - m_new)
            p = jnp.exp(s - jnp.concatenate([m_new] * (blk // LANES), axis=1))
            l_ref[h] = alpha * l_ref[h] + jnp.sum(p, axis=-1, keepdims=True)
            acc_ref[h] = alpha * acc_ref[h] + jnp.dot(p.astype(BF16), vj, preferred_element_type=F32)
            m_ref[h] = m_new

    def far(j, c):
        step(j, None)
        return c

    lax.fori_loop(0, i - 1, far, 0)

    @pl.when(i >= 1)
    def _():
        step(i - 1, 1)

    step(i, 0)
    for h in range(n_heads):
        o = acc_ref[h] / l_ref[h]
        o = o[:blk] - lam * o[blk:]
        o_ref[:, h * hw:(h + 1) * hw] = (_rms(o, g_ref[...]) * (1.0 - lambda_init)).astype(o_ref.dtype)


def _attn_prompt(lams, subln_g, q, ktb, vb, bias, n_batch, seq, n_heads, lambda_init):
    t, w = q.shape
    page = ktb.shape[2]
    hw = w // n_heads
    blk = min(ATTN_BLOCK, seq)
    nq = seq // blk
    fix = lambda b, i: (0, 0)
    return pl.pallas_call(
        functools.partial(_attn_p_kernel, blk=blk, page=page, n_heads=n_heads, hd=hw // 2,
                          lambda_init=lambda_init),
        grid=(n_batch, nq),
        in_specs=[pl.BlockSpec((1, hw // 2), fix)] * 4 + [
            pl.BlockSpec((1, hw), fix),
            pl.BlockSpec((blk, w), lambda b, i: (b * nq + i, 0)),
            pl.BlockSpec((seq // page, w, page), lambda b, i: (b, 0, 0)),
            pl.BlockSpec((seq, w), lambda b, i: (b, 0)),
            pl.BlockSpec((n_heads, 2, blk, blk), lambda b, i: (0, 0, 0, 0)),
        ],
        out_specs=pl.BlockSpec((blk, w), lambda b, i: (b * nq + i, 0)),
        out_shape=jax.ShapeDtypeStruct((t, w), BF16),
        scratch_shapes=[pltpu.VMEM((n_heads, 2 * blk, hw), BF16), pltpu.VMEM((n_heads, 2 * blk, LANES), F32),
                        pltpu.VMEM((n_heads, 2 * blk, LANES), F32), pltpu.VMEM((n_heads, 2 * blk, hw), F32)],
        compiler_params=_cparams(("parallel", "parallel")),
        name="attn_p",
    )(*lams, subln_g, q, ktb, vb, bias)


def _attn_s_kernel(pt_ref, lq1, lk1, lq2, lk2, g_ref, q_ref, kn_ref, vn_ref, b_ref, *rest,
                   pps, n_chunks, n_heads, hd, lambda_init):
    k_refs = rest[:pps]
    v_refs = rest[pps:2 * pps]
    o_ref = rest[2 * pps]
    m_ref, l_ref, acc_ref, kbf_ref, vbf_ref = rest[2 * pps + 1:]
    j = pl.program_id(1)
    last = n_chunks - 1
    dl = q_ref.shape[0]
    vd = 2 * hd
    hr = 2 * dl
    page = k_refs[0].shape[3]

    @pl.when(j == 0)
    def _():
        m_ref[...] = jnp.full(m_ref.shape, NEG_INF, F32)
        l_ref[...] = jnp.zeros(l_ref.shape, F32)
        acc_ref[...] = jnp.zeros(acc_ref.shape, F32)

    q8 = q_ref[...]
    qt = jnp.concatenate([q8] * (2 * n_heads), axis=0)
    lane = lax.broadcasted_iota(I32, qt.shape, 1)
    row = lax.broadcasted_iota(I32, qt.shape, 0)
    qbd = jnp.where(lane // hd == row // dl, qt, 0.0).astype(BF16)

    def update(s_list, v_heads):
        m_old = m_ref[...]
        m_new = m_old
        for s in s_list:
            m_new = jnp.maximum(m_new, jnp.max(s, axis=-1, keepdims=True))
        alpha = jnp.exp(m_old - m_new)
        l_new = alpha * l_ref[...]
        acc = alpha * acc_ref[...]
        parts = [acc[h * hr:(h + 1) * hr] for h in range(n_heads)]
        for s, v_of in zip(s_list, v_heads):
            p = jnp.exp(s - m_new)
            l_new = l_new + jnp.sum(p, axis=-1, keepdims=True)
            pb = p.astype(BF16)
            for h in range(n_heads):
                parts[h] = parts[h] + jnp.dot(pb[h * hr:(h + 1) * hr], v_of(h).astype(BF16),
                                              preferred_element_type=F32)
        m_ref[...] = m_new
        l_ref[...] = l_new
        acc_ref[...] = jnp.concatenate(parts, axis=0)

    for p in range(pps):
        kbf_ref[:, p * page:(p + 1) * page] = k_refs[p][0, 0].astype(BF16)
        for h in range(n_heads):
            vbf_ref[h, p * page:(p + 1) * page, :] = v_refs[p][0, 0, pl.ds(h, page, stride=n_heads), :].astype(BF16)
    s = jnp.dot(qbd, kbf_ref[...], preferred_element_type=F32)
    tail = s[:, (pps - 1) * page:] + jnp.where(j == last, b_ref[0], 0.0)
    s = jnp.concatenate([s[:, :(pps - 1) * page], tail], axis=1) if pps > 1 else tail
    update([s], [lambda h: vbf_ref[h]])

    @pl.when(j == last)
    def _():
        pad = jnp.zeros((page - dl, kn_ref.shape[1]), F32)
        kn = jnp.concatenate([kn_ref[...], pad], axis=0).astype(BF16)
        vn = jnp.concatenate([vn_ref[...], pad], axis=0)
        s_new = lax.dot_general(qbd, kn, (((1,), (1,)), ((), ())), preferred_element_type=F32) + b_ref[1]
        update([s_new], [lambda h: vn[:, h * vd:(h + 1) * vd]])
        lam = _lambda(lq1, lk1, lq2, lk2, lambda_init)
        o = acc_ref[...] / l_ref[...]
        for h in range(n_heads):
            oh = o[h * hr:h * hr + dl] - lam * o[h * hr + dl:(h + 1) * hr]
            o_ref[:, h * vd:(h + 1) * vd] = _rms(oh, g_ref[...]) * (1.0 - lambda_init)


def _attn_sample(page_table, lams, subln_g, q, k_new, v_new, bias, cache_kt, cache_v, layer, n_heads,
                 lambda_init):
    db, n_pages = page_table.shape
    t, w = q.shape
    dl = t // db
    page = cache_kt.shape[3]
    hd = w // (2 * n_heads)
    pps = math.gcd(PAGES_PER_STEP, n_pages)
    rows = 2 * n_heads * dl
    fix = lambda b, j, pt: (0, 0)
    seq = lambda b, j, pt: (b, 0)

    def page_spec(p, shape):
        return pl.BlockSpec((1, 1) + shape, lambda b, j, pt: (layer, pt[b, j * pps + p], 0, 0))

    grid_spec = pltpu.PrefetchScalarGridSpec(
        num_scalar_prefetch=1,
        grid=(db, n_pages // pps),
        in_specs=[pl.BlockSpec((1, hd), fix)] * 4 + [
            pl.BlockSpec((1, 2 * hd), fix),
            pl.BlockSpec((dl, w), seq), pl.BlockSpec((dl, w), seq), pl.BlockSpec((dl, w), seq),
            pl.BlockSpec((2, rows, page), lambda b, j, pt: (0, 0, 0)),
        ] + [page_spec(p, (w, page)) for p in range(pps)]
        + [page_spec(p, (page * n_heads, 2 * hd)) for p in range(pps)],
        out_specs=pl.BlockSpec((dl, w), seq),
        scratch_shapes=[pltpu.VMEM((rows, 1), F32), pltpu.VMEM((rows, 1), F32), pltpu.VMEM((rows, 2 * hd), F32),
                        pltpu.VMEM((w, pps * page), BF16), pltpu.VMEM((n_heads, pps * page, 2 * hd), BF16)],
    )
    return pl.pallas_call(
        functools.partial(_attn_s_kernel, pps=pps, n_chunks=n_pages // pps, n_heads=n_heads, hd=hd,
                          lambda_init=lambda_init),
        grid_spec=grid_spec,
        out_shape=jax.ShapeDtypeStruct((t, w), F32),
        compiler_params=_cparams(("parallel", "arbitrary")),
        name="attn_s",
    )(page_table, *lams, subln_g, q, k_new, v_new, bias, *([cache_kt] * pps), *([cache_v] * pps))


def _s5_prep_kernel(are_ref, aim_ref, ldt_ref, bre_ref, bim_ref, lb_ref, bb_ref, pw_ref, pwc_ref, *, seg):
    dt = jnp.exp(ldt_ref[...])
    a_re = jnp.minimum(are_ref[...], -1e-4)
    a_im = aim_ref[...]

    def lam_pow(n):
        mag = jnp.exp(a_re * dt * n)
        ang = a_im * dt * n
        return mag * jnp.cos(ang), mag * jnp.sin(ang)

    lb_re, lb_im = lam_pow(1.0)
    lb_ref[0] = lb_re
    lb_ref[1] = lb_im
    den = a_re * a_re + a_im * a_im
    nr = lb_re - 1.0
    co_re = (nr * a_re + lb_im * a_im) / den
    co_im = (lb_im * a_re - nr * a_im) / den
    bb_ref[0] = co_re * bre_ref[...] - co_im * bim_ref[...]
    bb_ref[1] = co_re * bim_ref[...] + co_im * bre_ref[...]
    steps = (lax.broadcasted_iota(I32, (seg, 1), 0) + 1).astype(F32)
    p_re, p_im = lam_pow(steps)
    pw_ref[0] = p_re
    pw_ref[1] = p_im
    n_dbl = pwc_ref.shape[1]
    for s in range(n_dbl):
        c_re, c_im = lam_pow(float(seg * (1 << s)))
        pwc_ref[0, s:s + 1, :] = c_re
        pwc_ref[1, s:s + 1, :] = c_im


def _s5_prep(a_re, a_im, log_dt, b_re, b_im, seg):
    g, p = a_re.shape
    c = b_re.shape[-1]
    gp = g * p
    flat = lambda a: a.reshape(1, gp)
    ldt = jnp.broadcast_to(log_dt[:, None], (g, p)).reshape(1, gp)
    chan = lambda b: jnp.transpose(b, (2, 0, 1)).reshape(c, gp)
    n_dbl = 3
    full = lambda shape: pl.BlockSpec(shape, lambda: (0,) * len(shape))
    return pl.pallas_call(
        functools.partial(_s5_prep_kernel, seg=seg),
        in_specs=[full((1, gp))] * 3 + [full((c, gp))] * 2,
        out_specs=[full((2, 1, gp)), full((2, c, gp)), full((2, seg, gp)), full((2, n_dbl, gp))],
        out_shape=[jax.ShapeDtypeStruct((2, 1, gp), F32), jax.ShapeDtypeStruct((2, c, gp), F32),
                   jax.ShapeDtypeStruct((2, seg, gp), F32), jax.ShapeDtypeStruct((2, n_dbl, gp), F32)],
        compiler_params=pltpu.CompilerParams(vmem_limit_bytes=VMEM_LIMIT),
        name="s5_prep",
    )(flat(a_re), flat(a_im), ldt, chan(b_re), chan(b_im))


def _cmul(ar, ai, br, bi):
    return ar * br - ai * bi, ar * bi + ai * br


def _s5_kernel(u_ref, bb_ref, cw_ref, d_ref, lb_ref, pw_ref, pwc_ref, s0_ref, y_ref, st_ref, up_ref, h_ref,
               *, seg, stride, chain):
    cw = lb_ref.shape[2]
    n = SUBLANES
    for i in range(seg):
        up_ref[i * n:(i + 1) * n, :] = u_ref[pl.ds(i, n, stride=stride), :]
    h_ref[...] = jnp.dot(up_ref[...].astype(BF16), bb_ref[0], preferred_element_type=F32)
    lr = jnp.broadcast_to(lb_ref[0], (n, cw))
    li = jnp.broadcast_to(lb_ref[1], (n, cw))

    def scan(i, carry):
        hr, hi = carry
        r = pl.multiple_of(i * n, n)
        tr, ti = _cmul(lr, li, hr, hi)
        hr = tr + h_ref[pl.ds(r, n), :cw]
        hi = ti + h_ref[pl.ds(r, n), cw:]
        h_ref[pl.ds(r, n), :cw] = hr
        h_ref[pl.ds(r, n), cw:] = hi
        return hr, hi

    end_r, end_i = lax.fori_loop(0, seg, scan, (s0_ref[0, 0], s0_ref[0, 1]))

    if chain:
        sub = lax.broadcasted_iota(I32, (n, cw), 0)

        def shift(x, d):
            return jnp.where(sub >= d, pltpu.roll(x, d, axis=0), 0.0)

        tr, ti = end_r, end_i
        for s in range(pwc_ref.shape[1]):
            d = 1 << s
            pr = jnp.broadcast_to(pwc_ref[0, s:s + 1, :], (n, cw))
            pi = jnp.broadcast_to(pwc_ref[1, s:s + 1, :], (n, cw))
            ar, ai = _cmul(pr, pi, shift(tr, d), shift(ti, d))
            tr, ti = tr + ar, ti + ai
        end_r, end_i = tr, ti
        cr, ci = shift(tr, 1), shift(ti, 1)

        def fix(i, _):
            r = pl.multiple_of(i * n, n)
            pr = jnp.broadcast_to(pw_ref[0, pl.ds(i, 1), :], (n, cw))
            pi = jnp.broadcast_to(pw_ref[1, pl.ds(i, 1), :], (n, cw))
            ar, ai = _cmul(pr, pi, cr, ci)
            h_ref[pl.ds(r, n), :cw] = h_ref[pl.ds(r, n), :cw] + ar
            h_ref[pl.ds(r, n), cw:] = h_ref[pl.ds(r, n), cw:] + ai
            return 0

        lax.fori_loop(0, seg, fix, 0)

    st_ref[0, 0, 0] = end_r
    st_ref[0, 0, 1] = end_i
    y = jnp.dot(h_ref[...].astype(BF16), cw_ref[0], preferred_element_type=F32) + d_ref[...] * up_ref[...]
    up_ref[...] = jax.nn.gelu(y)
    for i in range(seg):
        y_ref[pl.ds(i, n, stride=stride), :] = up_ref[i * n:(i + 1) * n, :]


def _s5(u, bb_bd, cw_bd, d_vec, lb, pw, pwc, s0, n_seq, seg, stride, chain):
    t, w = u.shape
    ncc, uw, cw2 = bb_bd.shape
    cw = cw2 // 2
    rows = SUBLANES * seg
    return pl.pallas_call(
        functools.partial(_s5_kernel, seg=seg, stride=stride, chain=chain),
        grid=(n_seq, ncc),
        in_specs=[
            pl.BlockSpec((rows, uw), lambda b, c: (b, c)),
            pl.BlockSpec((1, uw, cw2), lambda b, c: (c, 0, 0)),
            pl.BlockSpec((1, cw2, uw), lambda b, c: (c, 0, 0)),
            pl.BlockSpec((1, uw), lambda b, c: (0, c)),
            pl.BlockSpec((2, 1, cw), lambda b, c: (0, 0, c)),
            pl.BlockSpec((2, pw.shape[1], cw), lambda b, c: (0, 0, c)),
            pl.BlockSpec((2, pwc.shape[1], cw), lambda b, c: (0, 0, c)),
            pl.BlockSpec((1, 2, SUBLANES, cw), lambda b, c: (b, 0, 0, c)),
        ],
        out_specs=[
            pl.BlockSpec((rows, uw), lambda b, c: (b, c)),
            pl.BlockSpec((1, 1, 2, SUBLANES, cw), lambda b, c: (b, c, 0, 0, 0)),
        ],
        out_shape=[jax.ShapeDtypeStruct((t, w), F32),
                   jax.ShapeDtypeStruct((n_seq, ncc, 2, SUBLANES, cw), F32)],
        scratch_shapes=[pltpu.VMEM((rows, uw), F32), pltpu.VMEM((rows, cw2), F32)],
        compiler_params=_cparams(("parallel", "parallel")),
        name="s5_chain" if chain else "s5_step",
    )(u, bb_bd, cw_bd, d_vec, lb, pw, pwc, s0)


def _block_diag_weights(bb, c_re, c_im, cw):
    _, c, gp = bb.shape
    g = c_re.shape[0]
    p = gp // g
    gpc = cw // p
    ncc = g // gpc
    eye = jnp.eye(gpc, dtype=F32)
    b4 = jnp.transpose(bb.reshape(2, c, ncc, gpc, p), (2, 0, 3, 1, 4))
    bbd = b4[:, :, :, :, None, :] * eye[None, None, :, None, :, None]
    bbd = jnp.transpose(bbd.reshape(ncc, 2, gpc * c, cw), (0, 2, 1, 3)).reshape(ncc, gpc * c, 2 * cw)
    cc = jnp.stack([c_re, -c_im]).reshape(2, ncc, gpc, c, p)
    c4 = jnp.transpose(cc, (1, 0, 2, 4, 3))
    cbd = c4[:, :, :, :, None, :] * eye[None, None, :, None, :, None]
    cbd = cbd.reshape(ncc, 2 * cw, gpc * c)
    return bbd.astype(BF16), cbd.astype(BF16)


def _merge_kernel(x_ref, oa_ref, yg_ref, gm_ref, wg_ref, wglu_ref, bglu_ref, wa_ref, ws_ref, wo_ref, gf_ref,
                  wr_ref, br_ref, h_ref, hn_ref, ti_ref, tp_ref, lr_ref, cnt_ref):
    x = x_ref[...]
    tm, d = x.shape
    dot = functools.partial(jnp.dot, preferred_element_type=F32)
    xn = _rms(x, gm_ref[...]).astype(BF16)
    gates = jax.nn.sigmoid(dot(xn, wg_ref[...]))
    yg = yg_ref[...]
    ys = yg * jax.nn.sigmoid(dot(yg.astype(BF16), wglu_ref[...]) + bglu_ref[...])
    merged = (gates[:, :d] * dot(oa_ref[...].astype(BF16), wa_ref[...])
              + gates[:, d:] * dot(ys.astype(BF16), ws_ref[...]))
    h = x + dot(merged.astype(BF16), wo_ref[...])
    h_ref[...] = h
    hn = _rms(h, gf_ref[...]).astype(BF16)
    hn_ref[...] = hn
    logit = lax.dot_general(wr_ref[...], hn, (((1,), (1,)), ((), ())), preferred_element_type=F32) + br_ref[...]
    n_e = logit.shape[0]
    eid = lax.broadcasted_iota(I32, logit.shape, 0)
    vals, idxs = [], []
    for _ in range(TOP_K):
        mx = jnp.max(logit, axis=0, keepdims=True)
        ix = jnp.min(jnp.where(logit == mx, eid, n_e), axis=0, keepdims=True)
        vals.append(mx)
        idxs.append(ix)
        logit = jnp.where(eid == ix, -jnp.inf, logit)
    ex = [jnp.exp(v - vals[0]) for v in vals]
    tot = ex[0]
    for e in ex[1:]:
        tot = tot + e
    ti_ref[...] = jnp.concatenate(idxs, axis=0)
    tp_ref[...] = jnp.concatenate(ex, axis=0) / tot
    onehot = [jnp.where(eid == ix, 1.0, 0.0) for ix in idxs]
    before = (lax.broadcasted_iota(I32, (tm, tm), 0) < lax.broadcasted_iota(I32, (tm, tm), 1))
    prefix = dot(jnp.concatenate(onehot, axis=0).astype(BF16), jnp.where(before, 1.0, 0.0).astype(BF16))
    base = jnp.zeros((n_e, 1), F32)
    ranks = []
    for k in range(TOP_K):
        ranks.append(jnp.sum(onehot[k] * (prefix[k * n_e:(k + 1) * n_e] + base), axis=0, keepdims=True))
        base = base + jnp.sum(onehot[k], axis=1, keepdims=True)
    lr_ref[...] = jnp.concatenate(ranks, axis=0).astype(I32)
    cnt_ref[0] = base.astype(I32)


def _merge(x, oa, yg, w, tm):
    t, d = x.shape
    aw = oa.shape[1]
    sw = yg.shape[1]
    n_e = w["wr"].shape[0]
    row = lambda i: (i, 0)
    col = lambda i: (0, i)
    fix = lambda i: (0, 0)
    full = lambda a: pl.BlockSpec(a.shape, fix)
    names = ["gm", "wg", "wglu", "bglu", "wa", "ws", "wo", "gf", "wr", "br"]
    kt = pl.BlockSpec((TOP_K, tm), col)
    return pl.pallas_call(
        _merge_kernel,
        grid=(t // tm,),
        in_specs=[pl.BlockSpec((tm, d), row), pl.BlockSpec((tm, aw), row), pl.BlockSpec((tm, sw), row)]
        + [full(w[n]) for n in names],
        out_specs=[pl.BlockSpec((tm, d), row), pl.BlockSpec((tm, d), row), kt, kt, kt,
                   pl.BlockSpec((1, n_e, 1), lambda i: (i, 0, 0))],
        out_shape=[jax.ShapeDtypeStruct((t, d), F32), jax.ShapeDtypeStruct((t, d), BF16),
                   jax.ShapeDtypeStruct((TOP_K, t), I32), jax.ShapeDtypeStruct((TOP_K, t), F32),
                   jax.ShapeDtypeStruct((TOP_K, t), I32), jax.ShapeDtypeStruct((t // tm, n_e, 1), I32)],
        compiler_params=_cparams(("parallel",)),
        name="merge",
    )(x, oa, yg, *[w[n] for n in names])


def _chunks_per_tile(tm, n_experts):
    rows = TOP_K * tm + (SUBLANES - 1) * n_experts
    return -(-rows // (SUBLANES * SUBLANES)) * SUBLANES


def _route_plan(cnt, moe_tile):
    n_e = cnt.shape[1]
    cnt8 = (cnt + SUBLANES - 1) // SUBLANES * SUBLANES
    total = jnp.sum(cnt8, axis=0)
    padded = (total + moe_tile - 1) // moe_tile * moe_tile
    pad_end = jnp.cumsum(padded)
    tile_base = (pad_end - padded)[None, :] + jnp.cumsum(cnt8, axis=0) - cnt8
    off_end = jnp.cumsum(cnt8, axis=1)
    off8 = off_end - cnt8
    return dict(padded=padded.astype(I32), pad_end=pad_end.astype(I32), tile_base=tile_base, off8=off8,
                off_end=off_end, nch=(off_end[:, -1] // SUBLANES).astype(I32), n_e=n_e)


def _chunk_rows(plan, lo, hi, g_max):
    off_end = plan["off_end"][lo:hi]
    delta = (plan["tile_base"] - plan["off8"])[lo:hi]
    row0 = jnp.arange(g_max, dtype=I32) * SUBLANES
    grp = jnp.sum((off_end[:, None, :] <= row0[None, :, None]).astype(I32), axis=-1)
    grp = jnp.minimum(grp, plan["n_e"] - 1)
    sel = grp[:, :, None] == jnp.arange(plan["n_e"], dtype=I32)[None, None, :]
    dst = row0[None, :] + jnp.sum(jnp.where(sel, delta[:, None, :], 0), axis=-1)
    return dst.astype(I32)[:, None, :]


def _dispatch_kernel(nch_ref, pe_ref, pd_ref, dst_ref, hnp_ref, hns_ref, ix_ref, lr_ref, off_ref, pos_ref, xs_hbm,
                     xbuf, zbuf, sems, zsem, *, n, n_p, moe_tile):
    i = pl.program_id(0)
    tm = hnp_ref.shape[0]
    rows = xbuf.shape[1]
    n_e = off_ref.shape[1]
    n_moe_tiles = xs_hbm.shape[0] // moe_tile
    slot = lax.rem(i, 2)

    def chunk(s, g, row):
        r = pl.multiple_of(row, SUBLANES)
        src = xbuf.at[s, pl.ds(pl.multiple_of(g * SUBLANES, SUBLANES), SUBLANES), :]
        return pltpu.make_async_copy(src, xs_hbm.at[pl.ds(r, SUBLANES), :], sems.at[s])

    def drain(s, count):
        def body(g, c):
            chunk(s, 0, 0).wait()
            return c
        lax.fori_loop(0, count, body, 0)

    @pl.when(i == 0)
    def _():
        zbuf[...] = jnp.zeros(zbuf.shape, zbuf.dtype)
        n_used = pe_ref[n_e - 1] // moe_tile

        def zero_tile(row):
            r = pl.multiple_of(row, SUBLANES)
            return pltpu.make_async_copy(zbuf, xs_hbm.at[pl.ds(r, moe_tile), :], zsem.at[0])

        for e in range(n_e):
            @pl.when(pd_ref[e] > 0)
            def _():
                zero_tile(jnp.maximum(pe_ref[e] - moe_tile, 0)).start()

        def start_rest(t, c):
            zero_tile(t * moe_tile).start()
            return c
        lax.fori_loop(n_used, n_moe_tiles, start_rest, 0)
        for e in range(n_e):
            @pl.when(pd_ref[e] > 0)
            def _():
                zero_tile(0).wait()

        def wait_rest(t, c):
            zero_tile(0).wait()
            return c
        lax.fori_loop(n_used, n_moe_tiles, wait_rest, 0)

    @pl.when(i >= 2)
    def _():
        drain(slot, nch_ref[jnp.maximum(i - 2, 0)])

    hn = jnp.where(i < n_p, hnp_ref[...], hns_ref[...])
    eid = lax.broadcasted_iota(I32, (n_e, tm), 0)
    off = off_ref[0].astype(F32)
    r_iota = lax.broadcasted_iota(I32, (rows, tm), 0)
    sel = jnp.zeros((rows, tm), F32)
    pos_rows = []
    for k in range(TOP_K):
        onehot = jnp.where(eid == ix_ref[k:k + 1, :], 1.0, 0.0)
        pos = jnp.sum(onehot * off, axis=0, keepdims=True).astype(I32) + lr_ref[k:k + 1, :]
        pos_rows.append(pos)
        sel = sel + jnp.where(r_iota == pos, 1.0, 0.0)
    pos_ref[...] = jnp.concatenate(pos_rows, axis=0)
    xbuf[slot] = jnp.dot(sel.astype(BF16), hn, preferred_element_type=F32)

    def send(g, c):
        chunk(slot, g, dst_ref[0, 0, g]).start()
        return c
    lax.fori_loop(0, nch_ref[i], send, 0)

    @pl.when(i == n - 1)
    def _():
        drain(slot, nch_ref[i])
        if n >= 2:
            drain(1 - slot, nch_ref[jnp.maximum(i - 1, 0)])


def _dispatch(plan, tm, hn_p, hn_s, top_i, lrank, s_pad, moe_tile):
    d = hn_p.shape[1]
    n_p, n_s = hn_p.shape[0] // tm, hn_s.shape[0] // tm
    n = n_p + n_s
    n_e = plan["n_e"]
    g_max = _chunks_per_tile(tm, n_e)
    rows = g_max * SUBLANES
    dst = _chunk_rows(plan, 0, n, g_max)
    off = plan["off8"].astype(I32)[:, :, None]
    kt = pl.BlockSpec((TOP_K, tm), lambda i, *_: (0, i))
    grid_spec = pltpu.PrefetchScalarGridSpec(
        num_scalar_prefetch=3,
        grid=(n,),
        in_specs=[
            pl.BlockSpec((1, 1, g_max), lambda i, *_: (i, 0, 0), memory_space=pltpu.SMEM),
            pl.BlockSpec((tm, d), lambda i, *_: (jnp.minimum(i, n_p - 1), 0)),
            pl.BlockSpec((tm, d), lambda i, *_: (jnp.maximum(i - n_p, 0), 0)),
            kt, kt,
            pl.BlockSpec((1, n_e, 1), lambda i, *_: (i, 0, 0)),
        ],
        out_specs=[kt, pl.BlockSpec(memory_space=pl.ANY)],
        scratch_shapes=[pltpu.VMEM((2, rows, d), F32), pltpu.VMEM((moe_tile, d), F32),
                        pltpu.SemaphoreType.DMA((2,)), pltpu.SemaphoreType.DMA((1,))])
    return pl.pallas_call(
        functools.partial(_dispatch_kernel, n=n, n_p=n_p, moe_tile=moe_tile),
        grid_spec=grid_spec,
        out_shape=[jax.ShapeDtypeStruct((TOP_K, n * tm), I32), jax.ShapeDtypeStruct((s_pad, d), F32)],
        compiler_params=_cparams(("arbitrary",)),
        name="dispatch",
    )(plan["nch"], plan["pad_end"], plan["padded"], dst, hn_p, hn_s, top_i, lrank, off)


def _moe_kernel(te_ref, nu_ref, x_ref, wg_ref, bg_ref, wu_ref, bu_ref, wd_ref, bd_ref, o_ref, wgb, wub, wdb):
    i = pl.program_id(0)
    used = i < nu_ref[0]
    e = te_ref[i]
    prev = te_ref[jnp.maximum(i - 1, 0)]

    @pl.when(used & ((i == 0) | (e != prev)))
    def _():
        wgb[...] = wg_ref[0].astype(BF16)
        wub[...] = wu_ref[0].astype(BF16)
        wdb[...] = wd_ref[0].astype(BF16)

    @pl.when(used)
    def _():
        x = x_ref[...].astype(BF16)
        gt = jnp.dot(x, wgb[...], preferred_element_type=F32) + bg_ref[0]
        up = jnp.dot(x, wub[...], preferred_element_type=F32) + bu_ref[0]
        gt = jnp.minimum(gt, SWIGLU_LIMIT)
        up = jnp.clip(up, -SWIGLU_LIMIT, SWIGLU_LIMIT)
        hh = (up + 1.0) * (gt * jax.nn.sigmoid(SWIGLU_ALPHA * gt))
        o_ref[...] = jnp.dot(hh.astype(BF16), wdb[...], preferred_element_type=F32) + bd_ref[0]

    @pl.when(jnp.logical_not(used))
    def _():
        o_ref[...] = jnp.zeros(o_ref.shape, F32)


def _moe(tile_expert, n_used, xs, wg, bg, wu, bu, wd, bd, tm):
    n_e, d, f = wg.shape
    s_pad = xs.shape[0]
    wmap = lambda i, te, nu: (te[i], 0, 0)
    row = lambda i, te, nu: (i, 0)
    grid_spec = pltpu.PrefetchScalarGridSpec(
        num_scalar_prefetch=2,
        grid=(s_pad // tm,),
        in_specs=[
            pl.BlockSpec((tm, d), row),
            pl.BlockSpec((1, d, f), wmap), pl.BlockSpec((1, 1, f), wmap),
            pl.BlockSpec((1, d, f), wmap), pl.BlockSpec((1, 1, f), wmap),
            pl.BlockSpec((1, f, d), wmap), pl.BlockSpec((1, 1, d), wmap),
        ],
        out_specs=pl.BlockSpec((tm, d), row),
        scratch_shapes=[pltpu.VMEM((d, f), BF16), pltpu.VMEM((d, f), BF16), pltpu.VMEM((f, d), BF16)],
    )
    return pl.pallas_call(
        _moe_kernel,
        grid_spec=grid_spec,
        out_shape=jax.ShapeDtypeStruct((s_pad, d), F32),
        compiler_params=pltpu.CompilerParams(dimension_semantics=("arbitrary",),
                                             vmem_limit_bytes=56 * 1024 * 1024),
        name="moe",
    )(tile_expert, n_used, xs, wg, bg.reshape(n_e, 1, f), wu, bu.reshape(n_e, 1, f), wd, bd.reshape(n_e, 1, d))


def _combine_kernel(nch_ref, dst_ref, nxt_ref, h_ref, p_ref, pos_ref, g_ref, o_hbm, y_ref, obuf, sems, *, n):
    i = pl.program_id(0)
    tm = h_ref.shape[0]
    rows = obuf.shape[1]
    slot = lax.rem(i, 2)

    def chunk(s, g, row):
        r = pl.multiple_of(row, SUBLANES)
        dst = obuf.at[s, pl.ds(pl.multiple_of(g * SUBLANES, SUBLANES), SUBLANES), :]
        return pltpu.make_async_copy(o_hbm.at[pl.ds(r, SUBLANES), :], dst, sems.at[s])

    def fetch(s, rows_ref, count):
        def body(g, c):
            chunk(s, g, rows_ref[0, 0, g]).start()
            return c
        lax.fori_loop(0, count, body, 0)

    @pl.when(i == 0)
    def _():
        obuf[...] = jnp.zeros(obuf.shape, obuf.dtype)
        fetch(0, dst_ref, nch_ref[0])

    @pl.when(i + 1 < n)
    def _():
        fetch(1 - slot, nxt_ref, nch_ref[jnp.minimum(i + 1, n - 1)])

    def drain(g, c):
        chunk(slot, 0, 0).wait()
        return c
    lax.fori_loop(0, nch_ref[i], drain, 0)

    ob = obuf[slot].astype(BF16)
    lane = lax.broadcasted_iota(I32, (tm, rows), 1)
    w = jnp.zeros((tm, rows), F32)
    for k in range(TOP_K):
        w = w + jnp.where(lane == pos_ref[:, k:k + 1], p_ref[:, k:k + 1], 0.0)
    w_hi = w.astype(BF16)
    w_lo = (w - w_hi.astype(F32)).astype(BF16)
    moe = jnp.dot(w_hi, ob, preferred_element_type=F32) + jnp.dot(w_lo, ob, preferred_element_type=F32)
    y_ref[...] = _rms(h_ref[...] + moe, g_ref[...])


def _combine(plan, lo, tm, h, probs_t, pos_t, g, o_sorted):
    t, d = h.shape
    n = t // tm
    g_max = _chunks_per_tile(tm, plan["n_e"])
    rows = g_max * SUBLANES
    dst = _chunk_rows(plan, lo, lo + n, g_max)
    tok = lambda i, *_: (i, 0)
    grid_spec = pltpu.PrefetchScalarGridSpec(
        num_scalar_prefetch=1,
        grid=(n,),
        in_specs=[
            pl.BlockSpec((1, 1, g_max), lambda i, *_: (i, 0, 0), memory_space=pltpu.SMEM),
            pl.BlockSpec((1, 1, g_max), lambda i, *_: (jnp.minimum(i + 1, n - 1), 0, 0), memory_space=pltpu.SMEM),
            pl.BlockSpec((tm, d), tok), pl.BlockSpec((tm, TOP_K), tok), pl.BlockSpec((tm, TOP_K), tok),
            pl.BlockSpec((1, d), lambda i, *_: (0, 0)),
            pl.BlockSpec(memory_space=pl.ANY),
        ],
        out_specs=pl.BlockSpec((tm, d), tok),
        scratch_shapes=[pltpu.VMEM((2, rows, d), F32), pltpu.SemaphoreType.DMA((2,))],
    )
    return pl.pallas_call(
        functools.partial(_combine_kernel, n=n),
        grid_spec=grid_spec,
        out_shape=jax.ShapeDtypeStruct((t, d), F32),
        compiler_params=_cparams(("arbitrary",)),
        name="combine",
    )(plan["nch"][lo:lo + n], dst, dst, h, probs_t, pos_t, g, o_sorted)


def kernel(x_prompt, x_sample, cache_k, cache_v, state_ssm_re, state_ssm_im, page_table, norm_mix_g, w_in, lambda_q1, lambda_k1, lambda_q2, lambda_k2, subln_g, rel_bias, ssm_a_re, ssm_a_im, ssm_log_dt, ssm_b_re, ssm_b_im, ssm_c_re, ssm_c_im, ssm_d, w_glu, b_glu, w_attn_up, w_ssm_up, w_out, norm_ffn_g, w_router, b_router, w_e_gate, b_e_gate, w_e_up, b_e_up, w_e_down, b_e_down, norm_final_g):
    bp, seq, d = x_prompt.shape
    db, dl, _ = x_sample.shape
    depth, n_pool, page, n_heads, _, hd = cache_k.shape
    vd = cache_v.shape[-1]
    n_groups, n_state = ssm_a_re.shape[1:]
    gch = ssm_b_re.shape[-1]
    n_experts = w_router.shape[-1]
    qk_w = n_heads * 2 * hd
    attn_w = n_heads * vd
    ssm_w = n_groups * gch
    mix_w = 2 * qk_w + attn_w + ssm_w
    assert qk_w == attn_w == ssm_w and vd == 2 * hd
    assert seq % SUBLANES == 0 and db % SUBLANES == 0 and dl == SUBLANES
    tp, ts = bp * seq, db * dl
    scale = hd ** -0.5
    seg = seq // SUBLANES
    cw = min(S5_CHUNK, n_groups * n_state)

    ckt = jnp.transpose(cache_k, (0, 1, 3, 4, 5, 2)).reshape(depth, n_pool, qk_w, page)
    cv = cache_v.reshape(depth, n_pool, page * n_heads, vd)

    blk = min(ATTN_BLOCK, seq)
    qi = np.arange(blk)[:, None]
    ki = np.arange(blk)[None, :]
    bkt_p = np.stack([_bucket_tile(qi - ki), _bucket_tile(blk + qi - ki)])
    sq = np.tile(np.arange(dl), 2)[:, None]
    sk = np.arange(page)[None, :]
    new_dist = np.where(sk < dl, sq - sk, -1)
    bkt_s = np.stack([_bucket_tile(page + sq - sk), _bucket_tile(new_dist)])

    hp = x_prompt.reshape(tp, d)
    hs = x_sample.reshape(ts, d)
    outs = {n: [] for n in ("kp", "vp", "ks", "vs", "rp", "ip", "rs", "is")}
    tm_p = math.gcd(TOKEN_TILE, tp)
    tm_s = math.gcd(TOKEN_TILE, ts)
    for l in range(depth):
        lambda_init = 0.8 - 0.6 * math.exp(-0.3 * l)
        w_l = w_in[l].astype(BF16)
        g_mix = norm_mix_g[l].reshape(1, d)
        lams = [a[l].reshape(1, hd) for a in (lambda_q1, lambda_k1, lambda_q2, lambda_k2)]
        sub_g = subln_g[l].reshape(1, vd)

        w_qvu = jnp.concatenate([w_l[:, :qk_w], w_l[:, 2 * qk_w:mix_w]], axis=1)
        w_kt = jnp.transpose(w_l[:, qk_w:2 * qk_w])
        qp, ktp, vp, up, ktpb, vpb = _in_proj(hp, g_mix, w_qvu, w_kt, scale, tm_p, page)
        qs, kts, vs, us, _, _ = _in_proj(hs, g_mix, w_qvu, w_kt, scale, tm_s, None)
        ks = jnp.transpose(kts)

        bias_p = _rel_bias_tiles(rel_bias, bkt_p)
        bias_s = _rel_bias_tiles(rel_bias, bkt_s)
        bias_s = jnp.transpose(bias_s, (1, 0, 2, 3)).reshape(2, n_heads * 2 * dl, page)

        oa_p = _attn_prompt(lams, sub_g, qp, ktpb, vpb, bias_p, bp, seq, n_heads, lambda_init)
        oa_s = _attn_sample(page_table, lams, sub_g, qs, ks, vs, bias_s, ckt, cv, l, n_heads, lambda_init)

        lb, bb, pw, pwc = _s5_prep(ssm_a_re[l], ssm_a_im[l], ssm_log_dt[l], ssm_b_re[l], ssm_b_im[l], seg)
        bb_bd, cw_bd = _block_diag_weights(bb, ssm_c_re[l], ssm_c_im[l], cw)
        d_vec = ssm_d[l].reshape(1, ssm_w)
        zeros = jnp.zeros((bp, 2, SUBLANES, n_groups * n_state), F32)
        yg_p, st_p = _s5(up, bb_bd, cw_bd, d_vec, lb, pw, pwc, zeros, bp, seg, seg, True)
        s0 = jnp.stack([state_ssm_re[l], state_ssm_im[l]], axis=1).reshape(
            db // SUBLANES, SUBLANES, 2, n_groups * n_state)
        s0 = jnp.transpose(s0, (0, 2, 1, 3))
        yg_s, st_s = _s5(us, bb_bd, cw_bd, d_vec, lb, pw, pwc, s0, db // SUBLANES, dl, dl, False)

        wts = dict(
            gm=g_mix, wg=w_l[:, mix_w:], wglu=w_glu[l].astype(BF16), bglu=b_glu[l].reshape(1, ssm_w),
            wa=w_attn_up[l].astype(BF16), ws=w_ssm_up[l].astype(BF16), wo=w_out[l].astype(BF16),
            gf=norm_ffn_g[l].reshape(1, d), wr=jnp.transpose(w_router[l]).astype(BF16),
            br=b_router[l].reshape(n_experts, 1))
        rt = math.gcd(ROUTE_TILE, math.gcd(tp, ts))
        h_p, hn_p, ti_p, pr_p, lr_p, cnt_p = _merge(hp, oa_p, yg_p, wts, rt)
        h_s, hn_s, ti_s, pr_s, lr_s, cnt_s = _merge(hs, oa_s, yg_s, wts, rt)

        n_p, n_s = tp // rt, ts // rt
        plan = _route_plan(jnp.concatenate([cnt_p[:, :, 0], cnt_s[:, :, 0]], axis=0), MOE_TILE)
        worst = TOP_K * (tp + ts) + (SUBLANES - 1) * n_experts * (n_p + n_s) + n_experts * (MOE_TILE - 1)
        s_pad = -(-worst // MOE_TILE) * MOE_TILE
        pos, xs = _dispatch(plan, rt, hn_p, hn_s, jnp.concatenate([ti_p, ti_s], axis=1),
                            jnp.concatenate([lr_p, lr_s], axis=1), s_pad, MOE_TILE)
        tile_start = jnp.arange(s_pad // MOE_TILE, dtype=I32) * MOE_TILE
        tile_expert = jnp.minimum(jnp.sum((plan["pad_end"][None, :] <= tile_start[:, None]).astype(I32), axis=1),
                                  n_experts - 1)
        n_used = (plan["pad_end"][-1:] // MOE_TILE).astype(I32)
        o_sorted = _moe(tile_expert, n_used, xs, w_e_gate[l], b_e_gate[l], w_e_up[l], b_e_up[l],
                        w_e_down[l], b_e_down[l], MOE_TILE)

        last = l == depth - 1
        g_out = norm_final_g.reshape(1, d) if last else None
        assert last, "multi-layer stacks need a combine variant without the final norm"
        pos_t = jnp.transpose(pos)
        hp = _combine(plan, 0, rt, h_p, jnp.transpose(pr_p), pos_t[:tp], g_out, o_sorted)
        hs = _combine(plan, n_p, rt, h_s, jnp.transpose(pr_s), pos_t[tp:], g_out, o_sorted)

        gp = n_groups * n_state
        st_p = jnp.transpose(st_p, (0, 2, 3, 1, 4)).reshape(bp, 2, SUBLANES, gp)[:, :, SUBLANES - 1]
        st_s = jnp.transpose(st_s, (0, 3, 2, 1, 4)).reshape(db, 2, gp)
        outs["kp"].append(jnp.transpose(ktp.reshape(bp, seq // page, n_heads, 2, hd, page), (0, 1, 5, 2, 3, 4)))
        outs["vp"].append(vp.reshape(bp, seq // page, page, n_heads, vd))
        outs["ks"].append(ks.reshape(db, dl, n_heads, 2, hd))
        outs["vs"].append(vs.reshape(db, dl, n_heads, vd))
        outs["rp"].append(st_p[:, 0].reshape(bp, n_groups, n_state))
        outs["ip"].append(st_p[:, 1].reshape(bp, n_groups, n_state))
        outs["rs"].append(st_s[:, 0].reshape(db, n_groups, n_state))
        outs["is"].append(st_s[:, 1].reshape(db, n_groups, n_state))

    st = lambda n: jnp.stack(outs[n])
    return (hp.reshape(bp, seq, d), hs.reshape(db, dl, d), st("kp"), st("vp"), st("ks"), st("vs"),
            st("rp"), st("ip"), st("rs"), st("is"))
```

```python
import functools
import math

import numpy as np
import jax
import jax.numpy as jnp
from jax import lax
from jax.experimental import pallas as pl
from jax.experimental.pallas import tpu as pltpu

F32 = jnp.float32
BF16 = jnp.bfloat16
I32 = jnp.int32

EPS = 1e-6
NEG_INF = -1e30
TOP_K = 4
N_BUCKETS = 32
MAX_EXACT = N_BUCKETS // 2
MAX_DISTANCE = 128
SWIGLU_LIMIT = 7.0
SWIGLU_ALPHA = 1.702

SUBLANES = 8
LANES = 128
VMEM_LIMIT = 48 * 1024 * 1024

ATTN_BLOCK = 512
PAGES_PER_STEP = 16
TOKEN_TILE = 512
MOE_TILE = 512
ROUTE_TILE = 256
S5_CHUNK = 512


def _cparams(sem):
    return pltpu.CompilerParams(dimension_semantics=sem, vmem_limit_bytes=VMEM_LIMIT)


def _rms(x, g):
    return x * lax.rsqrt(jnp.mean(x * x, axis=-1, keepdims=True) + EPS) * g


def _in_proj_kernel(x_ref, g_ref, w_ref, wkt_ref, q_ref, k_ref, v_ref, u_ref, kb_ref, vb_ref, *, scale, page, n_heads):
    xn = _rms(x_ref[...], g_ref[...]).astype(BF16)
    p = jnp.dot(xn, w_ref[...], preferred_element_type=F32)
    w3 = p.shape[1] // 3
    tm = p.shape[0]
    vd = w3 // n_heads
    q_ref[...] = p[:, :w3] * scale
    v = p[:, w3:2 * w3]
    for h in range(n_heads):
        v_ref[pl.ds(h, tm, stride=n_heads), :] = v[:, h * vd:(h + 1) * vd]
    vb_ref[...] = v.astype(BF16)
    u_ref[...] = p[:, 2 * w3:]
    kt = lax.dot_general(wkt_ref[...], xn, (((1,), (1,)), ((), ())), preferred_element_type=F32)
    if page is None:
        k_ref[...] = kt
        kb_ref[...] = kt.astype(BF16)
    else:
        for pg in range(kt.shape[1] // page):
            tile = kt[:, pg * page:(pg + 1) * page]
            k_ref[pg] = tile
            kb_ref[pg] = tile.astype(BF16)


def _in_proj(x, g, w_qvu, w_kt, scale, tm, page, n_heads):
    t, d = x.shape
    w3 = w_qvu.shape[1] // 3
    row = lambda i: (i, 0)
    fix = lambda i: (0, 0)
    f32o = jax.ShapeDtypeStruct((t, w3), F32)
    if page is None:
        kt_shape, kt_spec = (w3, t), pl.BlockSpec((w3, tm), lambda i: (0, i))
    else:
        kt_shape, kt_spec = (t // page, w3, page), pl.BlockSpec((tm // page, w3, page), lambda i: (i, 0, 0))
    tok = pl.BlockSpec((tm, w3), row)
    return pl.pallas_call(
        functools.partial(_in_proj_kernel, scale=scale, page=page, n_heads=n_heads),
        grid=(t // tm,),
        in_specs=[pl.BlockSpec((tm, d), row), pl.BlockSpec((1, d), fix), pl.BlockSpec((d, 3 * w3), fix),
                  pl.BlockSpec((w3, d), fix)],
        out_specs=[tok, kt_spec, pl.BlockSpec((tm * n_heads, w3 // n_heads), row), tok, kt_spec, tok],
        out_shape=[f32o, jax.ShapeDtypeStruct(kt_shape, F32),
                   jax.ShapeDtypeStruct((t * n_heads, w3 // n_heads), F32), f32o,
                   jax.ShapeDtypeStruct(kt_shape, BF16), jax.ShapeDtypeStruct((t, w3), BF16)],
        compiler_params=_cparams(("parallel",)),
        name="in_proj",
    )(x, g, w_qvu, w_kt)


def _bucket_np(n):
    nf = np.maximum(n, 1).astype(np.float64)
    large = MAX_EXACT + np.trunc(
        np.log(nf / MAX_EXACT) / math.log(MAX_DISTANCE / MAX_EXACT) * (N_BUCKETS - MAX_EXACT)).astype(np.int64)
    return np.where(n < MAX_EXACT, n, np.minimum(large, N_BUCKETS - 1)).astype(np.int32)


def _bucket_tile(dist):
    return np.where(dist < 0, -1, _bucket_np(np.maximum(dist, 0))).astype(np.int32)


def _rel_bias_kernel(rb_ref, bkt_ref, o_ref):
    h = pl.program_id(0)
    far = rb_ref[N_BUCKETS - 1, h]
    bkt = bkt_ref[...]
    acc = jnp.zeros(bkt.shape, F32)
    for b in range(N_BUCKETS - 1):
        acc = jnp.where(bkt == b, rb_ref[b, h] - far, acc)
    o_ref[0] = jnp.where(bkt < 0, NEG_INF, acc)


def _rel_bias_tiles(rel_bias, buckets):
    n_heads = rel_bias.shape[1]
    n, r, c = buckets.shape
    return pl.pallas_call(
        _rel_bias_kernel,
        grid=(n_heads,),
        in_specs=[pl.BlockSpec(memory_space=pltpu.SMEM), pl.BlockSpec((n, r, c), lambda h: (0, 0, 0))],
        out_specs=pl.BlockSpec((1, n, r, c), lambda h: (h, 0, 0, 0)),
        out_shape=jax.ShapeDtypeStruct((n_heads, n, r, c), F32),
        compiler_params=_cparams(("arbitrary",)),
        name="rel_bias",
    )(rel_bias, jnp.asarray(buckets))


def _lambda(lq1, lk1, lq2, lk2, lambda_init):
    a = jnp.sum(lq1[...] * lk1[...], axis=-1, keepdims=True)
    b = jnp.sum(lq2[...] * lk2[...], axis=-1, keepdims=True)
    return jnp.exp(a) - jnp.exp(b) + lambda_init


def _attn_p_kernel(lq1, lk1, lq2, lk2, g_ref, q_ref, k_ref, v_ref, b_ref, o_ref, qq_ref, m_ref, l_ref, acc_ref,
                   *, blk, page, n_heads, hd, lambda_init):
    i = pl.program_id(1)
    hw = 2 * hd
    ppb = blk // page
    lam = _lambda(lq1, lk1, lq2, lk2, lambda_init)
    lane = lax.broadcasted_iota(I32, (blk, hw), 1)
    for h in range(n_heads):
        q = q_ref[:, h * hw:(h + 1) * hw].astype(BF16)
        zero = jnp.zeros_like(q)
        qq_ref[h, :blk] = jnp.where(lane < hd, q, zero)
        qq_ref[h, blk:] = jnp.where(lane >= hd, q, zero)
    m_ref[...] = jnp.full(m_ref.shape, NEG_INF, F32)
    l_ref[...] = jnp.zeros(l_ref.shape, F32)
    acc_ref[...] = jnp.zeros(acc_ref.shape, F32)

    def step(j, bias_idx):
        off = pl.multiple_of(j * blk, blk)
        for h in range(n_heads):
            kt = jnp.concatenate([k_ref[j * ppb + p, h * hw:(h + 1) * hw, :] for p in range(ppb)], axis=1)
            vj = v_ref[pl.ds(off, blk), h * hw:(h + 1) * hw]
            s = jnp.dot(qq_ref[h], kt, preferred_element_type=F32)
            if bias_idx is not None:
                s = (s.reshape(2, blk, blk) + b_ref[h, bias_idx][None]).reshape(2 * blk, blk)
            m = m_ref[h]
            m_new = jnp.maximum(m, jnp.max(s, axis=-1, keepdims=True))
            alpha = jnp.exp(m - m_new)
            p = jnp.exp(s - jnp.concatenate([m_new] * (blk // LANES), axis=1))
            l_ref[h] = alpha * l_ref[h] + jnp.sum(p, axis=-1, keepdims=True)
            acc_ref[h] = alpha * acc_ref[h] + jnp.dot(p.astype(BF16), vj, preferred_element_type=F32)
            m_ref[h] = m_new

    def far(j, c):
        step(j, None)
        return c

    lax.fori_loop(0, i - 1, far, 0)

    @pl.when(i >= 1)
    def _():
        step(i - 1, 1)

    step(i, 0)
    for h in range(n_heads):
        o = acc_ref[h] / l_ref[h]
        o = o[:blk] - lam * o[blk:]
        o_ref[:, h * hw:(h + 1) * hw] = (_rms(o, g_ref[...]) * (1.0 - lambda_init)).astype(o_ref.dtype)


def _attn_prompt(lams, subln_g, q, ktb, vb, bias, n_batch, seq, n_heads, lambda_init):
    t, w = q.shape
    page = ktb.shape[2]
    hw = w // n_heads
    blk = math.gcd(ATTN_BLOCK, seq)
    nq = seq // blk
    fix = lambda b, i: (0, 0)
    return pl.pallas_call(
        functools.partial(_attn_p_kernel, blk=blk, page=page, n_heads=n_heads, hd=hw // 2,
                          lambda_init=lambda_init),
        grid=(n_batch, nq),
        in_specs=[pl.BlockSpec((1, hw // 2), fix)] * 4 + [
            pl.BlockSpec((1, hw), fix),
            pl.BlockSpec((blk, w), lambda b, i: (b * nq + i, 0)),
            pl.BlockSpec((seq // page, w, page), lambda b, i: (b, 0, 0)),
            pl.BlockSpec((seq, w), lambda b, i: (b, 0)),
            pl.BlockSpec((n_heads, 2, blk, blk), lambda b, i: (0, 0, 0, 0)),
        ],
        out_specs=pl.BlockSpec((blk, w), lambda b, i: (b * nq + i, 0)),
        out_shape=jax.ShapeDtypeStruct((t, w), BF16),
        scratch_shapes=[pltpu.VMEM((n_heads, 2 * blk, hw), BF16), pltpu.VMEM((n_heads, 2 * blk, LANES), F32),
                        pltpu.VMEM((n_heads, 2 * blk, LANES), F32), pltpu.VMEM((n_heads, 2 * blk, hw), F32)],
        compiler_params=_cparams(("parallel", "parallel")),
        name="attn_p",
    )(*lams, subln_g, q, ktb, vb, bias)


def _attn_s_kernel(pt_ref, lq1, lk1, lq2, lk2, g_ref, q_ref, kn_ref, vn_ref, b_ref, *rest,
                   pps, n_chunks, n_heads, hd, lambda_init):
    k_refs = rest[:pps]
    v_refs = rest[pps:2 * pps]
    o_ref = rest[2 * pps]
    m_ref, l_ref, acc_ref, kbf_ref, vbf_ref = rest[2 * pps + 1:]
    j = pl.program_id(1)
    last = n_chunks - 1
    dl = q_ref.shape[0]
    vd = 2 * hd
    hr = 2 * dl
    page = k_refs[0].shape[3]

    @pl.when(j == 0)
    def _():
        m_ref[...] = jnp.full(m_ref.shape, NEG_INF, F32)
        l_ref[...] = jnp.zeros(l_ref.shape, F32)
        acc_ref[...] = jnp.zeros(acc_ref.shape, F32)

    q8 = q_ref[...]
    qt = jnp.concatenate([q8] * (2 * n_heads), axis=0)
    lane = lax.broadcasted_iota(I32, qt.shape, 1)
    row = lax.broadcasted_iota(I32, qt.shape, 0)
    qbd = jnp.where(lane // hd == row // dl, qt, 0.0).astype(BF16)

    def update(s_list, v_heads):
        m_old = m_ref[...]
        m_new = m_old
        for s in s_list:
            m_new = jnp.maximum(m_new, jnp.max(s, axis=-1, keepdims=True))
        alpha = jnp.exp(m_old - m_new)
        l_new = alpha * l_ref[...]
        acc = alpha * acc_ref[...]
        parts = [acc[h * hr:(h + 1) * hr] for h in range(n_heads)]
        for s, v_of in zip(s_list, v_heads):
            p = jnp.exp(s - m_new)
            l_new = l_new + jnp.sum(p, axis=-1, keepdims=True)
            pb = p.astype(BF16)
            for h in range(n_heads):
                parts[h] = parts[h] + jnp.dot(pb[h * hr:(h + 1) * hr], v_of(h).astype(BF16),
                                              preferred_element_type=F32)
        m_ref[...] = m_new
        l_ref[...] = l_new
        acc_ref[...] = jnp.concatenate(parts, axis=0)

    for p in range(pps):
        kbf_ref[:, p * page:(p + 1) * page] = k_refs[p][0, 0].astype(BF16)
        for h in range(n_heads):
            vbf_ref[h, p * page:(p + 1) * page, :] = v_refs[p][0, 0, pl.ds(h, page, stride=n_heads), :].astype(BF16)
    s = jnp.dot(qbd, kbf_ref[...], preferred_element_type=F32)
    tail = s[:, (pps - 1) * page:] + jnp.where(j == last, b_ref[0], 0.0)
    s = jnp.concatenate([s[:, :(pps - 1) * page], tail], axis=1) if pps > 1 else tail
    update([s], [lambda h: vbf_ref[h]])

    @pl.when(j == last)
    def _():
        pad = jnp.zeros((page - dl, kn_ref.shape[1]), F32)
        kn = jnp.concatenate([kn_ref[...], pad], axis=0).astype(BF16)
        vpad = jnp.zeros((page - dl, vd), F32)
        s_new = lax.dot_general(qbd, kn, (((1,), (1,)), ((), ())), preferred_element_type=F32) + b_ref[1]
        update([s_new], [lambda h: jnp.concatenate([vn_ref[pl.ds(h, dl, stride=n_heads), :], vpad], axis=0)])
        lam = _lambda(lq1, lk1, lq2, lk2, lambda_init)
        o = acc_ref[...] / l_ref[...]
        for h in range(n_heads):
            oh = o[h * hr:h * hr + dl] - lam * o[h * hr + dl:(h + 1) * hr]
            o_ref[:, h * vd:(h + 1) * vd] = _rms(oh, g_ref[...]) * (1.0 - lambda_init)


def _attn_sample(page_table, lams, subln_g, q, k_new, v_new, bias, cache_kt, cache_v, layer, n_heads,
                 lambda_init):
    db, n_pages = page_table.shape
    t, w = q.shape
    dl = t // db
    page = cache_kt.shape[3]
    hd = w // (2 * n_heads)
    pps = math.gcd(PAGES_PER_STEP, n_pages)
    rows = 2 * n_heads * dl
    fix = lambda b, j, pt: (0, 0)
    seq = lambda b, j, pt: (b, 0)

    def page_spec(p, shape):
        return pl.BlockSpec((1, 1) + shape, lambda b, j, pt: (layer, pt[b, j * pps + p], 0, 0))

    grid_spec = pltpu.PrefetchScalarGridSpec(
        num_scalar_prefetch=1,
        grid=(db, n_pages // pps),
        in_specs=[pl.BlockSpec((1, hd), fix)] * 4 + [
            pl.BlockSpec((1, 2 * hd), fix),
            pl.BlockSpec((dl, w), seq), pl.BlockSpec((dl, w), seq), pl.BlockSpec((dl * n_heads, 2 * hd), seq),
            pl.BlockSpec((2, rows, page), lambda b, j, pt: (0, 0, 0)),
        ] + [page_spec(p, (w, page)) for p in range(pps)]
        + [page_spec(p, (page * n_heads, 2 * hd)) for p in range(pps)],
        out_specs=pl.BlockSpec((dl, w), seq),
        scratch_shapes=[pltpu.VMEM((rows, 1), F32), pltpu.VMEM((rows, 1), F32), pltpu.VMEM((rows, 2 * hd), F32),
                        pltpu.VMEM((w, pps * page), BF16), pltpu.VMEM((n_heads, pps * page, 2 * hd), BF16)],
    )
    return pl.pallas_call(
        functools.partial(_attn_s_kernel, pps=pps, n_chunks=n_pages // pps, n_heads=n_heads, hd=hd,
                          lambda_init=lambda_init),
        grid_spec=grid_spec,
        out_shape=jax.ShapeDtypeStruct((t, w), F32),
        compiler_params=_cparams(("parallel", "arbitrary")),
        name="attn_s",
    )(page_table, *lams, subln_g, q, k_new, v_new, bias, *([cache_kt] * pps), *([cache_v] * pps))


def _s5_prep_kernel(are_ref, aim_ref, ldt_ref, bre_ref, bim_ref, lb_ref, bb_ref, pwc_ref, *, seg):
    dt = jnp.exp(ldt_ref[...])
    a_re = jnp.minimum(are_ref[...], -1e-4)
    a_im = aim_ref[...]

    def lam_pow(n):
        mag = jnp.exp(a_re * dt * n)
        ang = a_im * dt * n
        return mag * jnp.cos(ang), mag * jnp.sin(ang)

    lb_re, lb_im = lam_pow(1.0)
    lb_ref[0] = lb_re
    lb_ref[1] = lb_im
    den = a_re * a_re + a_im * a_im
    nr = lb_re - 1.0
    co_re = (nr * a_re + lb_im * a_im) / den
    co_im = (lb_im * a_re - nr * a_im) / den
    bb_ref[0] = co_re * bre_ref[...] - co_im * bim_ref[...]
    bb_ref[1] = co_re * bim_ref[...] + co_im * bre_ref[...]
    n_dbl = pwc_ref.shape[1]
    for s in range(n_dbl):
        c_re, c_im = lam_pow(float(seg * (1 << s)))
        pwc_ref[0, s:s + 1, :] = c_re
        pwc_ref[1, s:s + 1, :] = c_im


def _s5_prep(a_re, a_im, log_dt, b_re, b_im, seg):
    g, p = a_re.shape
    c = b_re.shape[-1]
    gp = g * p
    flat = lambda a: a.reshape(1, gp)
    ldt = jnp.broadcast_to(log_dt[:, None], (g, p)).reshape(1, gp)
    chan = lambda b: jnp.transpose(b, (2, 0, 1)).reshape(c, gp)
    n_dbl = 3
    full = lambda shape: pl.BlockSpec(shape, lambda: (0,) * len(shape))
    return pl.pallas_call(
        functools.partial(_s5_prep_kernel, seg=seg),
        in_specs=[full((1, gp))] * 3 + [full((c, gp))] * 2,
        out_specs=[full((2, 1, gp)), full((2, c, gp)), full((2, n_dbl, gp))],
        out_shape=[jax.ShapeDtypeStruct((2, 1, gp), F32), jax.ShapeDtypeStruct((2, c, gp), F32),
                   jax.ShapeDtypeStruct((2, n_dbl, gp), F32)],
        compiler_params=pltpu.CompilerParams(vmem_limit_bytes=VMEM_LIMIT),
        name="s5_prep",
    )(flat(a_re), flat(a_im), ldt, chan(b_re), chan(b_im))


def _cmul(ar, ai, br, bi):
    return ar * br - ai * bi, ar * bi + ai * br


def _s5_kernel(u_ref, bb_ref, cw_ref, d_ref, lb_ref, pwc_ref, s0_ref, y_ref, st_ref, up_ref, h_ref, hb_ref,
               *, seg, stride, chain):
    cw = lb_ref.shape[2]
    n = SUBLANES
    unroll = math.gcd(seg // 2, 4)
    for i in range(seg):
        up_ref[i * n:(i + 1) * n, :] = u_ref[pl.ds(i, n, stride=stride), :]
    h_ref[...] = jnp.dot(up_ref[...].astype(BF16), bb_ref[0], preferred_element_type=F32)
    lr = jnp.broadcast_to(lb_ref[0], (n, cw))
    li = jnp.broadcast_to(lb_ref[1], (n, cw))

    rows2 = 2 * n

    def step(i, hr, hi):
        r = pl.multiple_of(i * n, n)
        tr, ti = _cmul(lr, li, hr, hi)
        return tr + h_ref[pl.ds(r, n), :cw], ti + h_ref[pl.ds(r, n), cw:]

    def scan(k, carry):
        hr0, hi0 = step(2 * k, *carry)
        hr1, hi1 = step(2 * k + 1, hr0, hi0)
        r = pl.multiple_of(k * rows2, rows2)
        if chain:
            h_ref[pl.ds(r, rows2), :cw] = jnp.concatenate([hr0, hr1], axis=0)
            h_ref[pl.ds(r, rows2), cw:] = jnp.concatenate([hi0, hi1], axis=0)
        else:
            hb_ref[pl.ds(r, rows2), :cw] = jnp.concatenate([hr0, hr1], axis=0).astype(BF16)
            hb_ref[pl.ds(r, rows2), cw:] = jnp.concatenate([hi0, hi1], axis=0).astype(BF16)
        return hr1, hi1

    end_r, end_i = lax.fori_loop(0, seg // 2, scan, (s0_ref[0, 0], s0_ref[0, 1]), unroll=unroll)

    if chain:
        sub = lax.broadcasted_iota(I32, (n, cw), 0)

        def shift(x, d):
            return jnp.where(sub >= d, pltpu.roll(x, d, axis=0), 0.0)

        tr, ti = end_r, end_i
        for s in range(pwc_ref.shape[1]):
            d = 1 << s
            pr = jnp.broadcast_to(pwc_ref[0, s:s + 1, :], (n, cw))
            pi = jnp.broadcast_to(pwc_ref[1, s:s + 1, :], (n, cw))
            ar, ai = _cmul(pr, pi, shift(tr, d), shift(ti, d))
            tr, ti = tr + ar, ti + ai
        end_r, end_i = tr, ti
        cr, ci = shift(tr, 1), shift(ti, 1)

        def fix(k, carry):
            r = pl.multiple_of(k * rows2, rows2)
            ar0, ai0 = _cmul(lr, li, *carry)
            ar1, ai1 = _cmul(lr, li, ar0, ai0)
            hb_ref[pl.ds(r, rows2), :cw] = (h_ref[pl.ds(r, rows2), :cw] + jnp.concatenate([ar0, ar1], axis=0)).astype(BF16)
            hb_ref[pl.ds(r, rows2), cw:] = (h_ref[pl.ds(r, rows2), cw:] + jnp.concatenate([ai0, ai1], axis=0)).astype(BF16)
            return ar1, ai1

        lax.fori_loop(0, seg // 2, fix, (cr, ci), unroll=unroll)

    st_ref[0, 0, 0] = end_r
    st_ref[0, 0, 1] = end_i
    y = jnp.dot(hb_ref[...], cw_ref[0], preferred_element_type=F32) + d_ref[...] * up_ref[...]
    up_ref[...] = jax.nn.gelu(y)
    for i in range(seg):
        y_ref[pl.ds(i, n, stride=stride), :] = up_ref[i * n:(i + 1) * n, :]


def _s5(u, bb_bd, cw_bd, d_vec, lb, pwc, s0, n_seq, seg, stride, chain):
    t, w = u.shape
    ncc, uw, cw2 = bb_bd.shape
    cw = cw2 // 2
    rows = SUBLANES * seg
    return pl.pallas_call(
        functools.partial(_s5_kernel, seg=seg, stride=stride, chain=chain),
        grid=(n_seq, ncc),
        in_specs=[
            pl.BlockSpec((rows, uw), lambda b, c: (b, c)),
            pl.BlockSpec((1, uw, cw2), lambda b, c: (c, 0, 0)),
            pl.BlockSpec((1, cw2, uw), lambda b, c: (c, 0, 0)),
            pl.BlockSpec((1, uw), lambda b, c: (0, c)),
            pl.BlockSpec((2, 1, cw), lambda b, c: (0, 0, c)),
            pl.BlockSpec((2, pwc.shape[1], cw), lambda b, c: (0, 0, c)),
            pl.BlockSpec((1, 2, SUBLANES, cw), lambda b, c: (b, 0, 0, c)),
        ],
        out_specs=[
            pl.BlockSpec((rows, uw), lambda b, c: (b, c)),
            pl.BlockSpec((1, 1, 2, SUBLANES, cw), lambda b, c: (b, c, 0, 0, 0)),
        ],
        out_shape=[jax.ShapeDtypeStruct((t, w), F32),
                   jax.ShapeDtypeStruct((n_seq, ncc, 2, SUBLANES, cw), F32)],
        scratch_shapes=[pltpu.VMEM((rows, uw), F32), pltpu.VMEM((rows, cw2), F32), pltpu.VMEM((rows, cw2), BF16)],
        compiler_params=_cparams(("parallel", "parallel")),
        name="s5_chain" if chain else "s5_step",
    )(u, bb_bd, cw_bd, d_vec, lb, pwc, s0)


def _block_diag_weights(bb, c_re, c_im, cw):
    _, c, gp = bb.shape
    g = c_re.shape[0]
    p = gp // g
    gpc = cw // p
    ncc = g // gpc
    eye = jnp.eye(gpc, dtype=F32)
    b4 = jnp.transpose(bb.reshape(2, c, ncc, gpc, p), (2, 0, 3, 1, 4))
    bbd = b4[:, :, :, :, None, :] * eye[None, None, :, None, :, None]
    bbd = jnp.transpose(bbd.reshape(ncc, 2, gpc * c, cw), (0, 2, 1, 3)).reshape(ncc, gpc * c, 2 * cw)
    cc = jnp.stack([c_re, -c_im]).reshape(2, ncc, gpc, c, p)
    c4 = jnp.transpose(cc, (1, 0, 2, 4, 3))
    cbd = c4[:, :, :, :, None, :] * eye[None, None, :, None, :, None]
    cbd = cbd.reshape(ncc, 2 * cw, gpc * c)
    return bbd.astype(BF16), cbd.astype(BF16)


def _merge_kernel(x_ref, oa_ref, yg_ref, gm_ref, wg_ref, wglu_ref, bglu_ref, wa_ref, ws_ref, wo_ref, gf_ref,
                  wr_ref, br_ref, h_ref, hn_ref, ti_ref, tp_ref, lr_ref, cnt_ref):
    x = x_ref[...]
    tm, d = x.shape
    dot = functools.partial(jnp.dot, preferred_element_type=F32)
    xn = _rms(x, gm_ref[...]).astype(BF16)
    gates = jax.nn.sigmoid(dot(xn, wg_ref[...]))
    yg = yg_ref[...]
    ys = yg * jax.nn.sigmoid(dot(yg.astype(BF16), wglu_ref[...]) + bglu_ref[...])
    merged = (gates[:, :d] * dot(oa_ref[...].astype(BF16), wa_ref[...])
              + gates[:, d:] * dot(ys.astype(BF16), ws_ref[...]))
    h = x + dot(merged.astype(BF16), wo_ref[...])
    h_ref[...] = h
    hn = _rms(h, gf_ref[...]).astype(BF16)
    hn_ref[...] = hn
    logit = lax.dot_general(wr_ref[...], hn, (((1,), (1,)), ((), ())), preferred_element_type=F32) + br_ref[...]
    n_e = logit.shape[0]
    eid = lax.broadcasted_iota(I32, logit.shape, 0)
    vals, idxs = [], []
    for _ in range(TOP_K):
        mx = jnp.max(logit, axis=0, keepdims=True)
        ix = jnp.min(jnp.where(logit == mx, eid, n_e), axis=0, keepdims=True)
        vals.append(mx)
        idxs.append(ix)
        logit = jnp.where(eid == ix, -jnp.inf, logit)
    ex = [jnp.exp(v - vals[0]) for v in vals]
    tot = ex[0]
    for e in ex[1:]:
        tot = tot + e
    ti_ref[...] = jnp.concatenate(idxs, axis=0)
    tp_ref[...] = jnp.concatenate(ex, axis=0) / tot
    onehot = [jnp.where(eid == ix, 1.0, 0.0) for ix in idxs]
    before = (lax.broadcasted_iota(I32, (tm, tm), 0) < lax.broadcasted_iota(I32, (tm, tm), 1))
    prefix = dot(jnp.concatenate(onehot, axis=0).astype(BF16), jnp.where(before, 1.0, 0.0).astype(BF16))
    base = jnp.zeros((n_e, 1), F32)
    ranks = []
    for k in range(TOP_K):
        ranks.append(jnp.sum(onehot[k] * (prefix[k * n_e:(k + 1) * n_e] + base), axis=0, keepdims=True))
        base = base + jnp.sum(onehot[k], axis=1, keepdims=True)
    lr_ref[...] = jnp.concatenate(ranks, axis=0).astype(I32)
    cnt_ref[0] = base.astype(I32)


def _merge(x, oa, yg, w, tm):
    t, d = x.shape
    aw = oa.shape[1]
    sw = yg.shape[1]
    n_e = w["wr"].shape[0]
    row = lambda i: (i, 0)
    col = lambda i: (0, i)
    fix = lambda i: (0, 0)
    full = lambda a: pl.BlockSpec(a.shape, fix)
    names = ["gm", "wg", "wglu", "bglu", "wa", "ws", "wo", "gf", "wr", "br"]
    kt = pl.BlockSpec((TOP_K, tm), col)
    return pl.pallas_call(
        _merge_kernel,
        grid=(t // tm,),
        in_specs=[pl.BlockSpec((tm, d), row), pl.BlockSpec((tm, aw), row), pl.BlockSpec((tm, sw), row)]
        + [full(w[n]) for n in names],
        out_specs=[pl.BlockSpec((tm, d), row), pl.BlockSpec((tm, d), row), kt, kt, kt,
                   pl.BlockSpec((1, n_e, 1), lambda i: (i, 0, 0))],
        out_shape=[jax.ShapeDtypeStruct((t, d), F32), jax.ShapeDtypeStruct((t, d), BF16),
                   jax.ShapeDtypeStruct((TOP_K, t), I32), jax.ShapeDtypeStruct((TOP_K, t), F32),
                   jax.ShapeDtypeStruct((TOP_K, t), I32), jax.ShapeDtypeStruct((t // tm, n_e, 1), I32)],
        compiler_params=_cparams(("parallel",)),
        name="merge",
    )(x, oa, yg, *[w[n] for n in names])


def _chunks_per_tile(tm, n_experts):
    rows = TOP_K * tm + (SUBLANES - 1) * n_experts
    return -(-rows // (SUBLANES * SUBLANES)) * SUBLANES


def _route_plan(cnt, moe_tile):
    n_e = cnt.shape[1]
    cnt8 = (cnt + SUBLANES - 1) // SUBLANES * SUBLANES
    total = jnp.sum(cnt8, axis=0)
    padded = (total + moe_tile - 1) // moe_tile * moe_tile
    pad_end = jnp.cumsum(padded)
    tile_base = (pad_end - padded)[None, :] + jnp.cumsum(cnt8, axis=0) - cnt8
    off_end = jnp.cumsum(cnt8, axis=1)
    off8 = off_end - cnt8
    return dict(padded=padded.astype(I32), pad_end=pad_end.astype(I32), tile_base=tile_base, off8=off8,
                off_end=off_end, nch=(off_end[:, -1] // SUBLANES).astype(I32), n_e=n_e)


def _chunk_rows(plan, lo, hi, g_max):
    off_end = plan["off_end"][lo:hi]
    delta = (plan["tile_base"] - plan["off8"])[lo:hi]
    row0 = jnp.arange(g_max, dtype=I32) * SUBLANES
    grp = jnp.sum((off_end[:, None, :] <= row0[None, :, None]).astype(I32), axis=-1)
    grp = jnp.minimum(grp, plan["n_e"] - 1)
    sel = grp[:, :, None] == jnp.arange(plan["n_e"], dtype=I32)[None, None, :]
    dst = row0[None, :] + jnp.sum(jnp.where(sel, delta[:, None, :], 0), axis=-1)
    return dst.astype(I32)[:, None, :]


def _dispatch_kernel(nch_ref, pe_ref, pd_ref, dst_ref, hnp_ref, hns_ref, ix_ref, lr_ref, off_ref, pos_ref, xs_hbm,
                     xbuf, zbuf, sems, zsem, *, n, n_p, moe_tile):
    i = pl.program_id(0)
    tm = hnp_ref.shape[0]
    rows = xbuf.shape[1]
    n_e = off_ref.shape[1]
    n_moe_tiles = xs_hbm.shape[0] // moe_tile
    slot = lax.rem(i, 2)

    def chunk(s, g, row):
        r = pl.multiple_of(row, SUBLANES)
        src = xbuf.at[s, pl.ds(pl.multiple_of(g * SUBLANES, SUBLANES), SUBLANES), :]
        return pltpu.make_async_copy(src, xs_hbm.at[pl.ds(r, SUBLANES), :], sems.at[s])

    def drain(s, count):
        for bit in range((rows // SUBLANES).bit_length()):
            span = SUBLANES << bit
            if span > rows:
                continue

            @pl.when((count >> bit) & 1 == 1)
            def _():
                pltpu.make_async_copy(xbuf.at[s, pl.ds(0, span), :], xs_hbm.at[pl.ds(0, span), :], sems.at[s]).wait()

    @pl.when(i == 0)
    def _():
        zbuf[...] = jnp.zeros(zbuf.shape, zbuf.dtype)
        n_used = pe_ref[n_e - 1] // moe_tile

        def zero_tile(row):
            r = pl.multiple_of(row, SUBLANES)
            return pltpu.make_async_copy(zbuf, xs_hbm.at[pl.ds(r, moe_tile), :], zsem.at[0])

        for e in range(n_e):
            @pl.when(pd_ref[e] > 0)
            def _():
                zero_tile(jnp.maximum(pe_ref[e] - moe_tile, 0)).start()

        def start_rest(t, c):
            zero_tile(t * moe_tile).start()
            return c
        lax.fori_loop(n_used, n_moe_tiles, start_rest, 0)
        for e in range(n_e):
            @pl.when(pd_ref[e] > 0)
            def _():
                zero_tile(0).wait()

        def wait_rest(t, c):
            zero_tile(0).wait()
            return c
        lax.fori_loop(n_used, n_moe_tiles, wait_rest, 0)

    @pl.when(i >= 2)
    def _():
        drain(slot, nch_ref[jnp.maximum(i - 2, 0)])

    hn = jnp.where(i < n_p, hnp_ref[...], hns_ref[...])
    eid = lax.broadcasted_iota(I32, (n_e, tm), 0)
    off = off_ref[0].astype(F32)
    r_iota = lax.broadcasted_iota(I32, (rows, tm), 0)
    sel = jnp.zeros((rows, tm), F32)
    pos_rows = []
    for k in range(TOP_K):
        onehot = jnp.where(eid == ix_ref[k:k + 1, :], 1.0, 0.0)
        pos = jnp.sum(onehot * off, axis=0, keepdims=True).astype(I32) + lr_ref[k:k + 1, :]
        pos_rows.append(pos)
        sel = sel + jnp.where(r_iota == pos, 1.0, 0.0)
    pos_ref[...] = jnp.concatenate(pos_rows, axis=0)
    xbuf[slot] = jnp.dot(sel.astype(BF16), hn, preferred_element_type=F32)

    def send(g, c):
        chunk(slot, g, dst_ref[0, 0, g]).start()
        return c
    lax.fori_loop(0, nch_ref[i], send, 0)

    @pl.when(i == n - 1)
    def _():
        drain(slot, nch_ref[i])
        if n >= 2:
            drain(1 - slot, nch_ref[jnp.maximum(i - 1, 0)])


def _dispatch(plan, tm, hn_p, hn_s, top_i, lrank, s_pad, moe_tile):
    d = hn_p.shape[1]
    n_p, n_s = hn_p.shape[0] // tm, hn_s.shape[0] // tm
    n = n_p + n_s
    n_e = plan["n_e"]
    g_max = _chunks_per_tile(tm, n_e)
    rows = g_max * SUBLANES
    dst = _chunk_rows(plan, 0, n, g_max)
    off = plan["off8"].astype(I32)[:, :, None]
    kt = pl.BlockSpec((TOP_K, tm), lambda i, *_: (0, i))
    grid_spec = pltpu.PrefetchScalarGridSpec(
        num_scalar_prefetch=3,
        grid=(n,),
        in_specs=[
            pl.BlockSpec((1, 1, g_max), lambda i, *_: (i, 0, 0), memory_space=pltpu.SMEM),
            pl.BlockSpec((tm, d), lambda i, *_: (jnp.minimum(i, n_p - 1), 0)),
            pl.BlockSpec((tm, d), lambda i, *_: (jnp.maximum(i - n_p, 0), 0)),
            kt, kt,
            pl.BlockSpec((1, n_e, 1), lambda i, *_: (i, 0, 0)),
        ],
        out_specs=[kt, pl.BlockSpec(memory_space=pl.ANY)],
        scratch_shapes=[pltpu.VMEM((2, rows, d), F32), pltpu.VMEM((moe_tile, d), F32),
                        pltpu.SemaphoreType.DMA((2,)), pltpu.SemaphoreType.DMA((1,))])
    return pl.pallas_call(
        functools.partial(_dispatch_kernel, n=n, n_p=n_p, moe_tile=moe_tile),
        grid_spec=grid_spec,
        out_shape=[jax.ShapeDtypeStruct((TOP_K, n * tm), I32), jax.ShapeDtypeStruct((s_pad, d), F32)],
        compiler_params=_cparams(("arbitrary",)),
        name="dispatch",
    )(plan["nch"], plan["pad_end"], plan["padded"], dst, hn_p, hn_s, top_i, lrank, off)


def _moe_kernel(te_ref, nu_ref, x_ref, wg_ref, bg_ref, wu_ref, bu_ref, wd_ref, bd_ref, o_ref, wgb, wub, wdb):
    i = pl.program_id(0)
    used = i < nu_ref[0]
    e = te_ref[i]
    prev = te_ref[jnp.maximum(i - 1, 0)]

    @pl.when(used & ((i == 0) | (e != prev)))
    def _():
        wgb[...] = wg_ref[0].astype(BF16)
        wub[...] = wu_ref[0].astype(BF16)
        wdb[...] = wd_ref[0].astype(BF16)

    @pl.when(used)
    def _():
        x = x_ref[...].astype(BF16)
        gt = jnp.dot(x, wgb[...], preferred_element_type=F32) + bg_ref[0]
        up = jnp.dot(x, wub[...], preferred_element_type=F32) + bu_ref[0]
        gt = jnp.minimum(gt, SWIGLU_LIMIT)
        up = jnp.clip(up, -SWIGLU_LIMIT, SWIGLU_LIMIT)
        hh = (up + 1.0) * (gt * jax.nn.sigmoid(SWIGLU_ALPHA * gt))
        o_ref[...] = jnp.dot(hh.astype(BF16), wdb[...], preferred_element_type=F32) + bd_ref[0]

    @pl.when(jnp.logical_not(used))
    def _():
        o_ref[...] = jnp.zeros(o_ref.shape, F32)


def _moe(tile_expert, n_used, xs, wg, bg, wu, bu, wd, bd, tm):
    n_e, d, f = wg.shape
    s_pad = xs.shape[0]
    wmap = lambda i, te, nu: (te[i], 0, 0)
    row = lambda i, te, nu: (i, 0)
    grid_spec = pltpu.PrefetchScalarGridSpec(
        num_scalar_prefetch=2,
        grid=(s_pad // tm,),
        in_specs=[
            pl.BlockSpec((tm, d), row),
            pl.BlockSpec((1, d, f), wmap), pl.BlockSpec((1, 1, f), wmap),
            pl.BlockSpec((1, d, f), wmap), pl.BlockSpec((1, 1, f), wmap),
            pl.BlockSpec((1, f, d), wmap), pl.BlockSpec((1, 1, d), wmap),
        ],
        out_specs=pl.BlockSpec((tm, d), row),
        scratch_shapes=[pltpu.VMEM((d, f), BF16), pltpu.VMEM((d, f), BF16), pltpu.VMEM((f, d), BF16)],
    )
    return pl.pallas_call(
        _moe_kernel,
        grid_spec=grid_spec,
        out_shape=jax.ShapeDtypeStruct((s_pad, d), F32),
        compiler_params=pltpu.CompilerParams(dimension_semantics=("arbitrary",),
                                             vmem_limit_bytes=56 * 1024 * 1024),
        name="moe",
    )(tile_expert, n_used, xs, wg, bg.reshape(n_e, 1, f), wu, bu.reshape(n_e, 1, f), wd, bd.reshape(n_e, 1, d))


def _combine_kernel(nch_ref, dst_ref, nxt_ref, h_ref, p_ref, pos_ref, g_ref, o_hbm, y_ref, obuf, sems, *, n):
    i = pl.program_id(0)
    tm = h_ref.shape[0]
    rows = obuf.shape[1]
    slot = lax.rem(i, 2)

    def chunk(s, g, row):
        r = pl.multiple_of(row, SUBLANES)
        dst = obuf.at[s, pl.ds(pl.multiple_of(g * SUBLANES, SUBLANES), SUBLANES), :]
        return pltpu.make_async_copy(o_hbm.at[pl.ds(r, SUBLANES), :], dst, sems.at[s])

    def fetch(s, rows_ref, count):
        def body(g, c):
            chunk(s, g, rows_ref[0, 0, g]).start()
            return c
        lax.fori_loop(0, count, body, 0)

    @pl.when(i == 0)
    def _():
        obuf[...] = jnp.zeros(obuf.shape, obuf.dtype)
        fetch(0, dst_ref, nch_ref[0])

    @pl.when(i + 1 < n)
    def _():
        fetch(1 - slot, nxt_ref, nch_ref[jnp.minimum(i + 1, n - 1)])

    count = nch_ref[i]
    for bit in range((rows // SUBLANES).bit_length()):
        span = SUBLANES << bit
        if span > rows:
            continue

        @pl.when((count >> bit) & 1 == 1)
        def _():
            pltpu.make_async_copy(o_hbm.at[pl.ds(0, span), :], obuf.at[slot, pl.ds(0, span), :], sems.at[slot]).wait()

    ob = obuf[slot].astype(BF16)
    lane = lax.broadcasted_iota(I32, (tm, rows), 1)
    w = jnp.zeros((tm, rows), F32)
    for k in range(TOP_K):
        w = w + jnp.where(lane == pos_ref[:, k:k + 1], p_ref[:, k:k + 1], 0.0)
    w_hi = w.astype(BF16)
    w_lo = (w - w_hi.astype(F32)).astype(BF16)
    moe = jnp.dot(w_hi, ob, preferred_element_type=F32) + jnp.dot(w_lo, ob, preferred_element_type=F32)
    y_ref[...] = _rms(h_ref[...] + moe, g_ref[...])


def _combine(plan, lo, tm, h, probs_t, pos_t, g, o_sorted):
    t, d = h.shape
    n = t // tm
    g_max = _chunks_per_tile(tm, plan["n_e"])
    rows = g_max * SUBLANES
    dst = _chunk_rows(plan, lo, lo + n, g_max)
    tok = lambda i, *_: (i, 0)
    grid_spec = pltpu.PrefetchScalarGridSpec(
        num_scalar_prefetch=1,
        grid=(n,),
        in_specs=[
            pl.BlockSpec((1, 1, g_max), lambda i, *_: (i, 0, 0), memory_space=pltpu.SMEM),
            pl.BlockSpec((1, 1, g_max), lambda i, *_: (jnp.minimum(i + 1, n - 1), 0, 0), memory_space=pltpu.SMEM),
            pl.BlockSpec((tm, d), tok), pl.BlockSpec((tm, TOP_K), tok), pl.BlockSpec((tm, TOP_K), tok),
            pl.BlockSpec((1, d), lambda i, *_: (0, 0)),
            pl.BlockSpec(memory_space=pl.ANY),
        ],
        out_specs=pl.BlockSpec((tm, d), tok),
        scratch_shapes=[pltpu.VMEM((2, rows, d), F32), pltpu.SemaphoreType.DMA((2,))],
    )
    return pl.pallas_call(
        functools.partial(_combine_kernel, n=n),
        grid_spec=grid_spec,
        out_shape=jax.ShapeDtypeStruct((t, d), F32),
        compiler_params=_cparams(("arbitrary",)),
        name="combine",
    )(plan["nch"][lo:lo + n], dst, dst, h, probs_t, pos_t, g, o_sorted)


def kernel(x_prompt, x_sample, cache_k, cache_v, state_ssm_re, state_ssm_im, page_table, norm_mix_g, w_in, lambda_q1, lambda_k1, lambda_q2, lambda_k2, subln_g, rel_bias, ssm_a_re, ssm_a_im, ssm_log_dt, ssm_b_re, ssm_b_im, ssm_c_re, ssm_c_im, ssm_d, w_glu, b_glu, w_attn_up, w_ssm_up, w_out, norm_ffn_g, w_router, b_router, w_e_gate, b_e_gate, w_e_up, b_e_up, w_e_down, b_e_down, norm_final_g):
    bp, seq, d = x_prompt.shape
    db, dl, _ = x_sample.shape
    depth, n_pool, page, n_heads, _, hd = cache_k.shape
    vd = cache_v.shape[-1]
    n_groups, n_state = ssm_a_re.shape[1:]
    gch = ssm_b_re.shape[-1]
    n_experts = w_router.shape[-1]
    qk_w = n_heads * 2 * hd
    attn_w = n_heads * vd
    ssm_w = n_groups * gch
    mix_w = 2 * qk_w + attn_w + ssm_w
    assert qk_w == attn_w == ssm_w and vd == 2 * hd
    assert seq % SUBLANES == 0 and db % SUBLANES == 0 and dl == SUBLANES
    tp, ts = bp * seq, db * dl
    scale = hd ** -0.5
    seg = seq // SUBLANES
    cw = min(S5_CHUNK, n_groups * n_state)

    ckt = jnp.transpose(cache_k, (0, 1, 3, 4, 5, 2)).reshape(depth, n_pool, qk_w, page)
    cv = cache_v.reshape(depth, n_pool, page * n_heads, vd)

    blk = math.gcd(ATTN_BLOCK, seq)
    qi = np.arange(blk)[:, None]
    ki = np.arange(blk)[None, :]
    bkt_p = np.stack([_bucket_tile(qi - ki), _bucket_tile(blk + qi - ki)])
    sq = np.tile(np.arange(dl), 2)[:, None]
    sk = np.arange(page)[None, :]
    new_dist = np.where(sk < dl, sq - sk, -1)
    bkt_s = np.stack([_bucket_tile(page + sq - sk), _bucket_tile(new_dist)])

    hp = x_prompt.reshape(tp, d)
    hs = x_sample.reshape(ts, d)
    outs = {n: [] for n in ("kp", "vp", "ks", "vs", "rp", "ip", "rs", "is")}
    tm_p = math.gcd(TOKEN_TILE, tp)
    tm_s = math.gcd(TOKEN_TILE, ts)
    for l in range(depth):
        lambda_init = 0.8 - 0.6 * math.exp(-0.3 * l)
        w_l = w_in[l].astype(BF16)
        g_mix = norm_mix_g[l].reshape(1, d)
        lams = [a[l].reshape(1, hd) for a in (lambda_q1, lambda_k1, lambda_q2, lambda_k2)]
        sub_g = subln_g[l].reshape(1, vd)

        w_qvu = jnp.concatenate([w_l[:, :qk_w], w_l[:, 2 * qk_w:mix_w]], axis=1)
        w_kt = jnp.transpose(w_l[:, qk_w:2 * qk_w])
        qp, ktp, vp, up, ktpb, vpb = _in_proj(hp, g_mix, w_qvu, w_kt, scale, tm_p, page, n_heads)
        qs, kts, vs, us, _, _ = _in_proj(hs, g_mix, w_qvu, w_kt, scale, tm_s, None, n_heads)
        ks = jnp.transpose(kts)

        bias_p = _rel_bias_tiles(rel_bias, bkt_p)
        bias_s = _rel_bias_tiles(rel_bias, bkt_s)
        bias_s = jnp.transpose(bias_s, (1, 0, 2, 3)).reshape(2, n_heads * 2 * dl, page)

        oa_p = _attn_prompt(lams, sub_g, qp, ktpb, vpb, bias_p, bp, seq, n_heads, lambda_init)
        oa_s = _attn_sample(page_table, lams, sub_g, qs, ks, vs, bias_s, ckt, cv, l, n_heads, lambda_init)

        lb, bb, pwc = _s5_prep(ssm_a_re[l], ssm_a_im[l], ssm_log_dt[l], ssm_b_re[l], ssm_b_im[l], seg)
        bb_bd, cw_bd = _block_diag_weights(bb, ssm_c_re[l], ssm_c_im[l], cw)
        d_vec = ssm_d[l].reshape(1, ssm_w)
        zeros = jnp.zeros((bp, 2, SUBLANES, n_groups * n_state), F32)
        yg_p, st_p = _s5(up, bb_bd, cw_bd, d_vec, lb, pwc, zeros, bp, seg, seg, True)
        s0 = jnp.stack([state_ssm_re[l], state_ssm_im[l]], axis=1).reshape(
            db // SUBLANES, SUBLANES, 2, n_groups * n_state)
        s0 = jnp.transpose(s0, (0, 2, 1, 3))
        yg_s, st_s = _s5(us, bb_bd, cw_bd, d_vec, lb, pwc, s0, db // SUBLANES, dl, dl, False)

        wts = dict(
            gm=g_mix, wg=w_l[:, mix_w:], wglu=w_glu[l].astype(BF16), bglu=b_glu[l].reshape(1, ssm_w),
            wa=w_attn_up[l].astype(BF16), ws=w_ssm_up[l].astype(BF16), wo=w_out[l].astype(BF16),
            gf=norm_ffn_g[l].reshape(1, d), wr=jnp.transpose(w_router[l]).astype(BF16),
            br=b_router[l].reshape(n_experts, 1))
        rt = math.gcd(ROUTE_TILE, math.gcd(tp, ts))
        h_p, hn_p, ti_p, pr_p, lr_p, cnt_p = _merge(hp, oa_p, yg_p, wts, rt)
        h_s, hn_s, ti_s, pr_s, lr_s, cnt_s = _merge(hs, oa_s, yg_s, wts, rt)

        n_p, n_s = tp // rt, ts // rt
        plan = _route_plan(jnp.concatenate([cnt_p[:, :, 0], cnt_s[:, :, 0]], axis=0), MOE_TILE)
        worst = TOP_K * (tp + ts) + (SUBLANES - 1) * n_experts * (n_p + n_s) + n_experts * (MOE_TILE - 1)
        s_pad = -(-worst // MOE_TILE) * MOE_TILE
        pos, xs = _dispatch(plan, rt, hn_p, hn_s, jnp.concatenate([ti_p, ti_s], axis=1),
                            jnp.concatenate([lr_p, lr_s], axis=1), s_pad, MOE_TILE)
        tile_start = jnp.arange(s_pad // MOE_TILE, dtype=I32) * MOE_TILE
        tile_expert = jnp.minimum(jnp.sum((plan["pad_end"][None, :] <= tile_start[:, None]).astype(I32), axis=1),
                                  n_experts - 1)
        n_used = (plan["pad_end"][-1:] // MOE_TILE).astype(I32)
        o_sorted = _moe(tile_expert, n_used, xs, w_e_gate[l], b_e_gate[l], w_e_up[l], b_e_up[l],
                        w_e_down[l], b_e_down[l], MOE_TILE)

        last = l == depth - 1
        g_out = norm_final_g.reshape(1, d) if last else None
        assert last, "multi-layer stacks need a combine variant without the final norm"
        pos_t = jnp.transpose(pos)
        hp = _combine(plan, 0, rt, h_p, jnp.transpose(pr_p), pos_t[:tp], g_out, o_sorted)
        hs = _combine(plan, n_p, rt, h_s, jnp.transpose(pr_s), pos_t[tp:], g_out, o_sorted)

        gp = n_groups * n_state
        st_p = jnp.transpose(st_p, (0, 2, 3, 1, 4)).reshape(bp, 2, SUBLANES, gp)[:, :, SUBLANES - 1]
        st_s = jnp.transpose(st_s, (0, 3, 2, 1, 4)).reshape(db, 2, gp)
        outs["kp"].append(jnp.transpose(ktp.reshape(bp, seq // page, n_heads, 2, hd, page), (0, 1, 5, 2, 3, 4)))
        outs["vp"].append(vp.reshape(bp, seq // page, page, n_heads, vd))
        outs["ks"].append(ks.reshape(db, dl, n_heads, 2, hd))
        outs["vs"].append(vs.reshape(db, dl, n_heads, vd))
        outs["rp"].append(st_p[:, 0].reshape(bp, n_groups, n_state))
        outs["ip"].append(st_p[:, 1].reshape(bp, n_groups, n_state))
        outs["rs"].append(st_s[:, 0].reshape(db, n_groups, n_state))
        outs["is"].append(st_s[:, 1].reshape(db, n_groups, n_state))

    st = lambda n: jnp.stack(outs[n])
    return (hp.reshape(bp, seq, d), hs.reshape(db, dl, d), st("kp"), st("vp"), st("ks"), st("vs"),
            st("rp"), st("ip"), st("rs"), st("is"))
```

```python
import functools
import math

import numpy as np
import jax
import jax.numpy as jnp
from jax import lax
from jax.experimental import pallas as pl
from jax.experimental.pallas import tpu as pltpu

F32 = jnp.float32
BF16 = jnp.bfloat16
I32 = jnp.int32

EPS = 1e-6
NEG_INF = -1e30
TOP_K = 4
N_BUCKETS = 32
MAX_EXACT = N_BUCKETS // 2
MAX_DISTANCE = 128
SWIGLU_LIMIT = 7.0
SWIGLU_ALPHA = 1.702

SUBLANES = 8
LANES = 128
VMEM_LIMIT = 48 * 1024 * 1024

ATTN_BLOCK = 512
PAGES_PER_STEP = 16
TOKEN_TILE = 512
MOE_TILE = 512
ROUTE_TILE = 256
S5_CHUNK = 512


def _cparams(sem):
    return pltpu.CompilerParams(dimension_semantics=sem, vmem_limit_bytes=VMEM_LIMIT)


def _rms(x, g):
    return x * lax.rsqrt(jnp.mean(x * x, axis=-1, keepdims=True) + EPS) * g


def _in_proj_kernel(x_ref, g_ref, w_ref, wkt_ref, q_ref, k_ref, v_ref, u_ref, kb_ref, vb_ref, *, scale, page, n_heads):
    xn = _rms(x_ref[...], g_ref[...]).astype(BF16)
    p = jnp.dot(xn, w_ref[...], preferred_element_type=F32)
    w3 = p.shape[1] // 3
    tm = p.shape[0]
    vd = w3 // n_heads
    q_ref[...] = p[:, :w3] * scale
    v = p[:, w3:2 * w3]
    for h in range(n_heads):
        v_ref[pl.ds(h, tm, stride=n_heads), :] = v[:, h * vd:(h + 1) * vd]
    vb_ref[...] = v.astype(BF16)
    u_ref[...] = p[:, 2 * w3:]
    kt = lax.dot_general(wkt_ref[...], xn, (((1,), (1,)), ((), ())), preferred_element_type=F32)
    if page is None:
        k_ref[...] = kt
        kb_ref[...] = kt.astype(BF16)
    else:
        for pg in range(kt.shape[1] // page):
            tile = kt[:, pg * page:(pg + 1) * page]
            k_ref[pg] = tile
            kb_ref[pg] = tile.astype(BF16)


def _in_proj(x, g, w_qvu, w_kt, scale, tm, page, n_heads):
    t, d = x.shape
    w3 = w_qvu.shape[1] // 3
    row = lambda i: (i, 0)
    fix = lambda i: (0, 0)
    f32o = jax.ShapeDtypeStruct((t, w3), F32)
    if page is None:
        kt_shape, kt_spec = (w3, t), pl.BlockSpec((w3, tm), lambda i: (0, i))
    else:
        kt_shape, kt_spec = (t // page, w3, page), pl.BlockSpec((tm // page, w3, page), lambda i: (i, 0, 0))
    tok = pl.BlockSpec((tm, w3), row)
    return pl.pallas_call(
        functools.partial(_in_proj_kernel, scale=scale, page=page, n_heads=n_heads),
        grid=(t // tm,),
        in_specs=[pl.BlockSpec((tm, d), row), pl.BlockSpec((1, d), fix), pl.BlockSpec((d, 3 * w3), fix),
                  pl.BlockSpec((w3, d), fix)],
        out_specs=[tok, kt_spec, pl.BlockSpec((tm * n_heads, w3 // n_heads), row), tok, kt_spec, tok],
        out_shape=[f32o, jax.ShapeDtypeStruct(kt_shape, F32),
                   jax.ShapeDtypeStruct((t * n_heads, w3 // n_heads), F32), f32o,
                   jax.ShapeDtypeStruct(kt_shape, BF16), jax.ShapeDtypeStruct((t, w3), BF16)],
        compiler_params=_cparams(("parallel",)),
        name="in_proj",
    )(x, g, w_qvu, w_kt)


def _bucket_np(n):
    nf = np.maximum(n, 1).astype(np.float64)
    large = MAX_EXACT + np.trunc(
        np.log(nf / MAX_EXACT) / math.log(MAX_DISTANCE / MAX_EXACT) * (N_BUCKETS - MAX_EXACT)).astype(np.int64)
    return np.where(n < MAX_EXACT, n, np.minimum(large, N_BUCKETS - 1)).astype(np.int32)


def _bucket_tile(dist):
    return np.where(dist < 0, -1, _bucket_np(np.maximum(dist, 0))).astype(np.int32)


def _rel_bias_kernel(rb_ref, bkt_ref, o_ref):
    h = pl.program_id(0)
    far = rb_ref[N_BUCKETS - 1, h]
    bkt = bkt_ref[...]
    acc = jnp.zeros(bkt.shape, F32)
    for b in range(N_BUCKETS - 1):
        acc = jnp.where(bkt == b, rb_ref[b, h] - far, acc)
    o_ref[0] = jnp.where(bkt < 0, NEG_INF, acc)


def _rel_bias_tiles(rel_bias, buckets):
    n_heads = rel_bias.shape[1]
    n, r, c = buckets.shape
    return pl.pallas_call(
        _rel_bias_kernel,
        grid=(n_heads,),
        in_specs=[pl.BlockSpec(memory_space=pltpu.SMEM), pl.BlockSpec((n, r, c), lambda h: (0, 0, 0))],
        out_specs=pl.BlockSpec((1, n, r, c), lambda h: (h, 0, 0, 0)),
        out_shape=jax.ShapeDtypeStruct((n_heads, n, r, c), F32),
        compiler_params=_cparams(("arbitrary",)),
        name="rel_bias",
    )(rel_bias, jnp.asarray(buckets))


def _lambda(lq1, lk1, lq2, lk2, lambda_init):
    a = jnp.sum(lq1[...] * lk1[...], axis=-1, keepdims=True)
    b = jnp.sum(lq2[...] * lk2[...], axis=-1, keepdims=True)
    return jnp.exp(a) - jnp.exp(b) + lambda_init


def _attn_p_kernel(lq1, lk1, lq2, lk2, g_ref, q_ref, k_ref, v_ref, b_ref, o_ref, qq_ref, m_ref, l_ref, acc_ref,
                   *, blk, page, n_heads, hd, lambda_init):
    i = pl.program_id(1)
    hw = 2 * hd
    ppb = blk // page
    lam = _lambda(lq1, lk1, lq2, lk2, lambda_init)
    lane = lax.broadcasted_iota(I32, (blk, hw), 1)
    for h in range(n_heads):
        q = q_ref[:, h * hw:(h + 1) * hw].astype(BF16)
        zero = jnp.zeros_like(q)
        qq_ref[h, :blk] = jnp.where(lane < hd, q, zero)
        qq_ref[h, blk:] = jnp.where(lane >= hd, q, zero)
    m_ref[...] = jnp.full(m_ref.shape, NEG_INF, F32)
    l_ref[...] = jnp.zeros(l_ref.shape, F32)
    acc_ref[...] = jnp.zeros(acc_ref.shape, F32)

    def step(j, bias_idx):
        off = pl.multiple_of(j * blk, blk)
        for h in range(n_heads):
            kt = jnp.concatenate([k_ref[j * ppb + p, h * hw:(h + 1) * hw, :] for p in range(ppb)], axis=1)
            vj = v_ref[pl.ds(off, blk), h * hw:(h + 1) * hw]
            s = jnp.dot(qq_ref[h], kt, preferred_element_type=F32)
            if bias_idx is not None:
                s = (s.reshape(2, blk, blk) + b_ref[h, bias_idx][None]).reshape(2 * blk, blk)
            m = m_ref[h]
            m_new = jnp.maximum(m, jnp.max(s, axis=-1, keepdims=True))
            alpha = jnp.exp(m - m_new)
            p = jnp.exp(s - jnp.concatenate([m_new] * (blk // LANES), axis=1))
            l_ref[h] = alpha * l_ref[h] + jnp.sum(p, axis=-1, keepdims=True)
            acc_ref[h] = alpha * acc_ref[h] + jnp.dot(p.astype(BF16), vj, preferred_element_type=F32)
            m_ref[h] = m_new

    def far(j, c):
        step(j, None)
        return c

    lax.fori_loop(0, i - 1, far, 0)

    @pl.when(i >= 1)
    def _():
        step(i - 1, 1)

    step(i, 0)
    for h in range(n_heads):
        o = acc_ref[h] / l_ref[h]
        o = o[:blk] - lam * o[blk:]
        o_ref[:, h * hw:(h + 1) * hw] = (_rms(o, g_ref[...]) * (1.0 - lambda_init)).astype(o_ref.dtype)


def _attn_prompt(lams, subln_g, q, ktb, vb, bias, n_batch, seq, n_heads, lambda_init):
    t, w = q.shape
    page = ktb.shape[2]
    hw = w // n_heads
    blk = math.gcd(ATTN_BLOCK, seq)
    nq = seq // blk
    fix = lambda b, i: (0, 0)
    return pl.pallas_call(
        functools.partial(_attn_p_kernel, blk=blk, page=page, n_heads=n_heads, hd=hw // 2,
                          lambda_init=lambda_init),
        grid=(n_batch, nq),
        in_specs=[pl.BlockSpec((1, hw // 2), fix)] * 4 + [
            pl.BlockSpec((1, hw), fix),
            pl.BlockSpec((blk, w), lambda b, i: (b * nq + i, 0)),
            pl.BlockSpec((seq // page, w, page), lambda b, i: (b, 0, 0)),
            pl.BlockSpec((seq, w), lambda b, i: (b, 0)),
            pl.BlockSpec((n_heads, 2, blk, blk), lambda b, i: (0, 0, 0, 0)),
        ],
        out_specs=pl.BlockSpec((blk, w), lambda b, i: (b * nq + i, 0)),
        out_shape=jax.ShapeDtypeStruct((t, w), BF16),
        scratch_shapes=[pltpu.VMEM((n_heads, 2 * blk, hw), BF16), pltpu.VMEM((n_heads, 2 * blk, LANES), F32),
                        pltpu.VMEM((n_heads, 2 * blk, LANES), F32), pltpu.VMEM((n_heads, 2 * blk, hw), F32)],
        compiler_params=_cparams(("parallel", "parallel")),
        name="attn_p",
    )(*lams, subln_g, q, ktb, vb, bias)


def _attn_s_kernel(pt_ref, lq1, lk1, lq2, lk2, g_ref, q_ref, kn_ref, vn_ref, b_ref, *rest,
                   pps, n_chunks, n_heads, hd, lambda_init):
    k_refs = rest[:pps]
    v_refs = rest[pps:2 * pps]
    o_ref = rest[2 * pps]
    m_ref, l_ref, acc_ref, kbf_ref, vbf_ref = rest[2 * pps + 1:]
    j = pl.program_id(1)
    last = n_chunks - 1
    dl = q_ref.shape[0]
    vd = 2 * hd
    hr = 2 * dl
    page = k_refs[0].shape[3]

    @pl.when(j == 0)
    def _():
        m_ref[...] = jnp.full(m_ref.shape, NEG_INF, F32)
        l_ref[...] = jnp.zeros(l_ref.shape, F32)
        acc_ref[...] = jnp.zeros(acc_ref.shape, F32)

    q8 = q_ref[...]
    qt = jnp.concatenate([q8] * (2 * n_heads), axis=0)
    lane = lax.broadcasted_iota(I32, qt.shape, 1)
    row = lax.broadcasted_iota(I32, qt.shape, 0)
    qbd = jnp.where(lane // hd == row // dl, qt, 0.0).astype(BF16)

    def update(s_list, v_heads):
        m_old = m_ref[...]
        m_new = m_old
        for s in s_list:
            m_new = jnp.maximum(m_new, jnp.max(s, axis=-1, keepdims=True))
        alpha = jnp.exp(m_old - m_new)
        l_new = alpha * l_ref[...]
        acc = alpha * acc_ref[...]
        parts = [acc[h * hr:(h + 1) * hr] for h in range(n_heads)]
        for s, v_of in zip(s_list, v_heads):
            p = jnp.exp(s - m_new)
            l_new = l_new + jnp.sum(p, axis=-1, keepdims=True)
            pb = p.astype(BF16)
            for h in range(n_heads):
                parts[h] = parts[h] + jnp.dot(pb[h * hr:(h + 1) * hr], v_of(h).astype(BF16),
                                              preferred_element_type=F32)
        m_ref[...] = m_new
        l_ref[...] = l_new
        acc_ref[...] = jnp.concatenate(parts, axis=0)

    for p in range(pps):
        kbf_ref[:, p * page:(p + 1) * page] = k_refs[p][0, 0].astype(BF16)
        for h in range(n_heads):
            vbf_ref[h, p * page:(p + 1) * page, :] = v_refs[p][0, 0, pl.ds(h, page, stride=n_heads), :].astype(BF16)
    s = jnp.dot(qbd, kbf_ref[...], preferred_element_type=F32)
    tail = s[:, (pps - 1) * page:] + jnp.where(j == last, b_ref[0], 0.0)
    s = jnp.concatenate([s[:, :(pps - 1) * page], tail], axis=1) if pps > 1 else tail
    update([s], [lambda h: vbf_ref[h]])

    @pl.when(j == last)
    def _():
        pad = jnp.zeros((page - dl, kn_ref.shape[1]), F32)
        kn = jnp.concatenate([kn_ref[...], pad], axis=0).astype(BF16)
        vpad = jnp.zeros((page - dl, vd), F32)
        s_new = lax.dot_general(qbd, kn, (((1,), (1,)), ((), ())), preferred_element_type=F32) + b_ref[1]
        update([s_new], [lambda h: jnp.concatenate([vn_ref[pl.ds(h, dl, stride=n_heads), :], vpad], axis=0)])
        lam = _lambda(lq1, lk1, lq2, lk2, lambda_init)
        o = acc_ref[...] / l_ref[...]
        for h in range(n_heads):
            oh = o[h * hr:h * hr + dl] - lam * o[h * hr + dl:(h + 1) * hr]
            o_ref[:, h * vd:(h + 1) * vd] = _rms(oh, g_ref[...]) * (1.0 - lambda_init)


def _attn_sample(page_table, lams, subln_g, q, k_new, v_new, bias, cache_kt, cache_v, layer, n_heads,
                 lambda_init):
    db, n_pages = page_table.shape
    t, w = q.shape
    dl = t // db
    page = cache_kt.shape[3]
    hd = w // (2 * n_heads)
    pps = math.gcd(PAGES_PER_STEP, n_pages)
    rows = 2 * n_heads * dl
    fix = lambda b, j, pt: (0, 0)
    seq = lambda b, j, pt: (b, 0)

    def page_spec(p, shape):
        return pl.BlockSpec((1, 1) + shape, lambda b, j, pt: (layer, pt[b, j * pps + p], 0, 0))

    grid_spec = pltpu.PrefetchScalarGridSpec(
        num_scalar_prefetch=1,
        grid=(db, n_pages // pps),
        in_specs=[pl.BlockSpec((1, hd), fix)] * 4 + [
            pl.BlockSpec((1, 2 * hd), fix),
            pl.BlockSpec((dl, w), seq), pl.BlockSpec((dl, w), seq), pl.BlockSpec((dl * n_heads, 2 * hd), seq),
            pl.BlockSpec((2, rows, page), lambda b, j, pt: (0, 0, 0)),
        ] + [page_spec(p, (w, page)) for p in range(pps)]
        + [page_spec(p, (page * n_heads, 2 * hd)) for p in range(pps)],
        out_specs=pl.BlockSpec((dl, w), seq),
        scratch_shapes=[pltpu.VMEM((rows, 1), F32), pltpu.VMEM((rows, 1), F32), pltpu.VMEM((rows, 2 * hd), F32),
                        pltpu.VMEM((w, pps * page), BF16), pltpu.VMEM((n_heads, pps * page, 2 * hd), BF16)],
    )
    return pl.pallas_call(
        functools.partial(_attn_s_kernel, pps=pps, n_chunks=n_pages // pps, n_heads=n_heads, hd=hd,
                          lambda_init=lambda_init),
        grid_spec=grid_spec,
        out_shape=jax.ShapeDtypeStruct((t, w), F32),
        compiler_params=_cparams(("parallel", "arbitrary")),
        name="attn_s",
    )(page_table, *lams, subln_g, q, k_new, v_new, bias, *([cache_kt] * pps), *([cache_v] * pps))


def _s5_prep_kernel(are_ref, aim_ref, ldt_ref, bre_ref, bim_ref, lb_ref, bb_ref, pwc_ref, *, seg):
    dt = jnp.exp(ldt_ref[...])
    a_re = jnp.minimum(are_ref[...], -1e-4)
    a_im = aim_ref[...]

    def lam_pow(n):
        mag = jnp.exp(a_re * dt * n)
        ang = a_im * dt * n
        return mag * jnp.cos(ang), mag * jnp.sin(ang)

    lb_re, lb_im = lam_pow(1.0)
    lb_ref[0] = lb_re
    lb_ref[1] = lb_im
    den = a_re * a_re + a_im * a_im
    nr = lb_re - 1.0
    co_re = (nr * a_re + lb_im * a_im) / den
    co_im = (lb_im * a_re - nr * a_im) / den
    bb_ref[0] = co_re * bre_ref[...] - co_im * bim_ref[...]
    bb_ref[1] = co_re * bim_ref[...] + co_im * bre_ref[...]
    n_dbl = pwc_ref.shape[1]
    for s in range(n_dbl):
        c_re, c_im = lam_pow(float(seg * (1 << s)))
        pwc_ref[0, s:s + 1, :] = c_re
        pwc_ref[1, s:s + 1, :] = c_im


def _s5_prep(a_re, a_im, log_dt, b_re, b_im, seg):
    g, p = a_re.shape
    c = b_re.shape[-1]
    gp = g * p
    flat = lambda a: a.reshape(1, gp)
    ldt = jnp.broadcast_to(log_dt[:, None], (g, p)).reshape(1, gp)
    chan = lambda b: jnp.transpose(b, (2, 0, 1)).reshape(c, gp)
    n_dbl = 3
    full = lambda shape: pl.BlockSpec(shape, lambda: (0,) * len(shape))
    return pl.pallas_call(
        functools.partial(_s5_prep_kernel, seg=seg),
        in_specs=[full((1, gp))] * 3 + [full((c, gp))] * 2,
        out_specs=[full((2, 1, gp)), full((2, c, gp)), full((2, n_dbl, gp))],
        out_shape=[jax.ShapeDtypeStruct((2, 1, gp), F32), jax.ShapeDtypeStruct((2, c, gp), F32),
                   jax.ShapeDtypeStruct((2, n_dbl, gp), F32)],
        compiler_params=pltpu.CompilerParams(vmem_limit_bytes=VMEM_LIMIT),
        name="s5_prep",
    )(flat(a_re), flat(a_im), ldt, chan(b_re), chan(b_im))


def _cmul(ar, ai, br, bi):
    return ar * br - ai * bi, ar * bi + ai * br


def _s5_kernel(u_ref, bb_ref, cw_ref, d_ref, lb_ref, pwc_ref, s0_ref, y_ref, st_ref, up_ref, h_ref, hb_ref,
               *, seg, stride, chain):
    cw = lb_ref.shape[2]
    n = SUBLANES
    unroll = math.gcd(seg // 2, 4)
    for i in range(seg):
        up_ref[i * n:(i + 1) * n, :] = u_ref[pl.ds(i, n, stride=stride), :]
    h_ref[...] = jnp.dot(up_ref[...].astype(BF16), bb_ref[0], preferred_element_type=F32)
    lr = jnp.broadcast_to(lb_ref[0], (n, cw))
    li = jnp.broadcast_to(lb_ref[1], (n, cw))

    rows2 = 2 * n

    def step(i, hr, hi):
        r = pl.multiple_of(i * n, n)
        tr, ti = _cmul(lr, li, hr, hi)
        return tr + h_ref[pl.ds(r, n), :cw], ti + h_ref[pl.ds(r, n), cw:]

    def scan(k, carry):
        hr0, hi0 = step(2 * k, *carry)
        hr1, hi1 = step(2 * k + 1, hr0, hi0)
        r = pl.multiple_of(k * rows2, rows2)
        if chain:
            h_ref[pl.ds(r, rows2), :cw] = jnp.concatenate([hr0, hr1], axis=0)
            h_ref[pl.ds(r, rows2), cw:] = jnp.concatenate([hi0, hi1], axis=0)
        else:
            hb_ref[pl.ds(r, rows2), :cw] = jnp.concatenate([hr0, hr1], axis=0).astype(BF16)
            hb_ref[pl.ds(r, rows2), cw:] = jnp.concatenate([hi0, hi1], axis=0).astype(BF16)
        return hr1, hi1

    end_r, end_i = lax.fori_loop(0, seg // 2, scan, (s0_ref[0, 0], s0_ref[0, 1]), unroll=unroll)

    if chain:
        sub = lax.broadcasted_iota(I32, (n, cw), 0)

        def shift(x, d):
            return jnp.where(sub >= d, pltpu.roll(x, d, axis=0), 0.0)

        tr, ti = end_r, end_i
        for s in range(pwc_ref.shape[1]):
            d = 1 << s
            pr = jnp.broadcast_to(pwc_ref[0, s:s + 1, :], (n, cw))
            pi = jnp.broadcast_to(pwc_ref[1, s:s + 1, :], (n, cw))
            ar, ai = _cmul(pr, pi, shift(tr, d), shift(ti, d))
            tr, ti = tr + ar, ti + ai
        end_r, end_i = tr, ti
        cr, ci = shift(tr, 1), shift(ti, 1)

        def fix(k, carry):
            r = pl.multiple_of(k * rows2, rows2)
            ar0, ai0 = _cmul(lr, li, *carry)
            ar1, ai1 = _cmul(lr, li, ar0, ai0)
            hb_ref[pl.ds(r, rows2), :cw] = (h_ref[pl.ds(r, rows2), :cw] + jnp.concatenate([ar0, ar1], axis=0)).astype(BF16)
            hb_ref[pl.ds(r, rows2), cw:] = (h_ref[pl.ds(r, rows2), cw:] + jnp.concatenate([ai0, ai1], axis=0)).astype(BF16)
            return ar1, ai1

        lax.fori_loop(0, seg // 2, fix, (cr, ci), unroll=unroll)

    st_ref[0, 0, 0] = end_r
    st_ref[0, 0, 1] = end_i
    y = jnp.dot(hb_ref[...], cw_ref[0], preferred_element_type=F32) + d_ref[...] * up_ref[...]
    up_ref[...] = jax.nn.gelu(y)
    for i in range(seg):
        y_ref[pl.ds(i, n, stride=stride), :] = up_ref[i * n:(i + 1) * n, :]


def _s5(u, bb_bd, cw_bd, d_vec, lb, pwc, s0, n_seq, seg, stride, chain):
    t, w = u.shape
    ncc, uw, cw2 = bb_bd.shape
    cw = cw2 // 2
    rows = SUBLANES * seg
    return pl.pallas_call(
        functools.partial(_s5_kernel, seg=seg, stride=stride, chain=chain),
        grid=(n_seq, ncc),
        in_specs=[
            pl.BlockSpec((rows, uw), lambda b, c: (b, c)),
            pl.BlockSpec((1, uw, cw2), lambda b, c: (c, 0, 0)),
            pl.BlockSpec((1, cw2, uw), lambda b, c: (c, 0, 0)),
            pl.BlockSpec((1, uw), lambda b, c: (0, c)),
            pl.BlockSpec((2, 1, cw), lambda b, c: (0, 0, c)),
            pl.BlockSpec((2, pwc.shape[1], cw), lambda b, c: (0, 0, c)),
            pl.BlockSpec((1, 2, SUBLANES, cw), lambda b, c: (b, 0, 0, c)),
        ],
        out_specs=[
            pl.BlockSpec((rows, uw), lambda b, c: (b, c)),
            pl.BlockSpec((1, 1, 2, SUBLANES, cw), lambda b, c: (b, c, 0, 0, 0)),
        ],
        out_shape=[jax.ShapeDtypeStruct((t, w), F32),
                   jax.ShapeDtypeStruct((n_seq, ncc, 2, SUBLANES, cw), F32)],
        scratch_shapes=[pltpu.VMEM((rows, uw), F32), pltpu.VMEM((rows, cw2), F32), pltpu.VMEM((rows, cw2), BF16)],
        compiler_params=_cparams(("parallel", "parallel")),
        name="s5_chain" if chain else "s5_step",
    )(u, bb_bd, cw_bd, d_vec, lb, pwc, s0)


def _block_diag_weights(bb, c_re, c_im, cw):
    _, c, gp = bb.shape
    g = c_re.shape[0]
    p = gp // g
    gpc = cw // p
    ncc = g // gpc
    eye = jnp.eye(gpc, dtype=F32)
    b4 = jnp.transpose(bb.reshape(2, c, ncc, gpc, p), (2, 0, 3, 1, 4))
    bbd = b4[:, :, :, :, None, :] * eye[None, None, :, None, :, None]
    bbd = jnp.transpose(bbd.reshape(ncc, 2, gpc * c, cw), (0, 2, 1, 3)).reshape(ncc, gpc * c, 2 * cw)
    cc = jnp.stack([c_re, -c_im]).reshape(2, ncc, gpc, c, p)
    c4 = jnp.transpose(cc, (1, 0, 2, 4, 3))
    cbd = c4[:, :, :, :, None, :] * eye[None, None, :, None, :, None]
    cbd = cbd.reshape(ncc, 2 * cw, gpc * c)
    return bbd.astype(BF16), cbd.astype(BF16)


def _merge_kernel(x_ref, oa_ref, yg_ref, gm_ref, wg_ref, wglu_ref, bglu_ref, wa_ref, ws_ref, wo_ref, gf_ref,
                  wr_ref, br_ref, h_ref, hn_ref, ti_ref, tp_ref, lr_ref, cnt_ref):
    x = x_ref[...]
    tm, d = x.shape
    dot = functools.partial(jnp.dot, preferred_element_type=F32)
    xn = _rms(x, gm_ref[...]).astype(BF16)
    gates = jax.nn.sigmoid(dot(xn, wg_ref[...]))
    yg = yg_ref[...]
    ys = yg * jax.nn.sigmoid(dot(yg.astype(BF16), wglu_ref[...]) + bglu_ref[...])
    merged = (gates[:, :d] * dot(oa_ref[...].astype(BF16), wa_ref[...])
              + gates[:, d:] * dot(ys.astype(BF16), ws_ref[...]))
    h = x + dot(merged.astype(BF16), wo_ref[...])
    h_ref[...] = h
    hn = _rms(h, gf_ref[...]).astype(BF16)
    hn_ref[...] = hn
    logit = lax.dot_general(wr_ref[...], hn, (((1,), (1,)), ((), ())), preferred_element_type=F32) + br_ref[...]
    n_e = logit.shape[0]
    eid = lax.broadcasted_iota(I32, logit.shape, 0)
    vals, idxs = [], []
    for _ in range(TOP_K):
        mx = jnp.max(logit, axis=0, keepdims=True)
        ix = jnp.min(jnp.where(logit == mx, eid, n_e), axis=0, keepdims=True)
        vals.append(mx)
        idxs.append(ix)
        logit = jnp.where(eid == ix, -jnp.inf, logit)
    ex = [jnp.exp(v - vals[0]) for v in vals]
    tot = ex[0]
    for e in ex[1:]:
        tot = tot + e
    ti_ref[...] = jnp.concatenate(idxs, axis=0)
    tp_ref[...] = jnp.concatenate(ex, axis=0) / tot
    onehot = [jnp.where(eid == ix, 1.0, 0.0) for ix in idxs]
    before = (lax.broadcasted_iota(I32, (tm, tm), 0) < lax.broadcasted_iota(I32, (tm, tm), 1))
    prefix = dot(jnp.concatenate(onehot, axis=0).astype(BF16), jnp.where(before, 1.0, 0.0).astype(BF16))
    base = jnp.zeros((n_e, 1), F32)
    ranks = []
    for k in range(TOP_K):
        ranks.append(jnp.sum(onehot[k] * (prefix[k * n_e:(k + 1) * n_e] + base), axis=0, keepdims=True))
        base = base + jnp.sum(onehot[k], axis=1, keepdims=True)
    lr_ref[...] = jnp.concatenate(ranks, axis=0).astype(I32)
    cnt_ref[0] = base.astype(I32)


def _merge(x, oa, yg, w, tm):
    t, d = x.shape
    aw = oa.shape[1]
    sw = yg.shape[1]
    n_e = w["wr"].shape[0]
    row = lambda i: (i, 0)
    col = lambda i: (0, i)
    fix = lambda i: (0, 0)
    full = lambda a: pl.BlockSpec(a.shape, fix)
    names = ["gm", "wg", "wglu", "bglu", "wa", "ws", "wo", "gf", "wr", "br"]
    kt = pl.BlockSpec((TOP_K, tm), col)
    return pl.pallas_call(
        _merge_kernel,
        grid=(t // tm,),
        in_specs=[pl.BlockSpec((tm, d), row), pl.BlockSpec((tm, aw), row), pl.BlockSpec((tm, sw), row)]
        + [full(w[n]) for n in names],
        out_specs=[pl.BlockSpec((tm, d), row), pl.BlockSpec((tm, d), row), kt, kt, kt,
                   pl.BlockSpec((1, n_e, 1), lambda i: (i, 0, 0))],
        out_shape=[jax.ShapeDtypeStruct((t, d), F32), jax.ShapeDtypeStruct((t, d), BF16),
                   jax.ShapeDtypeStruct((TOP_K, t), I32), jax.ShapeDtypeStruct((TOP_K, t), F32),
                   jax.ShapeDtypeStruct((TOP_K, t), I32), jax.ShapeDtypeStruct((t // tm, n_e, 1), I32)],
        compiler_params=_cparams(("parallel",)),
        name="merge",
    )(x, oa, yg, *[w[n] for n in names])


def _chunks_per_tile(tm, n_experts):
    rows = TOP_K * tm + (SUBLANES - 1) * n_experts
    return -(-rows // (SUBLANES * SUBLANES)) * SUBLANES


CHUNK_GROUP = 8


def _route_plan(cnt, moe_tile):
    n_e = cnt.shape[1]
    cnt8 = (cnt + SUBLANES - 1) // SUBLANES * SUBLANES
    total = jnp.sum(cnt8, axis=0)
    padded = (total + moe_tile - 1) // moe_tile * moe_tile
    pad_end = jnp.cumsum(padded)
    tile_base = (pad_end - padded)[None, :] + jnp.cumsum(cnt8, axis=0) - cnt8
    off_end = jnp.cumsum(cnt8, axis=1)
    off8 = off_end - cnt8
    nch = off_end[:, -1] // SUBLANES
    ngrp = (nch + CHUNK_GROUP - 1) // CHUNK_GROUP
    return dict(padded=padded.astype(I32), pad_end=pad_end.astype(I32), tile_base=tile_base, off8=off8,
                off_end=off_end, nch=nch.astype(I32), ngrp=ngrp.astype(I32), n_e=n_e)


def _chunk_rows(plan, lo, hi, g_max, spare_row):
    off_end = plan["off_end"][lo:hi]
    delta = (plan["tile_base"] - plan["off8"])[lo:hi]
    row0 = jnp.arange(g_max, dtype=I32) * SUBLANES
    grp = jnp.sum((off_end[:, None, :] <= row0[None, :, None]).astype(I32), axis=-1)
    grp = jnp.minimum(grp, plan["n_e"] - 1)
    sel = grp[:, :, None] == jnp.arange(plan["n_e"], dtype=I32)[None, None, :]
    dst = row0[None, :] + jnp.sum(jnp.where(sel, delta[:, None, :], 0), axis=-1)
    g = jnp.arange(g_max, dtype=I32)[None, :]
    nch = plan["nch"][lo:hi, None]
    parity = (jnp.arange(lo, hi, dtype=I32) % 2)[:, None]
    spare = spare_row + (parity * CHUNK_GROUP + g - nch) * SUBLANES
    return jnp.where(g < nch, dst, spare).astype(I32)[:, None, :]


def _dispatch_kernel(nch_ref, pe_ref, pd_ref, dst_ref, hnp_ref, hns_ref, ix_ref, lr_ref, pr_ref, off_ref, pos_ref,
                     xs_hbm, xbuf, zbuf, sems, zsem, *, n, n_p, moe_tile):
    i = pl.program_id(0)
    tm = hnp_ref.shape[0]
    rows = xbuf.shape[1]
    n_e = off_ref.shape[1]
    n_moe_tiles = xs_hbm.shape[0] // moe_tile
    slot = lax.rem(i, 2)

    def chunk(s, g, row):
        r = pl.multiple_of(row, SUBLANES)
        src = xbuf.at[s, pl.ds(pl.multiple_of(g * SUBLANES, SUBLANES), SUBLANES), :]
        return pltpu.make_async_copy(src, xs_hbm.at[pl.ds(r, SUBLANES), :], sems.at[s])

    def drain(s, groups):
        grp_rows = CHUNK_GROUP * SUBLANES
        for bit in range((rows // grp_rows).bit_length()):
            span = grp_rows << bit
            if span > rows:
                continue

            @pl.when((groups >> bit) & 1 == 1)
            def _():
                pltpu.make_async_copy(xbuf.at[s, pl.ds(0, span), :], xs_hbm.at[pl.ds(0, span), :], sems.at[s]).wait()

    @pl.when(i == 0)
    def _():
        zbuf[...] = jnp.zeros(zbuf.shape, zbuf.dtype)
        n_used = pe_ref[n_e - 1] // moe_tile

        def zero_tile(row):
            r = pl.multiple_of(row, SUBLANES)
            return pltpu.make_async_copy(zbuf, xs_hbm.at[pl.ds(r, moe_tile), :], zsem.at[0])

        for e in range(n_e):
            @pl.when(pd_ref[e] > 0)
            def _():
                zero_tile(jnp.maximum(pe_ref[e] - moe_tile, 0)).start()

        def start_rest(t, c):
            zero_tile(t * moe_tile).start()
            return c
        lax.fori_loop(n_used, n_moe_tiles, start_rest, 0)
        for e in range(n_e):
            @pl.when(pd_ref[e] > 0)
            def _():
                zero_tile(0).wait()

        def wait_rest(t, c):
            zero_tile(0).wait()
            return c
        lax.fori_loop(n_used, n_moe_tiles, wait_rest, 0)

    @pl.when(i >= 2)
    def _():
        drain(slot, nch_ref[jnp.maximum(i - 2, 0)])

    hn = jnp.where(i < n_p, hnp_ref[...], hns_ref[...])
    eid = lax.broadcasted_iota(I32, (n_e, tm), 0)
    off = off_ref[0].astype(F32)
    r_iota = lax.broadcasted_iota(I32, (rows, tm), 0)
    prob = jnp.zeros((rows, tm), F32)
    pos_rows = []
    for k in range(TOP_K):
        onehot = jnp.where(eid == ix_ref[k:k + 1, :], 1.0, 0.0)
        pos = jnp.sum(onehot * off, axis=0, keepdims=True).astype(I32) + lr_ref[k:k + 1, :]
        pos_rows.append(pos)
        prob = jnp.where(r_iota == pos, pr_ref[k:k + 1, :], prob)
    sel = jnp.where(prob != 0.0, 1.0, 0.0)
    pos_ref[...] = jnp.concatenate(pos_rows, axis=0)
    d = hn.shape[1]
    xbuf[slot, :, :d] = jnp.dot(sel.astype(BF16), hn, preferred_element_type=F32)
    xbuf[slot, :, d:] = jnp.broadcast_to(jnp.sum(prob, axis=1, keepdims=True), (rows, LANES))

    def send(grp, c):
        for u in range(CHUNK_GROUP):
            g = grp * CHUNK_GROUP + u
            chunk(slot, g, dst_ref[0, 0, g]).start()
        return c
    lax.fori_loop(0, nch_ref[i], send, 0)

    @pl.when(i == n - 1)
    def _():
        drain(slot, nch_ref[i])
        if n >= 2:
            drain(1 - slot, nch_ref[jnp.maximum(i - 1, 0)])


def _dispatch(plan, tm, hn_p, hn_s, top_i, lrank, probs, s_pad, moe_tile):
    d = hn_p.shape[1]
    n_p, n_s = hn_p.shape[0] // tm, hn_s.shape[0] // tm
    n = n_p + n_s
    n_e = plan["n_e"]
    g_max = _chunks_per_tile(tm, n_e)
    rows = g_max * SUBLANES
    dst = _chunk_rows(plan, 0, n, g_max, s_pad - moe_tile)
    off = plan["off8"].astype(I32)[:, :, None]
    kt = pl.BlockSpec((TOP_K, tm), lambda i, *_: (0, i))
    grid_spec = pltpu.PrefetchScalarGridSpec(
        num_scalar_prefetch=3,
        grid=(n,),
        in_specs=[
            pl.BlockSpec((1, 1, g_max), lambda i, *_: (i, 0, 0), memory_space=pltpu.SMEM),
            pl.BlockSpec((tm, d), lambda i, *_: (jnp.minimum(i, n_p - 1), 0)),
            pl.BlockSpec((tm, d), lambda i, *_: (jnp.maximum(i - n_p, 0), 0)),
            kt, kt, kt,
            pl.BlockSpec((1, n_e, 1), lambda i, *_: (i, 0, 0)),
        ],
        out_specs=[kt, pl.BlockSpec(memory_space=pl.ANY)],
        scratch_shapes=[pltpu.VMEM((2, rows, d + LANES), F32), pltpu.VMEM((moe_tile, d + LANES), F32),
                        pltpu.SemaphoreType.DMA((2,)), pltpu.SemaphoreType.DMA((1,))])
    return pl.pallas_call(
        functools.partial(_dispatch_kernel, n=n, n_p=n_p, moe_tile=moe_tile),
        grid_spec=grid_spec,
        out_shape=[jax.ShapeDtypeStruct((TOP_K, n * tm), I32), jax.ShapeDtypeStruct((s_pad, d + LANES), F32)],
        compiler_params=_cparams(("arbitrary",)),
        name="dispatch",
    )(plan["ngrp"], plan["pad_end"], plan["padded"], dst, hn_p, hn_s, top_i, lrank, probs, off)


def _moe_kernel(te_ref, nu_ref, x_ref, wg_ref, bg_ref, wu_ref, bu_ref, wd_ref, bd_ref, o_ref, wgb, wub, wdb):
    i = pl.program_id(0)
    used = i < nu_ref[0]
    e = te_ref[i]
    prev = te_ref[jnp.maximum(i - 1, 0)]

    @pl.when(used & ((i == 0) | (e != prev)))
    def _():
        wgb[...] = wg_ref[0].astype(BF16)
        wub[...] = wu_ref[0].astype(BF16)
        wdb[...] = wd_ref[0].astype(BF16)

    @pl.when(used)
    def _():
        d = o_ref.shape[1]
        x = x_ref[:, :d].astype(BF16)
        gt = jnp.dot(x, wgb[...], preferred_element_type=F32) + bg_ref[0]
        up = jnp.dot(x, wub[...], preferred_element_type=F32) + bu_ref[0]
        gt = jnp.minimum(gt, SWIGLU_LIMIT)
        up = jnp.clip(up, -SWIGLU_LIMIT, SWIGLU_LIMIT)
        hh = (up + 1.0) * (gt * jax.nn.sigmoid(SWIGLU_ALPHA * gt))
        o_ref[...] = (jnp.dot(hh.astype(BF16), wdb[...], preferred_element_type=F32) + bd_ref[0]) * x_ref[:, d:d + 1]

    @pl.when(jnp.logical_not(used))
    def _():
        o_ref[...] = jnp.zeros(o_ref.shape, F32)


def _moe(tile_expert, n_used, xs, wg, bg, wu, bu, wd, bd, tm):
    n_e, d, f = wg.shape
    s_pad = xs.shape[0]
    wmap = lambda i, te, nu: (te[i], 0, 0)
    row = lambda i, te, nu: (i, 0)
    grid_spec = pltpu.PrefetchScalarGridSpec(
        num_scalar_prefetch=2,
        grid=(s_pad // tm,),
        in_specs=[
            pl.BlockSpec((tm, d + LANES), row),
            pl.BlockSpec((1, d, f), wmap), pl.BlockSpec((1, 1, f), wmap),
            pl.BlockSpec((1, d, f), wmap), pl.BlockSpec((1, 1, f), wmap),
            pl.BlockSpec((1, f, d), wmap), pl.BlockSpec((1, 1, d), wmap),
        ],
        out_specs=pl.BlockSpec((tm, d), row),
        scratch_shapes=[pltpu.VMEM((d, f), BF16), pltpu.VMEM((d, f), BF16), pltpu.VMEM((f, d), BF16)],
    )
    return pl.pallas_call(
        _moe_kernel,
        grid_spec=grid_spec,
        out_shape=jax.ShapeDtypeStruct((s_pad, d), F32),
        compiler_params=pltpu.CompilerParams(dimension_semantics=("arbitrary",),
                                             vmem_limit_bytes=56 * 1024 * 1024),
        name="moe",
    )(tile_expert, n_used, xs, wg, bg.reshape(n_e, 1, f), wu, bu.reshape(n_e, 1, f), wd, bd.reshape(n_e, 1, d))


def _combine_kernel(nch_ref, dst_ref, nxt_ref, h_ref, pos_ref, g_ref, o_hbm, y_ref, obuf, sems, *, n):
    i = pl.program_id(0)
    tm = h_ref.shape[0]
    rows = obuf.shape[1]
    slot = lax.rem(i, 2)

    def chunk(s, g, row):
        r = pl.multiple_of(row, SUBLANES)
        dst = obuf.at[s, pl.ds(pl.multiple_of(g * SUBLANES, SUBLANES), SUBLANES), :]
        return pltpu.make_async_copy(o_hbm.at[pl.ds(r, SUBLANES), :], dst, sems.at[s])

    def fetch(s, rows_ref, groups):
        def body(grp, c):
            for u in range(CHUNK_GROUP):
                g = grp * CHUNK_GROUP + u
                chunk(s, g, rows_ref[0, 0, g]).start()
            return c
        lax.fori_loop(0, groups, body, 0)

    @pl.when(i == 0)
    def _():
        obuf[...] = jnp.zeros(obuf.shape, obuf.dtype)
        fetch(0, dst_ref, nch_ref[0])

    @pl.when(i + 1 < n)
    def _():
        fetch(1 - slot, nxt_ref, nch_ref[jnp.minimum(i + 1, n - 1)])

    count = nch_ref[i]
    grp_rows = CHUNK_GROUP * SUBLANES
    for bit in range((rows // grp_rows).bit_length()):
        span = grp_rows << bit
        if span > rows:
            continue

        @pl.when((count >> bit) & 1 == 1)
        def _():
            pltpu.make_async_copy(o_hbm.at[pl.ds(0, span), :], obuf.at[slot, pl.ds(0, span), :], sems.at[slot]).wait()

    ob = obuf[slot].astype(BF16)
    lane = lax.broadcasted_iota(I32, (tm, rows), 1)
    w = jnp.zeros((tm, rows), F32)
    for k in range(TOP_K):
        w = w + jnp.where(lane == pos_ref[:, k:k + 1], 1.0, 0.0)
    moe = jnp.dot(w.astype(BF16), ob, preferred_element_type=F32)
    y_ref[...] = _rms(h_ref[...] + moe, g_ref[...])


def _combine(plan, lo, tm, h, pos_t, g, o_sorted, spare_row):
    t, d = h.shape
    n = t // tm
    g_max = _chunks_per_tile(tm, plan["n_e"])
    rows = g_max * SUBLANES
    dst = _chunk_rows(plan, lo, lo + n, g_max, spare_row)
    tok = lambda i, *_: (i, 0)
    grid_spec = pltpu.PrefetchScalarGridSpec(
        num_scalar_prefetch=1,
        grid=(n,),
        in_specs=[
            pl.BlockSpec((1, 1, g_max), lambda i, *_: (i, 0, 0), memory_space=pltpu.SMEM),
            pl.BlockSpec((1, 1, g_max), lambda i, *_: (jnp.minimum(i + 1, n - 1), 0, 0), memory_space=pltpu.SMEM),
            pl.BlockSpec((tm, d), tok), pl.BlockSpec((tm, TOP_K), tok),
            pl.BlockSpec((1, d), lambda i, *_: (0, 0)),
            pl.BlockSpec(memory_space=pl.ANY),
        ],
        out_specs=pl.BlockSpec((tm, d), tok),
        scratch_shapes=[pltpu.VMEM((2, rows, d), F32), pltpu.SemaphoreType.DMA((2,))],
    )
    return pl.pallas_call(
        functools.partial(_combine_kernel, n=n),
        grid_spec=grid_spec,
        out_shape=jax.ShapeDtypeStruct((t, d), F32),
        compiler_params=_cparams(("arbitrary",)),
        name="combine",
    )(plan["ngrp"][lo:lo + n], dst, dst, h, pos_t, g, o_sorted)


def kernel(x_prompt, x_sample, cache_k, cache_v, state_ssm_re, state_ssm_im, page_table, norm_mix_g, w_in, lambda_q1, lambda_k1, lambda_q2, lambda_k2, subln_g, rel_bias, ssm_a_re, ssm_a_im, ssm_log_dt, ssm_b_re, ssm_b_im, ssm_c_re, ssm_c_im, ssm_d, w_glu, b_glu, w_attn_up, w_ssm_up, w_out, norm_ffn_g, w_router, b_router, w_e_gate, b_e_gate, w_e_up, b_e_up, w_e_down, b_e_down, norm_final_g):
    bp, seq, d = x_prompt.shape
    db, dl, _ = x_sample.shape
    depth, n_pool, page, n_heads, _, hd = cache_k.shape
    vd = cache_v.shape[-1]
    n_groups, n_state = ssm_a_re.shape[1:]
    gch = ssm_b_re.shape[-1]
    n_experts = w_router.shape[-1]
    qk_w = n_heads * 2 * hd
    attn_w = n_heads * vd
    ssm_w = n_groups * gch
    mix_w = 2 * qk_w + attn_w + ssm_w
    assert qk_w == attn_w == ssm_w and vd == 2 * hd
    assert seq % SUBLANES == 0 and db % SUBLANES == 0 and dl == SUBLANES
    tp, ts = bp * seq, db * dl
    scale = hd ** -0.5
    seg = seq // SUBLANES
    cw = min(S5_CHUNK, n_groups * n_state)

    ckt = jnp.transpose(cache_k, (0, 1, 3, 4, 5, 2)).reshape(depth, n_pool, qk_w, page)
    cv = cache_v.reshape(depth, n_pool, page * n_heads, vd)

    blk = math.gcd(ATTN_BLOCK, seq)
    qi = np.arange(blk)[:, None]
    ki = np.arange(blk)[None, :]
    bkt_p = np.stack([_bucket_tile(qi - ki), _bucket_tile(blk + qi - ki)])
    sq = np.tile(np.arange(dl), 2)[:, None]
    sk = np.arange(page)[None, :]
    new_dist = np.where(sk < dl, sq - sk, -1)
    bkt_s = np.stack([_bucket_tile(page + sq - sk), _bucket_tile(new_dist)])

    hp = x_prompt.reshape(tp, d)
    hs = x_sample.reshape(ts, d)
    outs = {n: [] for n in ("kp", "vp", "ks", "vs", "rp", "ip", "rs", "is")}
    tm_p = math.gcd(TOKEN_TILE, tp)
    tm_s = math.gcd(TOKEN_TILE, ts)
    for l in range(depth):
        lambda_init = 0.8 - 0.6 * math.exp(-0.3 * l)
        w_l = w_in[l].astype(BF16)
        g_mix = norm_mix_g[l].reshape(1, d)
        lams = [a[l].reshape(1, hd) for a in (lambda_q1, lambda_k1, lambda_q2, lambda_k2)]
        sub_g = subln_g[l].reshape(1, vd)

        w_qvu = jnp.concatenate([w_l[:, :qk_w], w_l[:, 2 * qk_w:mix_w]], axis=1)
        w_kt = jnp.transpose(w_l[:, qk_w:2 * qk_w])
        qp, ktp, vp, up, ktpb, vpb = _in_proj(hp, g_mix, w_qvu, w_kt, scale, tm_p, page, n_heads)
        qs, kts, vs, us, _, _ = _in_proj(hs, g_mix, w_qvu, w_kt, scale, tm_s, None, n_heads)
        ks = jnp.transpose(kts)

        bias_p = _rel_bias_tiles(rel_bias, bkt_p)
        bias_s = _rel_bias_tiles(rel_bias, bkt_s)
        bias_s = jnp.transpose(bias_s, (1, 0, 2, 3)).reshape(2, n_heads * 2 * dl, page)

        oa_p = _attn_prompt(lams, sub_g, qp, ktpb, vpb, bias_p, bp, seq, n_heads, lambda_init)
        oa_s = _attn_sample(page_table, lams, sub_g, qs, ks, vs, bias_s, ckt, cv, l, n_heads, lambda_init)

        lb, bb, pwc = _s5_prep(ssm_a_re[l], ssm_a_im[l], ssm_log_dt[l], ssm_b_re[l], ssm_b_im[l], seg)
        bb_bd, cw_bd = _block_diag_weights(bb, ssm_c_re[l], ssm_c_im[l], cw)
        d_vec = ssm_d[l].reshape(1, ssm_w)
        zeros = jnp.zeros((bp, 2, SUBLANES, n_groups * n_state), F32)
        yg_p, st_p = _s5(up, bb_bd, cw_bd, d_vec, lb, pwc, zeros, bp, seg, seg, True)
        s0 = jnp.stack([state_ssm_re[l], state_ssm_im[l]], axis=1).reshape(
            db // SUBLANES, SUBLANES, 2, n_groups * n_state)
        s0 = jnp.transpose(s0, (0, 2, 1, 3))
        yg_s, st_s = _s5(us, bb_bd, cw_bd, d_vec, lb, pwc, s0, db // SUBLANES, dl, dl, False)

        wts = dict(
            gm=g_mix, wg=w_l[:, mix_w:], wglu=w_glu[l].astype(BF16), bglu=b_glu[l].reshape(1, ssm_w),
            wa=w_attn_up[l].astype(BF16), ws=w_ssm_up[l].astype(BF16), wo=w_out[l].astype(BF16),
            gf=norm_ffn_g[l].reshape(1, d), wr=jnp.transpose(w_router[l]).astype(BF16),
            br=b_router[l].reshape(n_experts, 1))
        rt = math.gcd(ROUTE_TILE, math.gcd(tp, ts))
        h_p, hn_p, ti_p, pr_p, lr_p, cnt_p = _merge(hp, oa_p, yg_p, wts, rt)
        h_s, hn_s, ti_s, pr_s, lr_s, cnt_s = _merge(hs, oa_s, yg_s, wts, rt)

        n_p, n_s = tp // rt, ts // rt
        plan = _route_plan(jnp.concatenate([cnt_p[:, :, 0], cnt_s[:, :, 0]], axis=0), MOE_TILE)
        worst = TOP_K * (tp + ts) + (SUBLANES - 1) * n_experts * (n_p + n_s) + n_experts * (MOE_TILE - 1)
        s_pad = (-(-worst // MOE_TILE) + 1) * MOE_TILE
        pos, xs = _dispatch(plan, rt, hn_p, hn_s, jnp.concatenate([ti_p, ti_s], axis=1),
                            jnp.concatenate([lr_p, lr_s], axis=1), jnp.concatenate([pr_p, pr_s], axis=1),
                            s_pad, MOE_TILE)
        tile_start = jnp.arange(s_pad // MOE_TILE, dtype=I32) * MOE_TILE
        tile_expert = jnp.minimum(jnp.sum((plan["pad_end"][None, :] <= tile_start[:, None]).astype(I32), axis=1),
                                  n_experts - 1)
        n_used = (plan["pad_end"][-1:] // MOE_TILE).astype(I32)
        o_sorted = _moe(tile_expert, n_used, xs, w_e_gate[l], b_e_gate[l], w_e_up[l], b_e_up[l],
                        w_e_down[l], b_e_down[l], MOE_TILE)

        last = l == depth - 1
        g_out = norm_final_g.reshape(1, d) if last else None
        assert last, "multi-layer stacks need a combine variant without the final norm"
        pos_t = jnp.transpose(pos)
        hp = _combine(plan, 0, rt, h_p, pos_t[:tp], g_out, o_sorted, s_pad - MOE_TILE)
        hs = _combine(plan, n_p, rt, h_s, pos_t[tp:], g_out, o_sorted, s_pad - MOE_TILE)

        gp = n_groups * n_state
        st_p = jnp.transpose(st_p, (0, 2, 3, 1, 4)).reshape(bp, 2, SUBLANES, gp)[:, :, SUBLANES - 1]
        st_s = jnp.transpose(st_s, (0, 3, 2, 1, 4)).reshape(db, 2, gp)
        outs["kp"].append(jnp.transpose(ktp.reshape(bp, seq // page, n_heads, 2, hd, page), (0, 1, 5, 2, 3, 4)))
        outs["vp"].append(vp.reshape(bp, seq // page, page, n_heads, vd))
        outs["ks"].append(ks.reshape(db, dl, n_heads, 2, hd))
        outs["vs"].append(vs.reshape(db, dl, n_heads, vd))
        outs["rp"].append(st_p[:, 0].reshape(bp, n_groups, n_state))
        outs["ip"].append(st_p[:, 1].reshape(bp, n_groups, n_state))
        outs["rs"].append(st_s[:, 0].reshape(db, n_groups, n_state))
        outs["is"].append(st_s[:, 1].reshape(db, n_groups, n_state))

    st = lambda n: jnp.stack(outs[n])
    return (hp.reshape(bp, seq, d), hs.reshape(db, dl, d), st("kp"), st("vp"), st("ks"), st("vs"),
            st("rp"), st("ip"), st("rs"), st("is"))
```

```python
import functools
import math

import numpy as np
import jax
import jax.numpy as jnp
from jax import lax
from jax.experimental import pallas as pl
from jax.experimental.pallas import tpu as pltpu

F32 = jnp.float32
BF16 = jnp.bfloat16
I32 = jnp.int32

EPS = 1e-6
NEG_INF = -1e30
TOP_K = 4
N_BUCKETS = 32
MAX_EXACT = N_BUCKETS // 2
MAX_DISTANCE = 128
SWIGLU_LIMIT = 7.0
SWIGLU_ALPHA = 1.702

SUBLANES = 8
LANES = 128
VMEM_LIMIT = 48 * 1024 * 1024

ATTN_BLOCK = 512
PAGES_PER_STEP = 16
TOKEN_TILE = 512
MOE_TILE = 512
ROUTE_TILE = 256
S5_CHUNK = 512


def _cparams(sem):
    return pltpu.CompilerParams(dimension_semantics=sem, vmem_limit_bytes=VMEM_LIMIT)


def _rms(x, g):
    return x * lax.rsqrt(jnp.mean(x * x, axis=-1, keepdims=True) + EPS) * g


def _in_proj_kernel(x_ref, g_ref, w_ref, wkt_ref, q_ref, k_ref, v_ref, u_ref, kb_ref, vb_ref, *, scale, page, n_heads):
    xn = _rms(x_ref[...], g_ref[...]).astype(BF16)
    p = jnp.dot(xn, w_ref[...], preferred_element_type=F32)
    w3 = p.shape[1] // 3
    tm = p.shape[0]
    vd = w3 // n_heads
    q_ref[...] = p[:, :w3] * scale
    v = p[:, w3:2 * w3]
    for h in range(n_heads):
        v_ref[pl.ds(h, tm, stride=n_heads), :] = v[:, h * vd:(h + 1) * vd]
    vb_ref[...] = v.astype(BF16)
    u_ref[...] = p[:, 2 * w3:]
    kt = lax.dot_general(wkt_ref[...], xn, (((1,), (1,)), ((), ())), preferred_element_type=F32)
    if page is None:
        k_ref[...] = kt
        kb_ref[...] = kt.astype(BF16)
    else:
        for pg in range(kt.shape[1] // page):
            tile = kt[:, pg * page:(pg + 1) * page]
            k_ref[pg] = tile
            kb_ref[pg] = tile.astype(BF16)


def _in_proj(x, g, w_qvu, w_kt, scale, tm, page, n_heads):
    t, d = x.shape
    w3 = w_qvu.shape[1] // 3
    row = lambda i: (i, 0)
    fix = lambda i: (0, 0)
    f32o = jax.ShapeDtypeStruct((t, w3), F32)
    if page is None:
        kt_shape, kt_spec = (w3, t), pl.BlockSpec((w3, tm), lambda i: (0, i))
    else:
        kt_shape, kt_spec = (t // page, w3, page), pl.BlockSpec((tm // page, w3, page), lambda i: (i, 0, 0))
    tok = pl.BlockSpec((tm, w3), row)
    return pl.pallas_call(
        functools.partial(_in_proj_kernel, scale=scale, page=page, n_heads=n_heads),
        grid=(t // tm,),
        in_specs=[pl.BlockSpec((tm, d), row), pl.BlockSpec((1, d), fix), pl.BlockSpec((d, 3 * w3), fix),
                  pl.BlockSpec((w3, d), fix)],
        out_specs=[tok, kt_spec, pl.BlockSpec((tm * n_heads, w3 // n_heads), row), tok, kt_spec, tok],
        out_shape=[f32o, jax.ShapeDtypeStruct(kt_shape, F32),
                   jax.ShapeDtypeStruct((t * n_heads, w3 // n_heads), F32), f32o,
                   jax.ShapeDtypeStruct(kt_shape, BF16), jax.ShapeDtypeStruct((t, w3), BF16)],
        compiler_params=_cparams(("parallel",)),
        name="in_proj",
    )(x, g, w_qvu, w_kt)


def _bucket_np(n):
    nf = np.maximum(n, 1).astype(np.float64)
    large = MAX_EXACT + np.trunc(
        np.log(nf / MAX_EXACT) / math.log(MAX_DISTANCE / MAX_EXACT) * (N_BUCKETS - MAX_EXACT)).astype(np.int64)
    return np.where(n < MAX_EXACT, n, np.minimum(large, N_BUCKETS - 1)).astype(np.int32)


def _bucket_tile(dist):
    return np.where(dist < 0, -1, _bucket_np(np.maximum(dist, 0))).astype(np.int32)


def _rel_bias_kernel(rb_ref, bkt_ref, o_ref):
    h = pl.program_id(0)
    far = rb_ref[N_BUCKETS - 1, h]
    bkt = bkt_ref[...]
    acc = jnp.zeros(bkt.shape, F32)
    for b in range(N_BUCKETS - 1):
        acc = jnp.where(bkt == b, rb_ref[b, h] - far, acc)
    o_ref[0] = jnp.where(bkt < 0, NEG_INF, acc)


def _rel_bias_tiles(rel_bias, buckets):
    n_heads = rel_bias.shape[1]
    n, r, c = buckets.shape
    return pl.pallas_call(
        _rel_bias_kernel,
        grid=(n_heads,),
        in_specs=[pl.BlockSpec(memory_space=pltpu.SMEM), pl.BlockSpec((n, r, c), lambda h: (0, 0, 0))],
        out_specs=pl.BlockSpec((1, n, r, c), lambda h: (h, 0, 0, 0)),
        out_shape=jax.ShapeDtypeStruct((n_heads, n, r, c), F32),
        compiler_params=_cparams(("arbitrary",)),
        name="rel_bias",
    )(rel_bias, jnp.asarray(buckets))


def _lambda(lq1, lk1, lq2, lk2, lambda_init):
    a = jnp.sum(lq1[...] * lk1[...], axis=-1, keepdims=True)
    b = jnp.sum(lq2[...] * lk2[...], axis=-1, keepdims=True)
    return jnp.exp(a) - jnp.exp(b) + lambda_init


def _attn_kernel(pt_ref, ua_ref, ub_ref, ui_ref, uj_ref, lq1, lk1, lq2, lk2, g_ref,
                 qs_ref, kn_ref, vn_ref, bs_ref, qp_ref, kp_ref, vp_ref, bp_ref, *rest,
                 pps, n_chunks, n_heads, hd, blk, page, lambda_init):
    k_refs = rest[:pps]
    v_refs = rest[pps:2 * pps]
    os_ref, op_ref = rest[2 * pps:2 * pps + 2]
    m_ref, l_ref, acc_ref, kbf_ref, vbf_ref, qq_ref, mp_ref, lp_ref, accp_ref = rest[2 * pps + 2:]
    j = pl.program_id(1)
    s_id = pl.program_id(0) * n_chunks + j
    last = n_chunks - 1
    dl = qs_ref.shape[0]
    vd = 2 * hd
    hr = 2 * dl
    lam = _lambda(lq1, lk1, lq2, lk2, lambda_init)

    @pl.when(j == 0)
    def _():
        m_ref[...] = jnp.full(m_ref.shape, NEG_INF, F32)
        l_ref[...] = jnp.zeros(l_ref.shape, F32)
        acc_ref[...] = jnp.zeros(acc_ref.shape, F32)

    q8 = qs_ref[...]
    qt = jnp.concatenate([q8] * (2 * n_heads), axis=0)
    lane = lax.broadcasted_iota(I32, qt.shape, 1)
    row = lax.broadcasted_iota(I32, qt.shape, 0)
    qbd = jnp.where(lane // hd == row // dl, qt, 0.0).astype(BF16)

    def update(s, v_of):
        m_old = m_ref[...]
        m_new = jnp.maximum(m_old, jnp.max(s, axis=-1, keepdims=True))
        alpha = jnp.exp(m_old - m_new)
        p = jnp.exp(s - m_new)
        l_ref[...] = alpha * l_ref[...] + jnp.sum(p, axis=-1, keepdims=True)
        pb = p.astype(BF16)
        acc = alpha * acc_ref[...]
        parts = [acc[h * hr:(h + 1) * hr] + jnp.dot(pb[h * hr:(h + 1) * hr], v_of(h).astype(BF16),
                                                  preferred_element_type=F32) for h in range(n_heads)]
        m_ref[...] = m_new
        acc_ref[...] = jnp.concatenate(parts, axis=0)

    for p in range(pps):
        kbf_ref[:, p * page:(p + 1) * page] = k_refs[p][0, 0].astype(BF16)
        for h in range(n_heads):
            vbf_ref[h, p * page:(p + 1) * page, :] = v_refs[p][0, 0, pl.ds(h, page, stride=n_heads), :].astype(BF16)
    s = jnp.dot(qbd, kbf_ref[...], preferred_element_type=F32)
    tail = s[:, (pps - 1) * page:] + jnp.where(j == last, bs_ref[0], 0.0)
    s = jnp.concatenate([s[:, :(pps - 1) * page], tail], axis=1) if pps > 1 else tail
    update(s, lambda h: vbf_ref[h])

    @pl.when(j == last)
    def _():
        pad = jnp.zeros((page - dl, kn_ref.shape[1]), F32)
        kn = jnp.concatenate([kn_ref[...], pad], axis=0).astype(BF16)
        vpad = jnp.zeros((page - dl, vd), F32)
        s_new = lax.dot_general(qbd, kn, (((1,), (1,)), ((), ())), preferred_element_type=F32) + bs_ref[1]
        update(s_new, lambda h: jnp.concatenate([vn_ref[pl.ds(h, dl, stride=n_heads), :], vpad], axis=0))
        o = acc_ref[...] / l_ref[...]
        for h in range(n_heads):
            oh = o[h * hr:h * hr + dl] - lam * o[h * hr + dl:(h + 1) * hr]
            os_ref[:, h * vd:(h + 1) * vd] = _rms(oh, g_ref[...]) * (1.0 - lambda_init)

    qi = ui_ref[s_id]
    kj = uj_ref[s_id]
    ppb = blk // page

    def unit(bias_idx):
        off = pl.multiple_of(kj * blk, blk)
        for h in range(n_heads):
            kt = jnp.concatenate([kp_ref[kj * ppb + p, h * vd:(h + 1) * vd, :] for p in range(ppb)], axis=1)
            vj = vp_ref[pl.ds(off, blk), h * vd:(h + 1) * vd]
            sc = jnp.dot(qq_ref[h], kt, preferred_element_type=F32)
            if bias_idx is not None:
                sc = (sc.reshape(2, blk, blk) + bp_ref[h, bias_idx][None]).reshape(2 * blk, blk)
            m = mp_ref[h]
            m_new = jnp.maximum(m, jnp.max(sc, axis=-1, keepdims=True))
            alpha = jnp.exp(m - m_new)
            p = jnp.exp(sc - jnp.concatenate([m_new] * (blk // LANES), axis=1))
            lp_ref[h] = alpha * lp_ref[h] + jnp.sum(p, axis=-1, keepdims=True)
            accp_ref[h] = alpha * accp_ref[h] + jnp.dot(p.astype(BF16), vj, preferred_element_type=F32)
            mp_ref[h] = m_new

    @pl.when(ua_ref[s_id] == 1)
    def _():
        @pl.when(kj == 0)
        def _():
            lane_q = lax.broadcasted_iota(I32, (blk, vd), 1)
            for h in range(n_heads):
                q = qp_ref[:, h * vd:(h + 1) * vd].astype(BF16)
                zero = jnp.zeros_like(q)
                qq_ref[h, :blk] = jnp.where(lane_q < hd, q, zero)
                qq_ref[h, blk:] = jnp.where(lane_q >= hd, q, zero)
            mp_ref[...] = jnp.full(mp_ref.shape, NEG_INF, F32)
            lp_ref[...] = jnp.zeros(lp_ref.shape, F32)
            accp_ref[...] = jnp.zeros(accp_ref.shape, F32)

        @pl.when(kj < qi - 1)
        def _():
            unit(None)

        @pl.when(kj == qi - 1)
        def _():
            unit(1)

        @pl.when(kj == qi)
        def _():
            unit(0)
            for h in range(n_heads):
                o = accp_ref[h] / lp_ref[h]
                o = o[:blk] - lam * o[blk:]
                op_ref[:, h * vd:(h + 1) * vd] = (_rms(o, g_ref[...]) * (1.0 - lambda_init)).astype(op_ref.dtype)


def _attention(page_table, lams, subln_g, qs, k_new, v_new, bias_s, cache_kt, cache_v, layer,
               qp, ktb, vb, bias_p, n_batch, seq, n_heads, lambda_init):
    db, n_pages = page_table.shape
    ts, w = qs.shape
    tp = qp.shape[0]
    dl = ts // db
    page = cache_kt.shape[3]
    hd = w // (2 * n_heads)
    blk = bias_p.shape[2]
    nq = seq // blk
    rows = 2 * n_heads * dl
    units = [(b, i, j) for b in range(n_batch) for i in range(nq) for j in range(i + 1)]
    pps = math.gcd(PAGES_PER_STEP, n_pages)
    while db * (n_pages // pps) < len(units) and pps % 2 == 0:
        pps //= 2
    n_chunks = n_pages // pps
    n_steps = db * n_chunks
    assert len(units) <= n_steps, "more prompt attention units than sample grid steps"
    act = np.zeros((n_steps,), np.int32)
    ub, ui, uj = (np.zeros((n_steps,), np.int32) for _ in range(3))
    for u, (b, i, j) in enumerate(units):
        st = (u * n_steps) // len(units)
        act[st], ub[st], ui[st], uj[st] = 1, b, i, j
    for st in range(1, n_steps):
        if not act[st]:
            ub[st], ui[st] = ub[st - 1], ui[st - 1]

    fix = lambda b, j, *_: (0, 0)
    seq_s = lambda b, j, *_: (b, 0)
    step = lambda b, j: b * n_chunks + j
    qblk = lambda b, j, pt, ua, ub_, ui_, uj_: (ub_[step(b, j)] * nq + ui_[step(b, j)], 0)

    def page_spec(p, shape):
        return pl.BlockSpec((1, 1) + shape, lambda b, j, pt, *_: (layer, pt[b, j * pps + p], 0, 0))

    grid_spec = pltpu.PrefetchScalarGridSpec(
        num_scalar_prefetch=5,
        grid=(db, n_chunks),
        in_specs=[pl.BlockSpec((1, hd), fix)] * 4 + [
            pl.BlockSpec((1, 2 * hd), fix),
            pl.BlockSpec((dl, w), seq_s), pl.BlockSpec((dl, w), seq_s), pl.BlockSpec((dl * n_heads, 2 * hd), seq_s),
            pl.BlockSpec((2, rows, page), lambda b, j, *_: (0, 0, 0)),
            pl.BlockSpec((blk, w), qblk),
            pl.BlockSpec((seq // page, w, page), lambda b, j, pt, ua, ub_, ui_, uj_: (ub_[step(b, j)], 0, 0)),
            pl.BlockSpec((seq, w), lambda b, j, pt, ua, ub_, ui_, uj_: (ub_[step(b, j)], 0)),
            pl.BlockSpec((n_heads, 2, blk, blk), lambda b, j, *_: (0, 0, 0, 0), pipeline_mode=pl.Buffered(1)),
        ] + [page_spec(p, (w, page)) for p in range(pps)]
        + [page_spec(p, (page * n_heads, 2 * hd)) for p in range(pps)],
        out_specs=[pl.BlockSpec((dl, w), seq_s), pl.BlockSpec((blk, w), qblk)],
        scratch_shapes=[pltpu.VMEM((rows, 1), F32), pltpu.VMEM((rows, 1), F32), pltpu.VMEM((rows, 2 * hd), F32),
                        pltpu.VMEM((w, pps * page), BF16), pltpu.VMEM((n_heads, pps * page, 2 * hd), BF16),
                        pltpu.VMEM((n_heads, 2 * blk, 2 * hd), BF16), pltpu.VMEM((n_heads, 2 * blk, LANES), F32),
                        pltpu.VMEM((n_heads, 2 * blk, LANES), F32), pltpu.VMEM((n_heads, 2 * blk, 2 * hd), F32)],
    )
    o_s, o_p = pl.pallas_call(
        functools.partial(_attn_kernel, pps=pps, n_chunks=n_chunks, n_heads=n_heads, hd=hd, blk=blk, page=page,
                          lambda_init=lambda_init),
        grid_spec=grid_spec,
        out_shape=[jax.ShapeDtypeStruct((ts, w), F32), jax.ShapeDtypeStruct((tp, w), BF16)],
        compiler_params=pltpu.CompilerParams(dimension_semantics=("arbitrary", "arbitrary"),
                                             vmem_limit_bytes=56 * 1024 * 1024),
        name="attn",
    )(page_table, jnp.asarray(act), jnp.asarray(ub), jnp.asarray(ui), jnp.asarray(uj), *lams, subln_g,
      qs, k_new, v_new, bias_s, qp, ktb, vb, bias_p, *([cache_kt] * pps), *([cache_v] * pps))
    return o_p, o_s


def _s5_prep_kernel(are_ref, aim_ref, ldt_ref, bre_ref, bim_ref, lb_ref, bb_ref, pwc_ref, *, seg):
    dt = jnp.exp(ldt_ref[...])
    a_re = jnp.minimum(are_ref[...], -1e-4)
    a_im = aim_ref[...]

    def lam_pow(n):
        mag = jnp.exp(a_re * dt * n)
        ang = a_im * dt * n
        return mag * jnp.cos(ang), mag * jnp.sin(ang)

    lb_re, lb_im = lam_pow(1.0)
    lb_ref[0] = lb_re
    lb_ref[1] = lb_im
    den = a_re * a_re + a_im * a_im
    nr = lb_re - 1.0
    co_re = (nr * a_re + lb_im * a_im) / den
    co_im = (lb_im * a_re - nr * a_im) / den
    bb_ref[0] = co_re * bre_ref[...] - co_im * bim_ref[...]
    bb_ref[1] = co_re * bim_ref[...] + co_im * bre_ref[...]
    n_dbl = pwc_ref.shape[1]
    for s in range(n_dbl):
        c_re, c_im = lam_pow(float(seg * (1 << s)))
        pwc_ref[0, s:s + 1, :] = c_re
        pwc_ref[1, s:s + 1, :] = c_im


def _s5_prep(a_re, a_im, log_dt, b_re, b_im, seg):
    g, p = a_re.shape
    c = b_re.shape[-1]
    gp = g * p
    flat = lambda a: a.reshape(1, gp)
    ldt = jnp.broadcast_to(log_dt[:, None], (g, p)).reshape(1, gp)
    chan = lambda b: jnp.transpose(b, (2, 0, 1)).reshape(c, gp)
    n_dbl = 3
    full = lambda shape: pl.BlockSpec(shape, lambda: (0,) * len(shape))
    return pl.pallas_call(
        functools.partial(_s5_prep_kernel, seg=seg),
        in_specs=[full((1, gp))] * 3 + [full((c, gp))] * 2,
        out_specs=[full((2, 1, gp)), full((2, c, gp)), full((2, n_dbl, gp))],
        out_shape=[jax.ShapeDtypeStruct((2, 1, gp), F32), jax.ShapeDtypeStruct((2, c, gp), F32),
                   jax.ShapeDtypeStruct((2, n_dbl, gp), F32)],
        compiler_params=pltpu.CompilerParams(vmem_limit_bytes=VMEM_LIMIT),
        name="s5_prep",
    )(flat(a_re), flat(a_im), ldt, chan(b_re), chan(b_im))


def _cmul(ar, ai, br, bi):
    return ar * br - ai * bi, ar * bi + ai * br


def _s5_kernel(u_ref, bb_ref, cw_ref, d_ref, lb_ref, pwc_ref, s0_ref, y_ref, st_ref, up_ref, h_ref, hb_ref,
               *, seg, stride, chain):
    cw = lb_ref.shape[2]
    n = SUBLANES
    unroll = math.gcd(seg // 2, 4)
    for i in range(seg):
        up_ref[i * n:(i + 1) * n, :] = u_ref[pl.ds(i, n, stride=stride), :]
    h_ref[...] = jnp.dot(up_ref[...].astype(BF16), bb_ref[0], preferred_element_type=F32)
    lr = jnp.broadcast_to(lb_ref[0], (n, cw))
    li = jnp.broadcast_to(lb_ref[1], (n, cw))

    rows2 = 2 * n

    def step(i, hr, hi):
        r = pl.multiple_of(i * n, n)
        tr, ti = _cmul(lr, li, hr, hi)
        return tr + h_ref[pl.ds(r, n), :cw], ti + h_ref[pl.ds(r, n), cw:]

    def scan(k, carry):
        hr0, hi0 = step(2 * k, *carry)
        hr1, hi1 = step(2 * k + 1, hr0, hi0)
        r = pl.multiple_of(k * rows2, rows2)
        if chain:
            h_ref[pl.ds(r, rows2), :cw] = jnp.concatenate([hr0, hr1], axis=0)
            h_ref[pl.ds(r, rows2), cw:] = jnp.concatenate([hi0, hi1], axis=0)
        else:
            hb_ref[pl.ds(r, rows2), :cw] = jnp.concatenate([hr0, hr1], axis=0).astype(BF16)
            hb_ref[pl.ds(r, rows2), cw:] = jnp.concatenate([hi0, hi1], axis=0).astype(BF16)
        return hr1, hi1

    end_r, end_i = lax.fori_loop(0, seg // 2, scan, (s0_ref[0, 0], s0_ref[0, 1]), unroll=unroll)

    if chain:
        sub = lax.broadcasted_iota(I32, (n, cw), 0)

        def shift(x, d):
            return jnp.where(sub >= d, pltpu.roll(x, d, axis=0), 0.0)

        tr, ti = end_r, end_i
        for s in range(pwc_ref.shape[1]):
            d = 1 << s
            pr = jnp.broadcast_to(pwc_ref[0, s:s + 1, :], (n, cw))
            pi = jnp.broadcast_to(pwc_ref[1, s:s + 1, :], (n, cw))
            ar, ai = _cmul(pr, pi, shift(tr, d), shift(ti, d))
            tr, ti = tr + ar, ti + ai
        end_r, end_i = tr, ti
        cr, ci = shift(tr, 1), shift(ti, 1)

        def fix(k, carry):
            r = pl.multiple_of(k * rows2, rows2)
            ar0, ai0 = _cmul(lr, li, *carry)
            ar1, ai1 = _cmul(lr, li, ar0, ai0)
            hb_ref[pl.ds(r, rows2), :cw] = (h_ref[pl.ds(r, rows2), :cw] + jnp.concatenate([ar0, ar1], axis=0)).astype(BF16)
            hb_ref[pl.ds(r, rows2), cw:] = (h_ref[pl.ds(r, rows2), cw:] + jnp.concatenate([ai0, ai1], axis=0)).astype(BF16)
            return ar1, ai1

        lax.fori_loop(0, seg // 2, fix, (cr, ci), unroll=unroll)

    st_ref[0, 0, 0] = end_r
    st_ref[0, 0, 1] = end_i
    y = jnp.dot(hb_ref[...], cw_ref[0], preferred_element_type=F32) + d_ref[...] * up_ref[...]
    up_ref[...] = jax.nn.gelu(y)
    for i in range(seg):
        y_ref[pl.ds(i, n, stride=stride), :] = up_ref[i * n:(i + 1) * n, :]


def _s5(u, bb_bd, cw_bd, d_vec, lb, pwc, s0, n_seq, seg, stride, chain):
    t, w = u.shape
    ncc, uw, cw2 = bb_bd.shape
    cw = cw2 // 2
    rows = SUBLANES * seg
    return pl.pallas_call(
        functools.partial(_s5_kernel, seg=seg, stride=stride, chain=chain),
        grid=(n_seq, ncc),
        in_specs=[
            pl.BlockSpec((rows, uw), lambda b, c: (b, c)),
            pl.BlockSpec((1, uw, cw2), lambda b, c: (c, 0, 0)),
            pl.BlockSpec((1, cw2, uw), lambda b, c: (c, 0, 0)),
            pl.BlockSpec((1, uw), lambda b, c: (0, c)),
            pl.BlockSpec((2, 1, cw), lambda b, c: (0, 0, c)),
            pl.BlockSpec((2, pwc.shape[1], cw), lambda b, c: (0, 0, c)),
            pl.BlockSpec((1, 2, SUBLANES, cw), lambda b, c: (b, 0, 0, c)),
        ],
        out_specs=[
            pl.BlockSpec((rows, uw), lambda b, c: (b, c)),
            pl.BlockSpec((1, 1, 2, SUBLANES, cw), lambda b, c: (b, c, 0, 0, 0)),
        ],
        out_shape=[jax.ShapeDtypeStruct((t, w), F32),
                   jax.ShapeDtypeStruct((n_seq, ncc, 2, SUBLANES, cw), F32)],
        scratch_shapes=[pltpu.VMEM((rows, uw), F32), pltpu.VMEM((rows, cw2), F32), pltpu.VMEM((rows, cw2), BF16)],
        compiler_params=_cparams(("parallel", "parallel")),
        name="s5_chain" if chain else "s5_step",
    )(u, bb_bd, cw_bd, d_vec, lb, pwc, s0)


def _block_diag_weights(bb, c_re, c_im, cw):
    _, c, gp = bb.shape
    g = c_re.shape[0]
    p = gp // g
    gpc = cw // p
    ncc = g // gpc
    eye = jnp.eye(gpc, dtype=F32)
    b4 = jnp.transpose(bb.reshape(2, c, ncc, gpc, p), (2, 0, 3, 1, 4))
    bbd = b4[:, :, :, :, None, :] * eye[None, None, :, None, :, None]
    bbd = jnp.transpose(bbd.reshape(ncc, 2, gpc * c, cw), (0, 2, 1, 3)).reshape(ncc, gpc * c, 2 * cw)
    cc = jnp.stack([c_re, -c_im]).reshape(2, ncc, gpc, c, p)
    c4 = jnp.transpose(cc, (1, 0, 2, 4, 3))
    cbd = c4[:, :, :, :, None, :] * eye[None, None, :, None, :, None]
    cbd = cbd.reshape(ncc, 2 * cw, gpc * c)
    return bbd.astype(BF16), cbd.astype(BF16)


def _merge_kernel(x_ref, oa_ref, yg_ref, gm_ref, wg_ref, wglu_ref, bglu_ref, wa_ref, ws_ref, wo_ref, gf_ref,
                  wr_ref, br_ref, h_ref, hn_ref, ti_ref, tp_ref, lr_ref, cnt_ref):
    x = x_ref[...]
    tm, d = x.shape
    dot = functools.partial(jnp.dot, preferred_element_type=F32)
    xn = _rms(x, gm_ref[...]).astype(BF16)
    gates = jax.nn.sigmoid(dot(xn, wg_ref[...]))
    yg = yg_ref[...]
    ys = yg * jax.nn.sigmoid(dot(yg.astype(BF16), wglu_ref[...]) + bglu_ref[...])
    merged = (gates[:, :d] * dot(oa_ref[...].astype(BF16), wa_ref[...])
              + gates[:, d:] * dot(ys.astype(BF16), ws_ref[...]))
    h = x + dot(merged.astype(BF16), wo_ref[...])
    h_ref[...] = h
    hn = _rms(h, gf_ref[...]).astype(BF16)
    hn_ref[...] = hn
    logit = lax.dot_general(wr_ref[...], hn, (((1,), (1,)), ((), ())), preferred_element_type=F32) + br_ref[...]
    n_e = logit.shape[0]
    eid = lax.broadcasted_iota(I32, logit.shape, 0)
    vals, idxs = [], []
    for _ in range(TOP_K):
        mx = jnp.max(logit, axis=0, keepdims=True)
        ix = jnp.min(jnp.where(logit == mx, eid, n_e), axis=0, keepdims=True)
        vals.append(mx)
        idxs.append(ix)
        logit = jnp.where(eid == ix, -jnp.inf, logit)
    ex = [jnp.exp(v - vals[0]) for v in vals]
    tot = ex[0]
    for e in ex[1:]:
        tot = tot + e
    ti_ref[...] = jnp.concatenate(idxs, axis=0)
    tp_ref[...] = jnp.concatenate(ex, axis=0) / tot
    onehot = [jnp.where(eid == ix, 1.0, 0.0) for ix in idxs]
    before = (lax.broadcasted_iota(I32, (tm, tm), 0) < lax.broadcasted_iota(I32, (tm, tm), 1))
    prefix = dot(jnp.concatenate(onehot, axis=0).astype(BF16), jnp.where(before, 1.0, 0.0).astype(BF16))
    base = jnp.zeros((n_e, 1), F32)
    ranks = []
    for k in range(TOP_K):
        ranks.append(jnp.sum(onehot[k] * (prefix[k * n_e:(k + 1) * n_e] + base), axis=0, keepdims=True))
        base = base + jnp.sum(onehot[k], axis=1, keepdims=True)
    lr_ref[...] = jnp.concatenate(ranks, axis=0).astype(I32)
    cnt_ref[0] = base.astype(I32)


def _merge(x, oa, yg, w, tm):
    t, d = x.shape
    aw = oa.shape[1]
    sw = yg.shape[1]
    n_e = w["wr"].shape[0]
    row = lambda i: (i, 0)
    col = lambda i: (0, i)
    fix = lambda i: (0, 0)
    full = lambda a: pl.BlockSpec(a.shape, fix)
    names = ["gm", "wg", "wglu", "bglu", "wa", "ws", "wo", "gf", "wr", "br"]
    kt = pl.BlockSpec((TOP_K, tm), col)
    return pl.pallas_call(
        _merge_kernel,
        grid=(t // tm,),
        in_specs=[pl.BlockSpec((tm, d), row), pl.BlockSpec((tm, aw), row), pl.BlockSpec((tm, sw), row)]
        + [full(w[n]) for n in names],
        out_specs=[pl.BlockSpec((tm, d), row), pl.BlockSpec((tm, d), row), kt, kt, kt,
                   pl.BlockSpec((1, n_e, 1), lambda i: (i, 0, 0))],
        out_shape=[jax.ShapeDtypeStruct((t, d), F32), jax.ShapeDtypeStruct((t, d), BF16),
                   jax.ShapeDtypeStruct((TOP_K, t), I32), jax.ShapeDtypeStruct((TOP_K, t), F32),
                   jax.ShapeDtypeStruct((TOP_K, t), I32), jax.ShapeDtypeStruct((t // tm, n_e, 1), I32)],
        compiler_params=_cparams(("parallel",)),
        name="merge",
    )(x, oa, yg, *[w[n] for n in names])


def _chunks_per_tile(tm, n_experts):
    rows = TOP_K * tm + (SUBLANES - 1) * n_experts
    return -(-rows // (SUBLANES * SUBLANES)) * SUBLANES


CHUNK_GROUP = 8


def _route_plan(cnt, moe_tile):
    n_e = cnt.shape[1]
    cnt8 = (cnt + SUBLANES - 1) // SUBLANES * SUBLANES
    total = jnp.sum(cnt8, axis=0)
    padded = (total + moe_tile - 1) // moe_tile * moe_tile
    pad_end = jnp.cumsum(padded)
    tile_base = (pad_end - padded)[None, :] + jnp.cumsum(cnt8, axis=0) - cnt8
    off_end = jnp.cumsum(cnt8, axis=1)
    off8 = off_end - cnt8
    nch = off_end[:, -1] // SUBLANES
    ngrp = (nch + CHUNK_GROUP - 1) // CHUNK_GROUP
    return dict(padded=padded.astype(I32), pad_end=pad_end.astype(I32), tile_base=tile_base, off8=off8,
                off_end=off_end, nch=nch.astype(I32), ngrp=ngrp.astype(I32), n_e=n_e)


def _chunk_rows(plan, lo, hi, g_max, spare_row):
    off_end = plan["off_end"][lo:hi]
    delta = (plan["tile_base"] - plan["off8"])[lo:hi]
    row0 = jnp.arange(g_max, dtype=I32) * SUBLANES
    grp = jnp.sum((off_end[:, None, :] <= row0[None, :, None]).astype(I32), axis=-1)
    grp = jnp.minimum(grp, plan["n_e"] - 1)
    sel = grp[:, :, None] == jnp.arange(plan["n_e"], dtype=I32)[None, None, :]
    dst = row0[None, :] + jnp.sum(jnp.where(sel, delta[:, None, :], 0), axis=-1)
    g = jnp.arange(g_max, dtype=I32)[None, :]
    nch = plan["nch"][lo:hi, None]
    parity = (jnp.arange(lo, hi, dtype=I32) % 2)[:, None]
    spare = spare_row + (parity * CHUNK_GROUP + g - nch) * SUBLANES
    return jnp.where(g < nch, dst, spare).astype(I32)[:, None, :]


def _dispatch_kernel(nch_ref, pe_ref, pd_ref, dst_ref, hnp_ref, hns_ref, ix_ref, lr_ref, pr_ref, off_ref, pos_ref,
                     xs_hbm, xbuf, zbuf, sems, zsem, *, n, n_p, moe_tile):
    i = pl.program_id(0)
    tm = hnp_ref.shape[0]
    rows = xbuf.shape[1]
    n_e = off_ref.shape[1]
    n_moe_tiles = xs_hbm.shape[0] // moe_tile
    slot = lax.rem(i, 2)

    def chunk(s, g, row):
        r = pl.multiple_of(row, SUBLANES)
        src = xbuf.at[s, pl.ds(pl.multiple_of(g * SUBLANES, SUBLANES), SUBLANES), :]
        return pltpu.make_async_copy(src, xs_hbm.at[pl.ds(r, SUBLANES), :], sems.at[s])

    def drain(s, groups):
        grp_rows = CHUNK_GROUP * SUBLANES
        for bit in range((rows // grp_rows).bit_length()):
            span = grp_rows << bit
            if span > rows:
                continue

            @pl.when((groups >> bit) & 1 == 1)
            def _():
                pltpu.make_async_copy(xbuf.at[s, pl.ds(0, span), :], xs_hbm.at[pl.ds(0, span), :], sems.at[s]).wait()

    @pl.when(i == 0)
    def _():
        zbuf[...] = jnp.zeros(zbuf.shape, zbuf.dtype)
        n_used = pe_ref[n_e - 1] // moe_tile

        def zero_tile(row):
            r = pl.multiple_of(row, SUBLANES)
            return pltpu.make_async_copy(zbuf, xs_hbm.at[pl.ds(r, moe_tile), :], zsem.at[0])

        for e in range(n_e):
            @pl.when(pd_ref[e] > 0)
            def _():
                zero_tile(jnp.maximum(pe_ref[e] - moe_tile, 0)).start()

        def start_rest(t, c):
            zero_tile(t * moe_tile).start()
            return c
        lax.fori_loop(n_used, n_moe_tiles, start_rest, 0)
        for e in range(n_e):
            @pl.when(pd_ref[e] > 0)
            def _():
                zero_tile(0).wait()

        def wait_rest(t, c):
            zero_tile(0).wait()
            return c
        lax.fori_loop(n_used, n_moe_tiles, wait_rest, 0)

    @pl.when(i >= 2)
    def _():
        drain(slot, nch_ref[jnp.maximum(i - 2, 0)])

    hn = jnp.where(i < n_p, hnp_ref[...], hns_ref[...])
    eid = lax.broadcasted_iota(I32, (n_e, tm), 0)
    off = off_ref[0].astype(F32)
    r_iota = lax.broadcasted_iota(I32, (rows, tm), 0)
    prob = jnp.zeros((rows, tm), F32)
    pos_rows = []
    for k in range(TOP_K):
        onehot = jnp.where(eid == ix_ref[k:k + 1, :], 1.0, 0.0)
        pos = jnp.sum(onehot * off, axis=0, keepdims=True).astype(I32) + lr_ref[k:k + 1, :]
        pos_rows.append(pos)
        prob = jnp.where(r_iota == pos, pr_ref[k:k + 1, :], prob)
    sel = jnp.where(prob != 0.0, 1.0, 0.0)
    pos_ref[...] = jnp.concatenate(pos_rows, axis=0)
    d = hn.shape[1]
    xbuf[slot, :, :d] = jnp.dot(sel.astype(BF16), hn, preferred_element_type=F32)
    xbuf[slot, :, d:] = jnp.broadcast_to(jnp.sum(prob, axis=1, keepdims=True), (rows, LANES))

    def send(grp, c):
        for u in range(CHUNK_GROUP):
            g = grp * CHUNK_GROUP + u
            chunk(slot, g, dst_ref[0, 0, g]).start()
        return c
    lax.fori_loop(0, nch_ref[i], send, 0)

    @pl.when(i == n - 1)
    def _():
        drain(slot, nch_ref[i])
        if n >= 2:
            drain(1 - slot, nch_ref[jnp.maximum(i - 1, 0)])


def _dispatch(plan, tm, hn_p, hn_s, top_i, lrank, probs, s_pad, moe_tile):
    d = hn_p.shape[1]
    n_p, n_s = hn_p.shape[0] // tm, hn_s.shape[0] // tm
    n = n_p + n_s
    n_e = plan["n_e"]
    g_max = _chunks_per_tile(tm, n_e)
    rows = g_max * SUBLANES
    dst = _chunk_rows(plan, 0, n, g_max, s_pad - moe_tile)
    off = plan["off8"].astype(I32)[:, :, None]
    kt = pl.BlockSpec((TOP_K, tm), lambda i, *_: (0, i))
    grid_spec = pltpu.PrefetchScalarGridSpec(
        num_scalar_prefetch=3,
        grid=(n,),
        in_specs=[
            pl.BlockSpec((1, 1, g_max), lambda i, *_: (i, 0, 0), memory_space=pltpu.SMEM),
            pl.BlockSpec((tm, d), lambda i, *_: (jnp.minimum(i, n_p - 1), 0)),
            pl.BlockSpec((tm, d), lambda i, *_: (jnp.maximum(i - n_p, 0), 0)),
            kt, kt, kt,
            pl.BlockSpec((1, n_e, 1), lambda i, *_: (i, 0, 0)),
        ],
        out_specs=[kt, pl.BlockSpec(memory_space=pl.ANY)],
        scratch_shapes=[pltpu.VMEM((2, rows, d + LANES), F32), pltpu.VMEM((moe_tile, d + LANES), F32),
                        pltpu.SemaphoreType.DMA((2,)), pltpu.SemaphoreType.DMA((1,))])
    return pl.pallas_call(
        functools.partial(_dispatch_kernel, n=n, n_p=n_p, moe_tile=moe_tile),
        grid_spec=grid_spec,
        out_shape=[jax.ShapeDtypeStruct((TOP_K, n * tm), I32), jax.ShapeDtypeStruct((s_pad, d + LANES), F32)],
        compiler_params=_cparams(("arbitrary",)),
        name="dispatch",
    )(plan["ngrp"], plan["pad_end"], plan["padded"], dst, hn_p, hn_s, top_i, lrank, probs, off)


def _moe_kernel(te_ref, nu_ref, x_ref, wg_ref, bg_ref, wu_ref, bu_ref, wd_ref, bd_ref, o_ref, wgb, wub, wdb):
    i = pl.program_id(0)
    used = i < nu_ref[0]
    e = te_ref[i]
    prev = te_ref[jnp.maximum(i - 1, 0)]

    @pl.when(used & ((i == 0) | (e != prev)))
    def _():
        wgb[...] = wg_ref[0].astype(BF16)
        wub[...] = wu_ref[0].astype(BF16)
        wdb[...] = wd_ref[0].astype(BF16)

    @pl.when(used)
    def _():
        d = o_ref.shape[1]
        x = x_ref[:, :d].astype(BF16)
        gt = jnp.dot(x, wgb[...], preferred_element_type=F32) + bg_ref[0]
        up = jnp.dot(x, wub[...], preferred_element_type=F32) + bu_ref[0]
        gt = jnp.minimum(gt, SWIGLU_LIMIT)
        up = jnp.clip(up, -SWIGLU_LIMIT, SWIGLU_LIMIT)
        hh = (up + 1.0) * (gt * jax.nn.sigmoid(SWIGLU_ALPHA * gt))
        o_ref[...] = (jnp.dot(hh.astype(BF16), wdb[...], preferred_element_type=F32) + bd_ref[0]) * x_ref[:, d:d + 1]

    @pl.when(jnp.logical_not(used))
    def _():
        o_ref[...] = jnp.zeros(o_ref.shape, F32)


def _moe(tile_expert, n_used, xs, wg, bg, wu, bu, wd, bd, tm):
    n_e, d, f = wg.shape
    s_pad = xs.shape[0]
    wmap = lambda i, te, nu: (te[i], 0, 0)
    row = lambda i, te, nu: (i, 0)
    grid_spec = pltpu.PrefetchScalarGridSpec(
        num_scalar_prefetch=2,
        grid=(s_pad // tm,),
        in_specs=[
            pl.BlockSpec((tm, d + LANES), row),
            pl.BlockSpec((1, d, f), wmap), pl.BlockSpec((1, 1, f), wmap),
            pl.BlockSpec((1, d, f), wmap), pl.BlockSpec((1, 1, f), wmap),
            pl.BlockSpec((1, f, d), wmap), pl.BlockSpec((1, 1, d), wmap),
        ],
        out_specs=pl.BlockSpec((tm, d), row),
        scratch_shapes=[pltpu.VMEM((d, f), BF16), pltpu.VMEM((d, f), BF16), pltpu.VMEM((f, d), BF16)],
    )
    return pl.pallas_call(
        _moe_kernel,
        grid_spec=grid_spec,
        out_shape=jax.ShapeDtypeStruct((s_pad, d), F32),
        compiler_params=pltpu.CompilerParams(dimension_semantics=("arbitrary",),
                                             vmem_limit_bytes=56 * 1024 * 1024),
        name="moe",
    )(tile_expert, n_used, xs, wg, bg.reshape(n_e, 1, f), wu, bu.reshape(n_e, 1, f), wd, bd.reshape(n_e, 1, d))


def _combine_kernel(nch_ref, dst_ref, nxt_ref, h_ref, pos_ref, g_ref, o_hbm, y_ref, obuf, sems, *, n):
    i = pl.program_id(0)
    tm = h_ref.shape[0]
    rows = obuf.shape[1]
    slot = lax.rem(i, 2)

    def chunk(s, g, row):
        r = pl.multiple_of(row, SUBLANES)
        dst = obuf.at[s, pl.ds(pl.multiple_of(g * SUBLANES, SUBLANES), SUBLANES), :]
        return pltpu.make_async_copy(o_hbm.at[pl.ds(r, SUBLANES), :], dst, sems.at[s])

    def fetch(s, rows_ref, groups):
        def body(grp, c):
            for u in range(CHUNK_GROUP):
                g = grp * CHUNK_GROUP + u
                chunk(s, g, rows_ref[0, 0, g]).start()
            return c
        lax.fori_loop(0, groups, body, 0)

    @pl.when(i == 0)
    def _():
        obuf[...] = jnp.zeros(obuf.shape, obuf.dtype)
        fetch(0, dst_ref, nch_ref[0])

    @pl.when(i + 1 < n)
    def _():
        fetch(1 - slot, nxt_ref, nch_ref[jnp.minimum(i + 1, n - 1)])

    count = nch_ref[i]
    grp_rows = CHUNK_GROUP * SUBLANES
    for bit in range((rows // grp_rows).bit_length()):
        span = grp_rows << bit
        if span > rows:
            continue

        @pl.when((count >> bit) & 1 == 1)
        def _():
            pltpu.make_async_copy(o_hbm.at[pl.ds(0, span), :], obuf.at[slot, pl.ds(0, span), :], sems.at[slot]).wait()

    ob = obuf[slot].astype(BF16)
    lane = lax.broadcasted_iota(I32, (tm, rows), 1)
    w = jnp.zeros((tm, rows), F32)
    for k in range(TOP_K):
        w = w + jnp.where(lane == pos_ref[:, k:k + 1], 1.0, 0.0)
    moe = jnp.dot(w.astype(BF16), ob, preferred_element_type=F32)
    y_ref[...] = _rms(h_ref[...] + moe, g_ref[...])


def _combine(plan, lo, tm, h, pos_t, g, o_sorted, spare_row):
    t, d = h.shape
    n = t // tm
    g_max = _chunks_per_tile(tm, plan["n_e"])
    rows = g_max * SUBLANES
    dst = _chunk_rows(plan, lo, lo + n, g_max, spare_row)
    tok = lambda i, *_: (i, 0)
    grid_spec = pltpu.PrefetchScalarGridSpec(
        num_scalar_prefetch=1,
        grid=(n,),
        in_specs=[
            pl.BlockSpec((1, 1, g_max), lambda i, *_: (i, 0, 0), memory_space=pltpu.SMEM),
            pl.BlockSpec((1, 1, g_max), lambda i, *_: (jnp.minimum(i + 1, n - 1), 0, 0), memory_space=pltpu.SMEM),
            pl.BlockSpec((tm, d), tok), pl.BlockSpec((tm, TOP_K), tok),
            pl.BlockSpec((1, d), lambda i, *_: (0, 0)),
            pl.BlockSpec(memory_space=pl.ANY),
        ],
        out_specs=pl.BlockSpec((tm, d), tok),
        scratch_shapes=[pltpu.VMEM((2, rows, d), F32), pltpu.SemaphoreType.DMA((2,))],
    )
    return pl.pallas_call(
        functools.partial(_combine_kernel, n=n),
        grid_spec=grid_spec,
        out_shape=jax.ShapeDtypeStruct((t, d), F32),
        compiler_params=_cparams(("arbitrary",)),
        name="combine",
    )(plan["ngrp"][lo:lo + n], dst, dst, h, pos_t, g, o_sorted)


def kernel(x_prompt, x_sample, cache_k, cache_v, state_ssm_re, state_ssm_im, page_table, norm_mix_g, w_in, lambda_q1, lambda_k1, lambda_q2, lambda_k2, subln_g, rel_bias, ssm_a_re, ssm_a_im, ssm_log_dt, ssm_b_re, ssm_b_im, ssm_c_re, ssm_c_im, ssm_d, w_glu, b_glu, w_attn_up, w_ssm_up, w_out, norm_ffn_g, w_router, b_router, w_e_gate, b_e_gate, w_e_up, b_e_up, w_e_down, b_e_down, norm_final_g):
    bp, seq, d = x_prompt.shape
    db, dl, _ = x_sample.shape
    depth, n_pool, page, n_heads, _, hd = cache_k.shape
    vd = cache_v.shape[-1]
    n_groups, n_state = ssm_a_re.shape[1:]
    gch = ssm_b_re.shape[-1]
    n_experts = w_router.shape[-1]
    qk_w = n_heads * 2 * hd
    attn_w = n_heads * vd
    ssm_w = n_groups * gch
    mix_w = 2 * qk_w + attn_w + ssm_w
    assert qk_w == attn_w == ssm_w and vd == 2 * hd
    assert seq % SUBLANES == 0 and db % SUBLANES == 0 and dl == SUBLANES
    tp, ts = bp * seq, db * dl
    scale = hd ** -0.5
    seg = seq // SUBLANES
    cw = min(S5_CHUNK, n_groups * n_state)

    ckt = jnp.transpose(cache_k, (0, 1, 3, 4, 5, 2)).reshape(depth, n_pool, qk_w, page)
    cv = cache_v.reshape(depth, n_pool, page * n_heads, vd)

    blk = math.gcd(ATTN_BLOCK, seq)
    qi = np.arange(blk)[:, None]
    ki = np.arange(blk)[None, :]
    bkt_p = np.stack([_bucket_tile(qi - ki), _bucket_tile(blk + qi - ki)])
    sq = np.tile(np.arange(dl), 2)[:, None]
    sk = np.arange(page)[None, :]
    new_dist = np.where(sk < dl, sq - sk, -1)
    bkt_s = np.stack([_bucket_tile(page + sq - sk), _bucket_tile(new_dist)])

    hp = x_prompt.reshape(tp, d)
    hs = x_sample.reshape(ts, d)
    outs = {n: [] for n in ("kp", "vp", "ks", "vs", "rp", "ip", "rs", "is")}
    tm_p = math.gcd(TOKEN_TILE, tp)
    tm_s = math.gcd(TOKEN_TILE, ts)
    for l in range(depth):
        lambda_init = 0.8 - 0.6 * math.exp(-0.3 * l)
        w_l = w_in[l].astype(BF16)
        g_mix = norm_mix_g[l].reshape(1, d)
        lams = [a[l].reshape(1, hd) for a in (lambda_q1, lambda_k1, lambda_q2, lambda_k2)]
        sub_g = subln_g[l].reshape(1, vd)

        w_qvu = jnp.concatenate([w_l[:, :qk_w], w_l[:, 2 * qk_w:mix_w]], axis=1)
        w_kt = jnp.transpose(w_l[:, qk_w:2 * qk_w])
        qp, ktp, vp, up, ktpb, vpb = _in_proj(hp, g_mix, w_qvu, w_kt, scale, tm_p, page, n_heads)
        qs, kts, vs, us, _, _ = _in_proj(hs, g_mix, w_qvu, w_kt, scale, tm_s, None, n_heads)
        ks = jnp.transpose(kts)

        bias_p = _rel_bias_tiles(rel_bias, bkt_p)
        bias_s = _rel_bias_tiles(rel_bias, bkt_s)
        bias_s = jnp.transpose(bias_s, (1, 0, 2, 3)).reshape(2, n_heads * 2 * dl, page)

        oa_p, oa_s = _attention(page_table, lams, sub_g, qs, ks, vs, bias_s, ckt, cv, l,
                                qp, ktpb, vpb, bias_p, bp, seq, n_heads, lambda_init)

        lb, bb, pwc = _s5_prep(ssm_a_re[l], ssm_a_im[l], ssm_log_dt[l], ssm_b_re[l], ssm_b_im[l], seg)
        bb_bd, cw_bd = _block_diag_weights(bb, ssm_c_re[l], ssm_c_im[l], cw)
        d_vec = ssm_d[l].reshape(1, ssm_w)
        zeros = jnp.zeros((bp, 2, SUBLANES, n_groups * n_state), F32)
        yg_p, st_p = _s5(up, bb_bd, cw_bd, d_vec, lb, pwc, zeros, bp, seg, seg, True)
        s0 = jnp.stack([state_ssm_re[l], state_ssm_im[l]], axis=1).reshape(
            db // SUBLANES, SUBLANES, 2, n_groups * n_state)
        s0 = jnp.transpose(s0, (0, 2, 1, 3))
        yg_s, st_s = _s5(us, bb_bd, cw_bd, d_vec, lb, pwc, s0, db // SUBLANES, dl, dl, False)

        wts = dict(
            gm=g_mix, wg=w_l[:, mix_w:], wglu=w_glu[l].astype(BF16), bglu=b_glu[l].reshape(1, ssm_w),
            wa=w_attn_up[l].astype(BF16), ws=w_ssm_up[l].astype(BF16), wo=w_out[l].astype(BF16),
            gf=norm_ffn_g[l].reshape(1, d), wr=jnp.transpose(w_router[l]).astype(BF16),
            br=b_router[l].reshape(n_experts, 1))
        rt = math.gcd(ROUTE_TILE, math.gcd(tp, ts))
        h_p, hn_p, ti_p, pr_p, lr_p, cnt_p = _merge(hp, oa_p, yg_p, wts, rt)
        h_s, hn_s, ti_s, pr_s, lr_s, cnt_s = _merge(hs, oa_s, yg_s, wts, rt)

        n_p, n_s = tp // rt, ts // rt
        plan = _route_plan(jnp.concatenate([cnt_p[:, :, 0], cnt_s[:, :, 0]], axis=0), MOE_TILE)
        worst = TOP_K * (tp + ts) + (SUBLANES - 1) * n_experts * (n_p + n_s) + n_experts * (MOE_TILE - 1)
        s_pad = (-(-worst // MOE_TILE) + 1) * MOE_TILE
        pos, xs = _dispatch(plan, rt, hn_p, hn_s, jnp.concatenate([ti_p, ti_s], axis=1),
                            jnp.concatenate([lr_p, lr_s], axis=1), jnp.concatenate([pr_p, pr_s], axis=1),
                            s_pad, MOE_TILE)
        tile_start = jnp.arange(s_pad // MOE_TILE, dtype=I32) * MOE_TILE
        tile_expert = jnp.minimum(jnp.sum((plan["pad_end"][None, :] <= tile_start[:, None]).astype(I32), axis=1),
                                  n_experts - 1)
        n_used = (plan["pad_end"][-1:] // MOE_TILE).astype(I32)
        o_sorted = _moe(tile_expert, n_used, xs, w_e_gate[l], b_e_gate[l], w_e_up[l], b_e_up[l],
                        w_e_down[l], b_e_down[l], MOE_TILE)

        last = l == depth - 1
        g_out = norm_final_g.reshape(1, d) if last else None
        assert last, "multi-layer stacks need a combine variant without the final norm"
        hp = _combine(plan, 0, rt, h_p, jnp.transpose(pos[:, :tp]), g_out, o_sorted, s_pad - MOE_TILE)
        hs = _combine(plan, n_p, rt, h_s, jnp.transpose(pos[:, tp:]), g_out, o_sorted, s_pad - MOE_TILE)

        gp = n_groups * n_state
        st_p = jnp.transpose(st_p, (0, 2, 3, 1, 4)).reshape(bp, 2, SUBLANES, gp)[:, :, SUBLANES - 1]
        st_s = jnp.transpose(st_s, (0, 3, 2, 1, 4)).reshape(db, 2, gp)
        outs["kp"].append(jnp.transpose(ktp.reshape(bp, seq // page, n_heads, 2, hd, page), (0, 1, 5, 2, 3, 4)))
        outs["vp"].append(vp.reshape(bp, seq // page, page, n_heads, vd))
        outs["ks"].append(ks.reshape(db, dl, n_heads, 2, hd))
        outs["vs"].append(vs.reshape(db, dl, n_heads, vd))
        outs["rp"].append(st_p[:, 0].reshape(bp, n_groups, n_state))
        outs["ip"].append(st_p[:, 1].reshape(bp, n_groups, n_state))
        outs["rs"].append(st_s[:, 0].reshape(db, n_groups, n_state))
        outs["is"].append(st_s[:, 1].reshape(db, n_groups, n_state))

    st = lambda n: jnp.stack(outs[n])
    return (hp.reshape(bp, seq, d), hs.reshape(db, dl, d), st("kp"), st("vp"), st("ks"), st("vs"),
            st("rp"), st("ip"), st("rs"), st("is"))
```

```python
import functools
import math

import numpy as np
import jax
import jax.numpy as jnp
from jax import lax
from jax.experimental import pallas as pl
from jax.experimental.pallas import tpu as pltpu

F32 = jnp.float32
BF16 = jnp.bfloat16
I32 = jnp.int32

EPS = 1e-6
NEG_INF = -1e30
TOP_K = 4
N_BUCKETS = 32
MAX_EXACT = N_BUCKETS // 2
MAX_DISTANCE = 128
SWIGLU_LIMIT = 7.0
SWIGLU_ALPHA = 1.702

SUBLANES = 8
LANES = 128
VMEM_LIMIT = 48 * 1024 * 1024

ATTN_BLOCK = 512
PAGES_PER_STEP = 16
TOKEN_TILE = 512
MOE_TILE = 512
MERGE_TILE = 1024
ROUTE_TILE = 256
S5_CHUNK = 512


def _cparams(sem):
    return pltpu.CompilerParams(dimension_semantics=sem, vmem_limit_bytes=VMEM_LIMIT)


def _rms(x, g):
    return x * lax.rsqrt(jnp.mean(x * x, axis=-1, keepdims=True) + EPS) * g


def _in_proj_kernel(x_ref, g_ref, w_ref, wkt_ref, q_ref, k_ref, v_ref, u_ref, kb_ref, vb_ref, *, scale, page, n_heads):
    xn = _rms(x_ref[...], g_ref[...]).astype(BF16)
    p = jnp.dot(xn, w_ref[...], preferred_element_type=F32)
    w3 = p.shape[1] // 3
    tm = p.shape[0]
    vd = w3 // n_heads
    q_ref[...] = p[:, :w3] * scale
    v = p[:, w3:2 * w3]
    for h in range(n_heads):
        v_ref[pl.ds(h, tm, stride=n_heads), :] = v[:, h * vd:(h + 1) * vd]
    vb_ref[...] = v.astype(BF16)
    u_ref[...] = p[:, 2 * w3:]
    kt = lax.dot_general(wkt_ref[...], xn, (((1,), (1,)), ((), ())), preferred_element_type=F32)
    if page is None:
        k_ref[...] = kt
        kb_ref[...] = kt.astype(BF16)
    else:
        for pg in range(kt.shape[1] // page):
            tile = kt[:, pg * page:(pg + 1) * page]
            k_ref[pg] = tile
            kb_ref[pg] = tile.astype(BF16)


def _in_proj(x, g, w_qvu, w_kt, scale, tm, page, n_heads):
    t, d = x.shape
    w3 = w_qvu.shape[1] // 3
    row = lambda i: (i, 0)
    fix = lambda i: (0, 0)
    f32o = jax.ShapeDtypeStruct((t, w3), F32)
    if page is None:
        kt_shape, kt_spec = (w3, t), pl.BlockSpec((w3, tm), lambda i: (0, i))
    else:
        kt_shape, kt_spec = (t // page, w3, page), pl.BlockSpec((tm // page, w3, page), lambda i: (i, 0, 0))
    tok = pl.BlockSpec((tm, w3), row)
    return pl.pallas_call(
        functools.partial(_in_proj_kernel, scale=scale, page=page, n_heads=n_heads),
        grid=(t // tm,),
        in_specs=[pl.BlockSpec((tm, d), row), pl.BlockSpec((1, d), fix), pl.BlockSpec((d, 3 * w3), fix),
                  pl.BlockSpec((w3, d), fix)],
        out_specs=[tok, kt_spec, pl.BlockSpec((tm * n_heads, w3 // n_heads), row), tok, kt_spec, tok],
        out_shape=[f32o, jax.ShapeDtypeStruct(kt_shape, F32),
                   jax.ShapeDtypeStruct((t * n_heads, w3 // n_heads), F32), f32o,
                   jax.ShapeDtypeStruct(kt_shape, BF16), jax.ShapeDtypeStruct((t, w3), BF16)],
        compiler_params=_cparams(("parallel",)),
        name="in_proj",
    )(x, g, w_qvu, w_kt)


def _bucket_np(n):
    nf = np.maximum(n, 1).astype(np.float64)
    large = MAX_EXACT + np.trunc(
        np.log(nf / MAX_EXACT) / math.log(MAX_DISTANCE / MAX_EXACT) * (N_BUCKETS - MAX_EXACT)).astype(np.int64)
    return np.where(n < MAX_EXACT, n, np.minimum(large, N_BUCKETS - 1)).astype(np.int32)


def _bucket_tile(dist):
    return np.where(dist < 0, -1, _bucket_np(np.maximum(dist, 0))).astype(np.int32)


def _rel_bias_kernel(rb_ref, bkt_ref, o_ref):
    h = pl.program_id(0)
    far = rb_ref[N_BUCKETS - 1, h]
    bkt = bkt_ref[...]
    acc = jnp.zeros(bkt.shape, F32)
    for b in range(N_BUCKETS - 1):
        acc = jnp.where(bkt == b, rb_ref[b, h] - far, acc)
    o_ref[0] = jnp.where(bkt < 0, NEG_INF, acc)


def _rel_bias_tiles(rel_bias, buckets):
    n_heads = rel_bias.shape[1]
    n, r, c = buckets.shape
    return pl.pallas_call(
        _rel_bias_kernel,
        grid=(n_heads,),
        in_specs=[pl.BlockSpec(memory_space=pltpu.SMEM), pl.BlockSpec((n, r, c), lambda h: (0, 0, 0))],
        out_specs=pl.BlockSpec((1, n, r, c), lambda h: (h, 0, 0, 0)),
        out_shape=jax.ShapeDtypeStruct((n_heads, n, r, c), F32),
        compiler_params=_cparams(("arbitrary",)),
        name="rel_bias",
    )(rel_bias, jnp.asarray(buckets))


def _lambda(lq1, lk1, lq2, lk2, lambda_init):
    a = jnp.sum(lq1[...] * lk1[...], axis=-1, keepdims=True)
    b = jnp.sum(lq2[...] * lk2[...], axis=-1, keepdims=True)
    return jnp.exp(a) - jnp.exp(b) + lambda_init


def _attn_p_kernel(lq1, lk1, lq2, lk2, g_ref, q_ref, k_ref, v_ref, b_ref, o_ref, qq_ref, m_ref, l_ref, acc_ref,
                   *, blk, page, n_heads, hd, lambda_init):
    i = pl.program_id(1)
    hw = 2 * hd
    ppb = blk // page
    lam = _lambda(lq1, lk1, lq2, lk2, lambda_init)
    lane = lax.broadcasted_iota(I32, (blk, hw), 1)
    for h in range(n_heads):
        q = q_ref[:, h * hw:(h + 1) * hw].astype(BF16)
        zero = jnp.zeros_like(q)
        qq_ref[h, :blk] = jnp.where(lane < hd, q, zero)
        qq_ref[h, blk:] = jnp.where(lane >= hd, q, zero)
    m_ref[...] = jnp.full(m_ref.shape, NEG_INF, F32)
    l_ref[...] = jnp.zeros(l_ref.shape, F32)
    acc_ref[...] = jnp.zeros(acc_ref.shape, F32)

    def step(j, bias_idx):
        off = pl.multiple_of(j * blk, blk)
        for h in range(n_heads):
            kt = jnp.concatenate([k_ref[j * ppb + p, h * hw:(h + 1) * hw, :] for p in range(ppb)], axis=1)
            vj = v_ref[pl.ds(off, blk), h * hw:(h + 1) * hw]
            s = jnp.dot(qq_ref[h], kt, preferred_element_type=F32)
            if bias_idx is not None:
                s = (s.reshape(2, blk, blk) + b_ref[h, bias_idx][None]).reshape(2 * blk, blk)
            m = m_ref[h]
            m_new = jnp.maximum(m, jnp.max(s, axis=-1, keepdims=True))
            alpha = jnp.exp(m - m_new)
            p = jnp.exp(s - jnp.concatenate([m_new] * (blk // LANES), axis=1))
            l_ref[h] = alpha * l_ref[h] + jnp.sum(p, axis=-1, keepdims=True)
            acc_ref[h] = alpha * acc_ref[h] + jnp.dot(p.astype(BF16), vj, preferred_element_type=F32)
            m_ref[h] = m_new

    def far(j, c):
        step(j, None)
        return c

    lax.fori_loop(0, i - 1, far, 0)

    @pl.when(i >= 1)
    def _():
        step(i - 1, 1)

    step(i, 0)
    for h in range(n_heads):
        o = acc_ref[h] / l_ref[h]
        o = o[:blk] - lam * o[blk:]
        o_ref[:, h * hw:(h + 1) * hw] = (_rms(o, g_ref[...]) * (1.0 - lambda_init)).astype(o_ref.dtype)


def _attn_prompt(lams, subln_g, q, ktb, vb, bias, n_batch, seq, n_heads, lambda_init):
    t, w = q.shape
    page = ktb.shape[2]
    hw = w // n_heads
    blk = math.gcd(ATTN_BLOCK, seq)
    nq = seq // blk
    fix = lambda b, i: (0, 0)
    return pl.pallas_call(
        functools.partial(_attn_p_kernel, blk=blk, page=page, n_heads=n_heads, hd=hw // 2,
                          lambda_init=lambda_init),
        grid=(n_batch, nq),
        in_specs=[pl.BlockSpec((1, hw // 2), fix)] * 4 + [
            pl.BlockSpec((1, hw), fix),
            pl.BlockSpec((blk, w), lambda b, i: (b * nq + i, 0)),
            pl.BlockSpec((seq // page, w, page), lambda b, i: (b, 0, 0)),
            pl.BlockSpec((seq, w), lambda b, i: (b, 0)),
            pl.BlockSpec((n_heads, 2, blk, blk), lambda b, i: (0, 0, 0, 0)),
        ],
        out_specs=pl.BlockSpec((blk, w), lambda b, i: (b * nq + i, 0)),
        out_shape=jax.ShapeDtypeStruct((t, w), BF16),
        scratch_shapes=[pltpu.VMEM((n_heads, 2 * blk, hw), BF16), pltpu.VMEM((n_heads, 2 * blk, LANES), F32),
                        pltpu.VMEM((n_heads, 2 * blk, LANES), F32), pltpu.VMEM((n_heads, 2 * blk, hw), F32)],
        compiler_params=_cparams(("parallel", "parallel")),
        name="attn_p",
    )(*lams, subln_g, q, ktb, vb, bias)


def _attn_s_kernel(pt_ref, lq1, lk1, lq2, lk2, g_ref, q_ref, kn_ref, vn_ref, b_ref, *rest,
                   pps, n_chunks, n_heads, hd, lambda_init):
    k_refs = rest[:pps]
    v_refs = rest[pps:2 * pps]
    o_ref = rest[2 * pps]
    m_ref, l_ref, acc_ref, kbf_ref, vbf_ref = rest[2 * pps + 1:]
    j = pl.program_id(1)
    last = n_chunks - 1
    dl = q_ref.shape[0]
    vd = 2 * hd
    hr = 2 * dl
    page = k_refs[0].shape[3]

    @pl.when(j == 0)
    def _():
        m_ref[...] = jnp.full(m_ref.shape, NEG_INF, F32)
        l_ref[...] = jnp.zeros(l_ref.shape, F32)
        acc_ref[...] = jnp.zeros(acc_ref.shape, F32)

    q8 = q_ref[...]
    qt = jnp.concatenate([q8] * (2 * n_heads), axis=0)
    lane = lax.broadcasted_iota(I32, qt.shape, 1)
    row = lax.broadcasted_iota(I32, qt.shape, 0)
    qbd = jnp.where(lane // hd == row // dl, qt, 0.0).astype(BF16)

    def update(s_list, v_heads):
        m_old = m_ref[...]
        m_new = m_old
        for s in s_list:
            m_new = jnp.maximum(m_new, jnp.max(s, axis=-1, keepdims=True))
        alpha = jnp.exp(m_old - m_new)
        l_new = alpha * l_ref[...]
        acc = alpha * acc_ref[...]
        parts = [acc[h * hr:(h + 1) * hr] for h in range(n_heads)]
        for s, v_of in zip(s_list, v_heads):
            p = jnp.exp(s - m_new)
            l_new = l_new + jnp.sum(p, axis=-1, keepdims=True)
            pb = p.astype(BF16)
            for h in range(n_heads):
                parts[h] = parts[h] + jnp.dot(pb[h * hr:(h + 1) * hr], v_of(h).astype(BF16),
                                              preferred_element_type=F32)
        m_ref[...] = m_new
        l_ref[...] = l_new
        acc_ref[...] = jnp.concatenate(parts, axis=0)

    for p in range(pps):
        kbf_ref[:, p * page:(p + 1) * page] = k_refs[p][0, 0].astype(BF16)
        for h in range(n_heads):
            vbf_ref[h, p * page:(p + 1) * page, :] = v_refs[p][0, 0, pl.ds(h, page, stride=n_heads), :].astype(BF16)
    s = jnp.dot(qbd, kbf_ref[...], preferred_element_type=F32)
    tail = s[:, (pps - 1) * page:] + jnp.where(j == last, b_ref[0], 0.0)
    s = jnp.concatenate([s[:, :(pps - 1) * page], tail], axis=1) if pps > 1 else tail
    update([s], [lambda h: vbf_ref[h]])

    @pl.when(j == last)
    def _():
        pad = jnp.zeros((page - dl, kn_ref.shape[1]), F32)
        kn = jnp.concatenate([kn_ref[...], pad], axis=0).astype(BF16)
        vpad = jnp.zeros((page - dl, vd), F32)
        s_new = lax.dot_general(qbd, kn, (((1,), (1,)), ((), ())), preferred_element_type=F32) + b_ref[1]
        update([s_new], [lambda h: jnp.concatenate([vn_ref[pl.ds(h, dl, stride=n_heads), :], vpad], axis=0)])
        lam = _lambda(lq1, lk1, lq2, lk2, lambda_init)
        o = acc_ref[...] / l_ref[...]
        for h in range(n_heads):
            oh = o[h * hr:h * hr + dl] - lam * o[h * hr + dl:(h + 1) * hr]
            o_ref[:, h * vd:(h + 1) * vd] = _rms(oh, g_ref[...]) * (1.0 - lambda_init)


def _attn_sample(page_table, lams, subln_g, q, k_new, v_new, bias, cache_kt, cache_v, layer, n_heads,
                 lambda_init):
    db, n_pages = page_table.shape
    t, w = q.shape
    dl = t // db
    page = cache_kt.shape[3]
    hd = w // (2 * n_heads)
    pps = math.gcd(PAGES_PER_STEP, n_pages)
    rows = 2 * n_heads * dl
    fix = lambda b, j, pt: (0, 0)
    seq = lambda b, j, pt: (b, 0)

    def page_spec(p, shape):
        return pl.BlockSpec((1, 1) + shape, lambda b, j, pt: (layer, pt[b, j * pps + p], 0, 0))

    grid_spec = pltpu.PrefetchScalarGridSpec(
        num_scalar_prefetch=1,
        grid=(db, n_pages // pps),
        in_specs=[pl.BlockSpec((1, hd), fix)] * 4 + [
            pl.BlockSpec((1, 2 * hd), fix),
            pl.BlockSpec((dl, w), seq), pl.BlockSpec((dl, w), seq), pl.BlockSpec((dl * n_heads, 2 * hd), seq),
            pl.BlockSpec((2, rows, page), lambda b, j, pt: (0, 0, 0)),
        ] + [page_spec(p, (w, page)) for p in range(pps)]
        + [page_spec(p, (page * n_heads, 2 * hd)) for p in range(pps)],
        out_specs=pl.BlockSpec((dl, w), seq),
        scratch_shapes=[pltpu.VMEM((rows, 1), F32), pltpu.VMEM((rows, 1), F32), pltpu.VMEM((rows, 2 * hd), F32),
                        pltpu.VMEM((w, pps * page), BF16), pltpu.VMEM((n_heads, pps * page, 2 * hd), BF16)],
    )
    return pl.pallas_call(
        functools.partial(_attn_s_kernel, pps=pps, n_chunks=n_pages // pps, n_heads=n_heads, hd=hd,
                          lambda_init=lambda_init),
        grid_spec=grid_spec,
        out_shape=jax.ShapeDtypeStruct((t, w), F32),
        compiler_params=_cparams(("parallel", "arbitrary")),
        name="attn_s",
    )(page_table, *lams, subln_g, q, k_new, v_new, bias, *([cache_kt] * pps), *([cache_v] * pps))


def _s5_prep_kernel(are_ref, aim_ref, ldt_ref, bre_ref, bim_ref, lb_ref, bb_ref, pwc_ref, *, seg):
    dt = jnp.exp(ldt_ref[...])
    a_re = jnp.minimum(are_ref[...], -1e-4)
    a_im = aim_ref[...]

    def lam_pow(n):
        mag = jnp.exp(a_re * dt * n)
        ang = a_im * dt * n
        return mag * jnp.cos(ang), mag * jnp.sin(ang)

    lb_re, lb_im = lam_pow(1.0)
    lb_ref[0] = lb_re
    lb_ref[1] = lb_im
    den = a_re * a_re + a_im * a_im
    nr = lb_re - 1.0
    co_re = (nr * a_re + lb_im * a_im) / den
    co_im = (lb_im * a_re - nr * a_im) / den
    bb_ref[0] = co_re * bre_ref[...] - co_im * bim_ref[...]
    bb_ref[1] = co_re * bim_ref[...] + co_im * bre_ref[...]
    n_dbl = pwc_ref.shape[1]
    for s in range(n_dbl):
        c_re, c_im = lam_pow(float(seg * (1 << s)))
        pwc_ref[0, s:s + 1, :] = c_re
        pwc_ref[1, s:s + 1, :] = c_im


def _s5_prep(a_re, a_im, log_dt, b_re, b_im, seg):
    g, p = a_re.shape
    c = b_re.shape[-1]
    gp = g * p
    flat = lambda a: a.reshape(1, gp)
    ldt = jnp.broadcast_to(log_dt[:, None], (g, p)).reshape(1, gp)
    chan = lambda b: jnp.transpose(b, (2, 0, 1)).reshape(c, gp)
    n_dbl = 3
    full = lambda shape: pl.BlockSpec(shape, lambda: (0,) * len(shape))
    return pl.pallas_call(
        functools.partial(_s5_prep_kernel, seg=seg),
        in_specs=[full((1, gp))] * 3 + [full((c, gp))] * 2,
        out_specs=[full((2, 1, gp)), full((2, c, gp)), full((2, n_dbl, gp))],
        out_shape=[jax.ShapeDtypeStruct((2, 1, gp), F32), jax.ShapeDtypeStruct((2, c, gp), F32),
                   jax.ShapeDtypeStruct((2, n_dbl, gp), F32)],
        compiler_params=pltpu.CompilerParams(vmem_limit_bytes=VMEM_LIMIT),
        name="s5_prep",
    )(flat(a_re), flat(a_im), ldt, chan(b_re), chan(b_im))


def _cmul(ar, ai, br, bi):
    return ar * br - ai * bi, ar * bi + ai * br


def _s5_kernel(u_ref, bb_ref, cw_ref, d_ref, lb_ref, pwc_ref, s0_ref, y_ref, st_ref, up_ref, h_ref, hb_ref,
               *, seg, stride, chain):
    cw = lb_ref.shape[2]
    n = SUBLANES
    unroll = math.gcd(seg // 2, 4)
    for i in range(seg):
        up_ref[i * n:(i + 1) * n, :] = u_ref[pl.ds(i, n, stride=stride), :]
    h_ref[...] = jnp.dot(up_ref[...].astype(BF16), bb_ref[0], preferred_element_type=F32)
    lr = jnp.broadcast_to(lb_ref[0], (n, cw))
    li = jnp.broadcast_to(lb_ref[1], (n, cw))

    rows2 = 2 * n

    def step(i, hr, hi):
        r = pl.multiple_of(i * n, n)
        tr, ti = _cmul(lr, li, hr, hi)
        return tr + h_ref[pl.ds(r, n), :cw], ti + h_ref[pl.ds(r, n), cw:]

    def scan(k, carry):
        hr0, hi0 = step(2 * k, *carry)
        hr1, hi1 = step(2 * k + 1, hr0, hi0)
        r = pl.multiple_of(k * rows2, rows2)
        if chain:
            h_ref[pl.ds(r, rows2), :cw] = jnp.concatenate([hr0, hr1], axis=0)
            h_ref[pl.ds(r, rows2), cw:] = jnp.concatenate([hi0, hi1], axis=0)
        else:
            hb_ref[pl.ds(r, rows2), :cw] = jnp.concatenate([hr0, hr1], axis=0).astype(BF16)
            hb_ref[pl.ds(r, rows2), cw:] = jnp.concatenate([hi0, hi1], axis=0).astype(BF16)
        return hr1, hi1

    end_r, end_i = lax.fori_loop(0, seg // 2, scan, (s0_ref[0, 0], s0_ref[0, 1]), unroll=unroll)

    if chain:
        sub = lax.broadcasted_iota(I32, (n, cw), 0)

        def shift(x, d):
            return jnp.where(sub >= d, pltpu.roll(x, d, axis=0), 0.0)

        tr, ti = end_r, end_i
        for s in range(pwc_ref.shape[1]):
            d = 1 << s
            pr = jnp.broadcast_to(pwc_ref[0, s:s + 1, :], (n, cw))
            pi = jnp.broadcast_to(pwc_ref[1, s:s + 1, :], (n, cw))
            ar, ai = _cmul(pr, pi, shift(tr, d), shift(ti, d))
            tr, ti = tr + ar, ti + ai
        end_r, end_i = tr, ti
        cr, ci = shift(tr, 1), shift(ti, 1)

        def fix(k, carry):
            r = pl.multiple_of(k * rows2, rows2)
            ar0, ai0 = _cmul(lr, li, *carry)
            ar1, ai1 = _cmul(lr, li, ar0, ai0)
            hb_ref[pl.ds(r, rows2), :cw] = (h_ref[pl.ds(r, rows2), :cw] + jnp.concatenate([ar0, ar1], axis=0)).astype(BF16)
            hb_ref[pl.ds(r, rows2), cw:] = (h_ref[pl.ds(r, rows2), cw:] + jnp.concatenate([ai0, ai1], axis=0)).astype(BF16)
            return ar1, ai1

        lax.fori_loop(0, seg // 2, fix, (cr, ci), unroll=unroll)

    st_ref[0, 0, 0] = end_r
    st_ref[0, 0, 1] = end_i
    y = jnp.dot(hb_ref[...], cw_ref[0], preferred_element_type=F32) + d_ref[...] * up_ref[...]
    up_ref[...] = jax.nn.gelu(y)
    for i in range(seg):
        y_ref[pl.ds(i, n, stride=stride), :] = up_ref[i * n:(i + 1) * n, :]


def _s5(u, bb_bd, cw_bd, d_vec, lb, pwc, s0, n_seq, seg, stride, chain):
    t, w = u.shape
    ncc, uw, cw2 = bb_bd.shape
    cw = cw2 // 2
    rows = SUBLANES * seg
    return pl.pallas_call(
        functools.partial(_s5_kernel, seg=seg, stride=stride, chain=chain),
        grid=(n_seq, ncc),
        in_specs=[
            pl.BlockSpec((rows, uw), lambda b, c: (b, c)),
            pl.BlockSpec((1, uw, cw2), lambda b, c: (c, 0, 0)),
            pl.BlockSpec((1, cw2, uw), lambda b, c: (c, 0, 0)),
            pl.BlockSpec((1, uw), lambda b, c: (0, c)),
            pl.BlockSpec((2, 1, cw), lambda b, c: (0, 0, c)),
            pl.BlockSpec((2, pwc.shape[1], cw), lambda b, c: (0, 0, c)),
            pl.BlockSpec((1, 2, SUBLANES, cw), lambda b, c: (b, 0, 0, c)),
        ],
        out_specs=[
            pl.BlockSpec((rows, uw), lambda b, c: (b, c)),
            pl.BlockSpec((1, 1, 2, SUBLANES, cw), lambda b, c: (b, c, 0, 0, 0)),
        ],
        out_shape=[jax.ShapeDtypeStruct((t, w), F32),
                   jax.ShapeDtypeStruct((n_seq, ncc, 2, SUBLANES, cw), F32)],
        scratch_shapes=[pltpu.VMEM((rows, uw), F32), pltpu.VMEM((rows, cw2), F32), pltpu.VMEM((rows, cw2), BF16)],
        compiler_params=_cparams(("parallel", "parallel")),
        name="s5_chain" if chain else "s5_step",
    )(u, bb_bd, cw_bd, d_vec, lb, pwc, s0)


def _block_diag_weights(bb, c_re, c_im, cw):
    _, c, gp = bb.shape
    g = c_re.shape[0]
    p = gp // g
    gpc = cw // p
    ncc = g // gpc
    eye = jnp.eye(gpc, dtype=F32)
    b4 = jnp.transpose(bb.reshape(2, c, ncc, gpc, p), (2, 0, 3, 1, 4))
    bbd = b4[:, :, :, :, None, :] * eye[None, None, :, None, :, None]
    bbd = jnp.transpose(bbd.reshape(ncc, 2, gpc * c, cw), (0, 2, 1, 3)).reshape(ncc, gpc * c, 2 * cw)
    cc = jnp.stack([c_re, -c_im]).reshape(2, ncc, gpc, c, p)
    c4 = jnp.transpose(cc, (1, 0, 2, 4, 3))
    cbd = c4[:, :, :, :, None, :] * eye[None, None, :, None, :, None]
    cbd = cbd.reshape(ncc, 2 * cw, gpc * c)
    return bbd.astype(BF16), cbd.astype(BF16)


def _merge_kernel(x_ref, oa_ref, yg_ref, gm_ref, wg_ref, wglu_ref, bglu_ref, wa_ref, ws_ref, wo_ref, gf_ref,
                  wr_ref, br_ref, h_ref, hn_ref, ti_ref, tp_ref, lr_ref, cnt_ref, *, rt):
    x = x_ref[...]
    tm, d = x.shape
    dot = functools.partial(jnp.dot, preferred_element_type=F32)
    xn = _rms(x, gm_ref[...]).astype(BF16)
    gates = jax.nn.sigmoid(dot(xn, wg_ref[...]))
    yg = yg_ref[...]
    ys = yg * jax.nn.sigmoid(dot(yg.astype(BF16), wglu_ref[...]) + bglu_ref[...])
    merged = (gates[:, :d] * dot(oa_ref[...].astype(BF16), wa_ref[...])
              + gates[:, d:] * dot(ys.astype(BF16), ws_ref[...]))
    h = x + dot(merged.astype(BF16), wo_ref[...])
    h_ref[...] = h
    hn = _rms(h, gf_ref[...]).astype(BF16)
    hn_ref[...] = hn
    logit = lax.dot_general(wr_ref[...], hn, (((1,), (1,)), ((), ())), preferred_element_type=F32) + br_ref[...]
    n_e = logit.shape[0]
    eid = lax.broadcasted_iota(I32, logit.shape, 0)
    vals, idxs = [], []
    for _ in range(TOP_K):
        mx = jnp.max(logit, axis=0, keepdims=True)
        ix = jnp.min(jnp.where(logit == mx, eid, n_e), axis=0, keepdims=True)
        vals.append(mx)
        idxs.append(ix)
        logit = jnp.where(eid == ix, -jnp.inf, logit)
    ex = [jnp.exp(v - vals[0]) for v in vals]
    tot = ex[0]
    for e in ex[1:]:
        tot = tot + e
    ti_ref[...] = jnp.concatenate(idxs, axis=0)
    tp_ref[...] = jnp.concatenate(ex, axis=0) / tot
    onehot = [jnp.where(eid == ix, 1.0, 0.0) for ix in idxs]
    before = (lax.broadcasted_iota(I32, (rt, rt), 0) < lax.broadcasted_iota(I32, (rt, rt), 1))
    before = jnp.where(before, 1.0, 0.0).astype(BF16)
    rank_tiles = []
    for sub in range(tm // rt):
        oh = [o[:, sub * rt:(sub + 1) * rt] for o in onehot]
        prefix = dot(jnp.concatenate(oh, axis=0).astype(BF16), before)
        base = jnp.zeros((n_e, 1), F32)
        ranks = []
        for k in range(TOP_K):
            ranks.append(jnp.sum(oh[k] * (prefix[k * n_e:(k + 1) * n_e] + base), axis=0, keepdims=True))
            base = base + jnp.sum(oh[k], axis=1, keepdims=True)
        rank_tiles.append(jnp.concatenate(ranks, axis=0))
        cnt_ref[sub] = base.astype(I32)
    lr_ref[...] = jnp.concatenate(rank_tiles, axis=1).astype(I32)


def _merge(x, oa, yg, w, tm, rt):
    t, d = x.shape
    aw = oa.shape[1]
    sw = yg.shape[1]
    n_e = w["wr"].shape[0]
    row = lambda i: (i, 0)
    col = lambda i: (0, i)
    fix = lambda i: (0, 0)
    full = lambda a: pl.BlockSpec(a.shape, fix)
    names = ["gm", "wg", "wglu", "bglu", "wa", "ws", "wo", "gf", "wr", "br"]
    kt = pl.BlockSpec((TOP_K, tm), col)
    return pl.pallas_call(
        functools.partial(_merge_kernel, rt=rt),
        grid=(t // tm,),
        in_specs=[pl.BlockSpec((tm, d), row), pl.BlockSpec((tm, aw), row), pl.BlockSpec((tm, sw), row)]
        + [full(w[n]) for n in names],
        out_specs=[pl.BlockSpec((tm, d), row), pl.BlockSpec((tm, d), row), kt, kt, kt,
                   pl.BlockSpec((tm // rt, n_e, 1), lambda i: (i, 0, 0))],
        out_shape=[jax.ShapeDtypeStruct((t, d), F32), jax.ShapeDtypeStruct((t, d), BF16),
                   jax.ShapeDtypeStruct((TOP_K, t), I32), jax.ShapeDtypeStruct((TOP_K, t), F32),
                   jax.ShapeDtypeStruct((TOP_K, t), I32), jax.ShapeDtypeStruct((t // rt, n_e, 1), I32)],
        compiler_params=_cparams(("parallel",)),
        name="merge",
    )(x, oa, yg, *[w[n] for n in names])


def _chunks_per_tile(tm, n_experts):
    rows = TOP_K * tm + (SUBLANES - 1) * n_experts
    return -(-rows // (SUBLANES * SUBLANES)) * SUBLANES


CHUNK_GROUP = 8


def _route_plan(cnt, moe_tile):
    n_e = cnt.shape[1]
    cnt8 = (cnt + SUBLANES - 1) // SUBLANES * SUBLANES
    total = jnp.sum(cnt8, axis=0)
    padded = (total + moe_tile - 1) // moe_tile * moe_tile
    pad_end = jnp.cumsum(padded)
    tile_base = (pad_end - padded)[None, :] + jnp.cumsum(cnt8, axis=0) - cnt8
    off_end = jnp.cumsum(cnt8, axis=1)
    off8 = off_end - cnt8
    nch = off_end[:, -1] // SUBLANES
    ngrp = (nch + CHUNK_GROUP - 1) // CHUNK_GROUP
    return dict(padded=padded.astype(I32), pad_end=pad_end.astype(I32), tile_base=tile_base, off8=off8,
                off_end=off_end, nch=nch.astype(I32), ngrp=ngrp.astype(I32), n_e=n_e)


def _chunk_rows(plan, lo, hi, g_max, spare_row):
    off_end = plan["off_end"][lo:hi]
    delta = (plan["tile_base"] - plan["off8"])[lo:hi]
    row0 = jnp.arange(g_max, dtype=I32) * SUBLANES
    grp = jnp.sum((off_end[:, None, :] <= row0[None, :, None]).astype(I32), axis=-1)
    grp = jnp.minimum(grp, plan["n_e"] - 1)
    sel = grp[:, :, None] == jnp.arange(plan["n_e"], dtype=I32)[None, None, :]
    dst = row0[None, :] + jnp.sum(jnp.where(sel, delta[:, None, :], 0), axis=-1)
    g = jnp.arange(g_max, dtype=I32)[None, :]
    nch = plan["nch"][lo:hi, None]
    parity = (jnp.arange(lo, hi, dtype=I32) % 2)[:, None]
    spare = spare_row + (parity * CHUNK_GROUP + g - nch) * SUBLANES
    return jnp.where(g < nch, dst, spare).astype(I32)[:, None, :]


def _dispatch_kernel(nch_ref, pe_ref, pd_ref, dst_ref, hnp_ref, hns_ref, ix_ref, lr_ref, pr_ref, off_ref, pos_ref,
                     xs_hbm, xbuf, zbuf, sems, zsem, *, n, n_p, moe_tile):
    i = pl.program_id(0)
    tm = hnp_ref.shape[0]
    rows = xbuf.shape[1]
    n_e = off_ref.shape[1]
    n_moe_tiles = xs_hbm.shape[0] // moe_tile
    slot = lax.rem(i, 2)

    def chunk(s, g, row):
        r = pl.multiple_of(row, SUBLANES)
        src = xbuf.at[s, pl.ds(pl.multiple_of(g * SUBLANES, SUBLANES), SUBLANES), :]
        return pltpu.make_async_copy(src, xs_hbm.at[pl.ds(r, SUBLANES), :], sems.at[s])

    def drain(s, groups):
        grp_rows = CHUNK_GROUP * SUBLANES
        for bit in range((rows // grp_rows).bit_length()):
            span = grp_rows << bit
            if span > rows:
                continue

            @pl.when((groups >> bit) & 1 == 1)
            def _():
                pltpu.make_async_copy(xbuf.at[s, pl.ds(0, span), :], xs_hbm.at[pl.ds(0, span), :], sems.at[s]).wait()

    @pl.when(i == 0)
    def _():
        zbuf[...] = jnp.zeros(zbuf.shape, zbuf.dtype)
        n_used = pe_ref[n_e - 1] // moe_tile

        def zero_tile(row):
            r = pl.multiple_of(row, SUBLANES)
            return pltpu.make_async_copy(zbuf, xs_hbm.at[pl.ds(r, moe_tile), :], zsem.at[0])

        for e in range(n_e):
            @pl.when(pd_ref[e] > 0)
            def _():
                zero_tile(jnp.maximum(pe_ref[e] - moe_tile, 0)).start()

        def start_rest(t, c):
            zero_tile(t * moe_tile).start()
            return c
        lax.fori_loop(n_used, n_moe_tiles, start_rest, 0)
        for e in range(n_e):
            @pl.when(pd_ref[e] > 0)
            def _():
                zero_tile(0).wait()

        def wait_rest(t, c):
            zero_tile(0).wait()
            return c
        lax.fori_loop(n_used, n_moe_tiles, wait_rest, 0)

    @pl.when(i >= 2)
    def _():
        drain(slot, nch_ref[jnp.maximum(i - 2, 0)])

    hn = jnp.where(i < n_p, hnp_ref[...], hns_ref[...])
    eid = lax.broadcasted_iota(I32, (n_e, tm), 0)
    off = off_ref[0].astype(F32)
    r_iota = lax.broadcasted_iota(I32, (rows, tm), 0)
    prob = jnp.zeros((rows, tm), F32)
    pos_rows = []
    for k in range(TOP_K):
        onehot = jnp.where(eid == ix_ref[k:k + 1, :], 1.0, 0.0)
        pos = jnp.sum(onehot * off, axis=0, keepdims=True).astype(I32) + lr_ref[k:k + 1, :]
        pos_rows.append(pos)
        prob = jnp.where(r_iota == pos, pr_ref[k:k + 1, :], prob)
    sel = jnp.where(prob != 0.0, 1.0, 0.0)
    pos_ref[...] = jnp.concatenate(pos_rows, axis=0)
    d = hn.shape[1]
    xbuf[slot, :, :d] = jnp.dot(sel.astype(BF16), hn, preferred_element_type=F32)
    xbuf[slot, :, d:] = jnp.broadcast_to(jnp.sum(prob, axis=1, keepdims=True), (rows, LANES))

    def send(grp, c):
        for u in range(CHUNK_GROUP):
            g = grp * CHUNK_GROUP + u
            chunk(slot, g, dst_ref[0, 0, g]).start()
        return c
    lax.fori_loop(0, nch_ref[i], send, 0)

    @pl.when(i == n - 1)
    def _():
        drain(slot, nch_ref[i])
        if n >= 2:
            drain(1 - slot, nch_ref[jnp.maximum(i - 1, 0)])


def _dispatch(plan, dst, tm, hn_p, hn_s, top_i, lrank, probs, s_pad, moe_tile):
    d = hn_p.shape[1]
    n_p, n_s = hn_p.shape[0] // tm, hn_s.shape[0] // tm
    n = n_p + n_s
    n_e = plan["n_e"]
    g_max = _chunks_per_tile(tm, n_e)
    rows = g_max * SUBLANES
    off = plan["off8"].astype(I32)[:, :, None]
    kt = pl.BlockSpec((TOP_K, tm), lambda i, *_: (0, i))
    grid_spec = pltpu.PrefetchScalarGridSpec(
        num_scalar_prefetch=3,
        grid=(n,),
        in_specs=[
            pl.BlockSpec((1, 1, g_max), lambda i, *_: (i, 0, 0), memory_space=pltpu.SMEM),
            pl.BlockSpec((tm, d), lambda i, *_: (jnp.minimum(i, n_p - 1), 0)),
            pl.BlockSpec((tm, d), lambda i, *_: (jnp.maximum(i - n_p, 0), 0)),
            kt, kt, kt,
            pl.BlockSpec((1, n_e, 1), lambda i, *_: (i, 0, 0)),
        ],
        out_specs=[kt, pl.BlockSpec(memory_space=pl.ANY)],
        scratch_shapes=[pltpu.VMEM((2, rows, d + LANES), F32), pltpu.VMEM((moe_tile, d + LANES), F32),
                        pltpu.SemaphoreType.DMA((2,)), pltpu.SemaphoreType.DMA((1,))])
    return pl.pallas_call(
        functools.partial(_dispatch_kernel, n=n, n_p=n_p, moe_tile=moe_tile),
        grid_spec=grid_spec,
        out_shape=[jax.ShapeDtypeStruct((TOP_K, n * tm), I32), jax.ShapeDtypeStruct((s_pad, d + LANES), F32)],
        compiler_params=_cparams(("arbitrary",)),
        name="dispatch",
    )(plan["ngrp"], plan["pad_end"], plan["padded"], dst, hn_p, hn_s, top_i, lrank, probs, off)


def _moe_kernel(te_ref, nu_ref, x_ref, wg_ref, bg_ref, wu_ref, bu_ref, wd_ref, bd_ref, o_ref, wgb, wub, wdb):
    i = pl.program_id(0)
    used = i < nu_ref[0]
    e = te_ref[i]
    prev = te_ref[jnp.maximum(i - 1, 0)]

    @pl.when(used & ((i == 0) | (e != prev)))
    def _():
        wgb[...] = wg_ref[0].astype(BF16)
        wub[...] = wu_ref[0].astype(BF16)
        wdb[...] = wd_ref[0].astype(BF16)

    @pl.when(used)
    def _():
        d = o_ref.shape[1]
        x = x_ref[:, :d].astype(BF16)
        gt = jnp.dot(x, wgb[...], preferred_element_type=F32) + bg_ref[0]
        up = jnp.dot(x, wub[...], preferred_element_type=F32) + bu_ref[0]
        gt = jnp.minimum(gt, SWIGLU_LIMIT)
        up = jnp.clip(up, -SWIGLU_LIMIT, SWIGLU_LIMIT)
        hh = (up + 1.0) * (gt * jax.nn.sigmoid(SWIGLU_ALPHA * gt))
        o_ref[...] = (jnp.dot(hh.astype(BF16), wdb[...], preferred_element_type=F32) + bd_ref[0]) * x_ref[:, d:d + 1]

    @pl.when(jnp.logical_not(used))
    def _():
        o_ref[...] = jnp.zeros(o_ref.shape, F32)


def _moe(tile_expert, n_used, xs, wg, bg, wu, bu, wd, bd, tm):
    n_e, d, f = wg.shape
    s_pad = xs.shape[0]
    wmap = lambda i, te, nu: (te[i], 0, 0)
    row = lambda i, te, nu: (i, 0)
    grid_spec = pltpu.PrefetchScalarGridSpec(
        num_scalar_prefetch=2,
        grid=(s_pad // tm,),
        in_specs=[
            pl.BlockSpec((tm, d + LANES), row),
            pl.BlockSpec((1, d, f), wmap), pl.BlockSpec((1, 1, f), wmap),
            pl.BlockSpec((1, d, f), wmap), pl.BlockSpec((1, 1, f), wmap),
            pl.BlockSpec((1, f, d), wmap), pl.BlockSpec((1, 1, d), wmap),
        ],
        out_specs=pl.BlockSpec((tm, d), row),
        scratch_shapes=[pltpu.VMEM((d, f), BF16), pltpu.VMEM((d, f), BF16), pltpu.VMEM((f, d), BF16)],
    )
    return pl.pallas_call(
        _moe_kernel,
        grid_spec=grid_spec,
        out_shape=jax.ShapeDtypeStruct((s_pad, d), F32),
        compiler_params=pltpu.CompilerParams(dimension_semantics=("arbitrary",),
                                             vmem_limit_bytes=56 * 1024 * 1024),
        name="moe",
    )(tile_expert, n_used, xs, wg, bg.reshape(n_e, 1, f), wu, bu.reshape(n_e, 1, f), wd, bd.reshape(n_e, 1, d))


def _combine_kernel(nch_ref, dst_ref, nxt_ref, h_ref, pos_ref, g_ref, o_hbm, y_ref, obuf, sems, *, n):
    i = pl.program_id(0)
    tm = h_ref.shape[0]
    rows = obuf.shape[1]
    slot = lax.rem(i, 2)

    def chunk(s, g, row):
        r = pl.multiple_of(row, SUBLANES)
        dst = obuf.at[s, pl.ds(pl.multiple_of(g * SUBLANES, SUBLANES), SUBLANES), :]
        return pltpu.make_async_copy(o_hbm.at[pl.ds(r, SUBLANES), :], dst, sems.at[s])

    def fetch(s, rows_ref, groups):
        def body(grp, c):
            for u in range(CHUNK_GROUP):
                g = grp * CHUNK_GROUP + u
                chunk(s, g, rows_ref[0, 0, g]).start()
            return c
        lax.fori_loop(0, groups, body, 0)

    @pl.when(i == 0)
    def _():
        obuf[...] = jnp.zeros(obuf.shape, obuf.dtype)
        fetch(0, dst_ref, nch_ref[0])

    @pl.when(i + 1 < n)
    def _():
        fetch(1 - slot, nxt_ref, nch_ref[jnp.minimum(i + 1, n - 1)])

    count = nch_ref[i]
    grp_rows = CHUNK_GROUP * SUBLANES
    for bit in range((rows // grp_rows).bit_length()):
        span = grp_rows << bit
        if span > rows:
            continue

        @pl.when((count >> bit) & 1 == 1)
        def _():
            pltpu.make_async_copy(o_hbm.at[pl.ds(0, span), :], obuf.at[slot, pl.ds(0, span), :], sems.at[slot]).wait()

    ob = obuf[slot].astype(BF16)
    lane = lax.broadcasted_iota(I32, (tm, rows), 1)
    w = jnp.zeros((tm, rows), F32)
    for k in range(TOP_K):
        w = w + jnp.where(lane == pos_ref[:, k:k + 1], 1.0, 0.0)
    moe = jnp.dot(w.astype(BF16), ob, preferred_element_type=F32)
    y_ref[...] = _rms(h_ref[...] + moe, g_ref[...])


def _combine(plan, dst_all, lo, tm, h, pos_t, g, o_sorted):
    t, d = h.shape
    n = t // tm
    g_max = _chunks_per_tile(tm, plan["n_e"])
    rows = g_max * SUBLANES
    dst = dst_all[lo:lo + n]
    tok = lambda i, *_: (i, 0)
    grid_spec = pltpu.PrefetchScalarGridSpec(
        num_scalar_prefetch=1,
        grid=(n,),
        in_specs=[
            pl.BlockSpec((1, 1, g_max), lambda i, *_: (i, 0, 0), memory_space=pltpu.SMEM),
            pl.BlockSpec((1, 1, g_max), lambda i, *_: (jnp.minimum(i + 1, n - 1), 0, 0), memory_space=pltpu.SMEM),
            pl.BlockSpec((tm, d), tok), pl.BlockSpec((tm, TOP_K), tok),
            pl.BlockSpec((1, d), lambda i, *_: (0, 0)),
            pl.BlockSpec(memory_space=pl.ANY),
        ],
        out_specs=pl.BlockSpec((tm, d), tok),
        scratch_shapes=[pltpu.VMEM((2, rows, d), F32), pltpu.SemaphoreType.DMA((2,))],
    )
    return pl.pallas_call(
        functools.partial(_combine_kernel, n=n),
        grid_spec=grid_spec,
        out_shape=jax.ShapeDtypeStruct((t, d), F32),
        compiler_params=_cparams(("arbitrary",)),
        name="combine",
    )(plan["ngrp"][lo:lo + n], dst, dst, h, pos_t, g, o_sorted)


def kernel(x_prompt, x_sample, cache_k, cache_v, state_ssm_re, state_ssm_im, page_table, norm_mix_g, w_in, lambda_q1, lambda_k1, lambda_q2, lambda_k2, subln_g, rel_bias, ssm_a_re, ssm_a_im, ssm_log_dt, ssm_b_re, ssm_b_im, ssm_c_re, ssm_c_im, ssm_d, w_glu, b_glu, w_attn_up, w_ssm_up, w_out, norm_ffn_g, w_router, b_router, w_e_gate, b_e_gate, w_e_up, b_e_up, w_e_down, b_e_down, norm_final_g):
    bp, seq, d = x_prompt.shape
    db, dl, _ = x_sample.shape
    depth, n_pool, page, n_heads, _, hd = cache_k.shape
    vd = cache_v.shape[-1]
    n_groups, n_state = ssm_a_re.shape[1:]
    gch = ssm_b_re.shape[-1]
    n_experts = w_router.shape[-1]
    qk_w = n_heads * 2 * hd
    attn_w = n_heads * vd
    ssm_w = n_groups * gch
    mix_w = 2 * qk_w + attn_w + ssm_w
    assert qk_w == attn_w == ssm_w and vd == 2 * hd
    assert seq % SUBLANES == 0 and db % SUBLANES == 0 and dl == SUBLANES
    tp, ts = bp * seq, db * dl
    scale = hd ** -0.5
    seg = seq // SUBLANES
    cw = min(S5_CHUNK, n_groups * n_state)

    ckt = jnp.transpose(cache_k, (0, 1, 3, 4, 5, 2)).reshape(depth, n_pool, qk_w, page)
    cv = cache_v.reshape(depth, n_pool, page * n_heads, vd)

    blk = math.gcd(ATTN_BLOCK, seq)
    qi = np.arange(blk)[:, None]
    ki = np.arange(blk)[None, :]
    bkt_p = np.stack([_bucket_tile(qi - ki), _bucket_tile(blk + qi - ki)])
    sq = np.tile(np.arange(dl), 2)[:, None]
    sk = np.arange(page)[None, :]
    new_dist = np.where(sk < dl, sq - sk, -1)
    bkt_s = np.stack([_bucket_tile(page + sq - sk), _bucket_tile(new_dist)])

    hp = x_prompt.reshape(tp, d)
    hs = x_sample.reshape(ts, d)
    outs = {n: [] for n in ("kp", "vp", "ks", "vs", "rp", "ip", "rs", "is")}
    tm_p = math.gcd(TOKEN_TILE, tp)
    tm_s = math.gcd(TOKEN_TILE, ts)
    for l in range(depth):
        lambda_init = 0.8 - 0.6 * math.exp(-0.3 * l)
        w_l = w_in[l].astype(BF16)
        g_mix = norm_mix_g[l].reshape(1, d)
        lams = [a[l].reshape(1, hd) for a in (lambda_q1, lambda_k1, lambda_q2, lambda_k2)]
        sub_g = subln_g[l].reshape(1, vd)

        w_qvu = jnp.concatenate([w_l[:, :qk_w], w_l[:, 2 * qk_w:mix_w]], axis=1)
        w_kt = jnp.transpose(w_l[:, qk_w:2 * qk_w])
        qp, ktp, vp, up, ktpb, vpb = _in_proj(hp, g_mix, w_qvu, w_kt, scale, tm_p, page, n_heads)
        qs, kts, vs, us, _, _ = _in_proj(hs, g_mix, w_qvu, w_kt, scale, tm_s, None, n_heads)
        ks = jnp.transpose(kts)

        bias_p = _rel_bias_tiles(rel_bias, bkt_p)
        bias_s = _rel_bias_tiles(rel_bias, bkt_s)
        bias_s = jnp.transpose(bias_s, (1, 0, 2, 3)).reshape(2, n_heads * 2 * dl, page)

        oa_p = _attn_prompt(lams, sub_g, qp, ktpb, vpb, bias_p, bp, seq, n_heads, lambda_init)
        oa_s = _attn_sample(page_table, lams, sub_g, qs, ks, vs, bias_s, ckt, cv, l, n_heads, lambda_init)

        lb, bb, pwc = _s5_prep(ssm_a_re[l], ssm_a_im[l], ssm_log_dt[l], ssm_b_re[l], ssm_b_im[l], seg)
        bb_bd, cw_bd = _block_diag_weights(bb, ssm_c_re[l], ssm_c_im[l], cw)
        d_vec = ssm_d[l].reshape(1, ssm_w)
        zeros = jnp.zeros((bp, 2, SUBLANES, n_groups * n_state), F32)
        yg_p, st_p = _s5(up, bb_bd, cw_bd, d_vec, lb, pwc, zeros, bp, seg, seg, True)
        s0 = jnp.stack([state_ssm_re[l], state_ssm_im[l]], axis=1).reshape(
            db // SUBLANES, SUBLANES, 2, n_groups * n_state)
        s0 = jnp.transpose(s0, (0, 2, 1, 3))
        yg_s, st_s = _s5(us, bb_bd, cw_bd, d_vec, lb, pwc, s0, db // SUBLANES, dl, dl, False)

        wts = dict(
            gm=g_mix, wg=w_l[:, mix_w:], wglu=w_glu[l].astype(BF16), bglu=b_glu[l].reshape(1, ssm_w),
            wa=w_attn_up[l].astype(BF16), ws=w_ssm_up[l].astype(BF16), wo=w_out[l].astype(BF16),
            gf=norm_ffn_g[l].reshape(1, d), wr=jnp.transpose(w_router[l]).astype(BF16),
            br=b_router[l].reshape(n_experts, 1))
        rt = math.gcd(ROUTE_TILE, math.gcd(tp, ts))
        h_p, hn_p, ti_p, pr_p, lr_p, cnt_p = _merge(hp, oa_p, yg_p, wts, math.gcd(MERGE_TILE, tp), rt)
        h_s, hn_s, ti_s, pr_s, lr_s, cnt_s = _merge(hs, oa_s, yg_s, wts, math.gcd(MERGE_TILE, ts), rt)

        n_p, n_s = tp // rt, ts // rt
        plan = _route_plan(jnp.concatenate([cnt_p[:, :, 0], cnt_s[:, :, 0]], axis=0), MOE_TILE)
        worst = TOP_K * (tp + ts) + (SUBLANES - 1) * n_experts * (n_p + n_s) + n_experts * (MOE_TILE - 1)
        s_pad = (-(-worst // MOE_TILE) + 1) * MOE_TILE
        dst = _chunk_rows(plan, 0, n_p + n_s, _chunks_per_tile(rt, n_experts), s_pad - MOE_TILE)
        pos, xs = _dispatch(plan, dst, rt, hn_p, hn_s, jnp.concatenate([ti_p, ti_s], axis=1),
                            jnp.concatenate([lr_p, lr_s], axis=1), jnp.concatenate([pr_p, pr_s], axis=1),
                            s_pad, MOE_TILE)
        tile_start = jnp.arange(s_pad // MOE_TILE, dtype=I32) * MOE_TILE
        tile_expert = jnp.minimum(jnp.sum((plan["pad_end"][None, :] <= tile_start[:, None]).astype(I32), axis=1),
                                  n_experts - 1)
        n_used = (plan["pad_end"][-1:] // MOE_TILE).astype(I32)
        o_sorted = _moe(tile_expert, n_used, xs, w_e_gate[l], b_e_gate[l], w_e_up[l], b_e_up[l],
                        w_e_down[l], b_e_down[l], MOE_TILE)

        last = l == depth - 1
        g_out = norm_final_g.reshape(1, d) if last else None
        assert last, "multi-layer stacks need a combine variant without the final norm"
        hp = _combine(plan, dst, 0, rt, h_p, jnp.transpose(pos[:, :tp]), g_out, o_sorted)
        hs = _combine(plan, dst, n_p, rt, h_s, jnp.transpose(pos[:, tp:]), g_out, o_sorted)

        gp = n_groups * n_state
        st_p = jnp.transpose(st_p, (0, 2, 3, 1, 4)).reshape(bp, 2, SUBLANES, gp)[:, :, SUBLANES - 1]
        st_s = jnp.transpose(st_s, (0, 3, 2, 1, 4)).reshape(db, 2, gp)
        outs["kp"].append(jnp.transpose(ktp.reshape(bp, seq // page, n_heads, 2, hd, page), (0, 1, 5, 2, 3, 4)))
        outs["vp"].append(vp.reshape(bp, seq // page, page, n_heads, vd))
        outs["ks"].append(ks.reshape(db, dl, n_heads, 2, hd))
        outs["vs"].append(vs.reshape(db, dl, n_heads, vd))
        outs["rp"].append(st_p[:, 0].reshape(bp, n_groups, n_state))
        outs["ip"].append(st_p[:, 1].reshape(bp, n_groups, n_state))
        outs["rs"].append(st_s[:, 0].reshape(db, n_groups, n_state))
        outs["is"].append(st_s[:, 1].reshape(db, n_groups, n_state))

    st = lambda n: jnp.stack(outs[n])
    return (hp.reshape(bp, seq, d), hs.reshape(db, dl, d), st("kp"), st("vp"), st("ks"), st("vs"),
            st("rp"), st("ip"), st("rs"), st("is"))
```

```python
import functools
import math

import numpy as np
import jax
import jax.numpy as jnp
from jax import lax
from jax.experimental import pallas as pl
from jax.experimental.pallas import tpu as pltpu

F32 = jnp.float32
BF16 = jnp.bfloat16
I32 = jnp.int32

EPS = 1e-6
NEG_INF = -1e30
TOP_K = 4
N_BUCKETS = 32
MAX_EXACT = N_BUCKETS // 2
MAX_DISTANCE = 128
SWIGLU_LIMIT = 7.0
SWIGLU_ALPHA = 1.702

SUBLANES = 8
LANES = 128
VMEM_LIMIT = 48 * 1024 * 1024

ATTN_BLOCK = 512
PAGES_PER_STEP = 16
TOKEN_TILE = 1024
MOE_TILE = 512
MERGE_TILE = 1024
ROUTE_TILE = 256
S5_CHUNK = 512


def _cparams(sem):
    return pltpu.CompilerParams(dimension_semantics=sem, vmem_limit_bytes=VMEM_LIMIT)


def _rms(x, g):
    return x * lax.rsqrt(jnp.mean(x * x, axis=-1, keepdims=True) + EPS) * g


def _in_proj_kernel(x_ref, g_ref, w_ref, wkt_ref, q_ref, k_ref, v_ref, u_ref, kb_ref, vb_ref, *, scale, page, n_heads):
    xn = _rms(x_ref[...], g_ref[...]).astype(BF16)
    p = jnp.dot(xn, w_ref[...], preferred_element_type=F32)
    w3 = p.shape[1] // 3
    tm = p.shape[0]
    vd = w3 // n_heads
    q_ref[...] = (p[:, :w3] * scale).astype(q_ref.dtype)
    v = p[:, w3:2 * w3]
    for h in range(n_heads):
        v_ref[pl.ds(h, tm, stride=n_heads), :] = v[:, h * vd:(h + 1) * vd]
    vb_ref[...] = v.astype(BF16)
    u_ref[...] = p[:, 2 * w3:]
    kt = lax.dot_general(wkt_ref[...], xn, (((1,), (1,)), ((), ())), preferred_element_type=F32)
    if page is None:
        k_ref[...] = kt
        kb_ref[...] = kt.astype(BF16)
    else:
        for pg in range(kt.shape[1] // page):
            tile = kt[:, pg * page:(pg + 1) * page]
            k_ref[pg] = tile
            kb_ref[pg] = tile.astype(BF16)


def _in_proj(x, g, w_qvu, w_kt, scale, tm, page, n_heads):
    t, d = x.shape
    w3 = w_qvu.shape[1] // 3
    row = lambda i: (i, 0)
    fix = lambda i: (0, 0)
    f32o = jax.ShapeDtypeStruct((t, w3), F32)
    if page is None:
        kt_shape, kt_spec = (w3, t), pl.BlockSpec((w3, tm), lambda i: (0, i))
    else:
        kt_shape, kt_spec = (t // page, w3, page), pl.BlockSpec((tm // page, w3, page), lambda i: (i, 0, 0))
    tok = pl.BlockSpec((tm, w3), row)
    return pl.pallas_call(
        functools.partial(_in_proj_kernel, scale=scale, page=page, n_heads=n_heads),
        grid=(t // tm,),
        in_specs=[pl.BlockSpec((tm, d), row), pl.BlockSpec((1, d), fix), pl.BlockSpec((d, 3 * w3), fix),
                  pl.BlockSpec((w3, d), fix)],
        out_specs=[tok, kt_spec, pl.BlockSpec((tm * n_heads, w3 // n_heads), row), tok, kt_spec, tok],
        out_shape=[jax.ShapeDtypeStruct((t, w3), F32 if page is None else BF16), jax.ShapeDtypeStruct(kt_shape, F32),
                   jax.ShapeDtypeStruct((t * n_heads, w3 // n_heads), F32), f32o,
                   jax.ShapeDtypeStruct(kt_shape, BF16), jax.ShapeDtypeStruct((t, w3), BF16)],
        compiler_params=_cparams(("parallel",)),
        name="in_proj",
    )(x, g, w_qvu, w_kt)


def _bucket_np(n):
    nf = np.maximum(n, 1).astype(np.float64)
    large = MAX_EXACT + np.trunc(
        np.log(nf / MAX_EXACT) / math.log(MAX_DISTANCE / MAX_EXACT) * (N_BUCKETS - MAX_EXACT)).astype(np.int64)
    return np.where(n < MAX_EXACT, n, np.minimum(large, N_BUCKETS - 1)).astype(np.int32)


def _bucket_tile(dist):
    return np.where(dist < 0, -1, _bucket_np(np.maximum(dist, 0))).astype(np.int32)


def _rel_bias_kernel(rb_ref, bkt_ref, o_ref):
    h = pl.program_id(0)
    far = rb_ref[N_BUCKETS - 1, h]
    bkt = bkt_ref[...]
    acc = jnp.zeros(bkt.shape, F32)
    for b in range(N_BUCKETS - 1):
        acc = jnp.where(bkt == b, rb_ref[b, h] - far, acc)
    o_ref[0] = jnp.where(bkt < 0, NEG_INF, acc)


def _rel_bias_tiles(rel_bias, buckets):
    n_heads = rel_bias.shape[1]
    n, r, c = buckets.shape
    return pl.pallas_call(
        _rel_bias_kernel,
        grid=(n_heads,),
        in_specs=[pl.BlockSpec(memory_space=pltpu.SMEM), pl.BlockSpec((n, r, c), lambda h: (0, 0, 0))],
        out_specs=pl.BlockSpec((1, n, r, c), lambda h: (h, 0, 0, 0)),
        out_shape=jax.ShapeDtypeStruct((n_heads, n, r, c), F32),
        compiler_params=_cparams(("arbitrary",)),
        name="rel_bias",
    )(rel_bias, jnp.asarray(buckets))


def _lambda(lq1, lk1, lq2, lk2, lambda_init):
    a = jnp.sum(lq1[...] * lk1[...], axis=-1, keepdims=True)
    b = jnp.sum(lq2[...] * lk2[...], axis=-1, keepdims=True)
    return jnp.exp(a) - jnp.exp(b) + lambda_init


def _attn_p_kernel(lq1, lk1, lq2, lk2, g_ref, q_ref, k_ref, v_ref, b_ref, o_ref, qq_ref, m_ref, l_ref, acc_ref,
                   *, blk, page, n_heads, hd, lambda_init):
    i = pl.program_id(1)
    hw = 2 * hd
    ppb = blk // page
    lam = _lambda(lq1, lk1, lq2, lk2, lambda_init)
    lane = lax.broadcasted_iota(I32, (blk, hw), 1)
    for h in range(n_heads):
        q = q_ref[:, h * hw:(h + 1) * hw].astype(BF16)
        zero = jnp.zeros_like(q)
        qq_ref[h, :blk] = jnp.where(lane < hd, q, zero)
        qq_ref[h, blk:] = jnp.where(lane >= hd, q, zero)
    m_ref[...] = jnp.full(m_ref.shape, NEG_INF, F32)
    l_ref[...] = jnp.zeros(l_ref.shape, F32)
    acc_ref[...] = jnp.zeros(acc_ref.shape, F32)

    def step(j, bias_idx):
        off = pl.multiple_of(j * blk, blk)
        for h in range(n_heads):
            kt = jnp.concatenate([k_ref[j * ppb + p, h * hw:(h + 1) * hw, :] for p in range(ppb)], axis=1)
            vj = v_ref[pl.ds(off, blk), h * hw:(h + 1) * hw]
            s = jnp.dot(qq_ref[h], kt, preferred_element_type=F32)
            if bias_idx is not None:
                s = (s.reshape(2, blk, blk) + b_ref[h, bias_idx][None]).reshape(2 * blk, blk)
            m = m_ref[h]
            m_new = jnp.maximum(m, jnp.max(s, axis=-1, keepdims=True))
            alpha = jnp.exp(m - m_new)
            p = jnp.exp(s - jnp.concatenate([m_new] * (blk // LANES), axis=1))
            l_ref[h] = alpha * l_ref[h] + jnp.sum(p, axis=-1, keepdims=True)
            acc_ref[h] = alpha * acc_ref[h] + jnp.dot(p.astype(BF16), vj, preferred_element_type=F32)
            m_ref[h] = m_new

    def far(j, c):
        step(j, None)
        return c

    lax.fori_loop(0, i - 1, far, 0)

    @pl.when(i >= 1)
    def _():
        step(i - 1, 1)

    step(i, 0)
    for h in range(n_heads):
        o = acc_ref[h] / l_ref[h]
        o = o[:blk] - lam * o[blk:]
        o_ref[:, h * hw:(h + 1) * hw] = (_rms(o, g_ref[...]) * (1.0 - lambda_init)).astype(o_ref.dtype)


def _attn_prompt(lams, subln_g, q, ktb, vb, bias, n_batch, seq, n_heads, lambda_init):
    t, w = q.shape
    page = ktb.shape[2]
    hw = w // n_heads
    blk = math.gcd(ATTN_BLOCK, seq)
    nq = seq // blk
    fix = lambda b, i: (0, 0)
    return pl.pallas_call(
        functools.partial(_attn_p_kernel, blk=blk, page=page, n_heads=n_heads, hd=hw // 2,
                          lambda_init=lambda_init),
        grid=(n_batch, nq),
        in_specs=[pl.BlockSpec((1, hw // 2), fix)] * 4 + [
            pl.BlockSpec((1, hw), fix),
            pl.BlockSpec((blk, w), lambda b, i: (b * nq + i, 0)),
            pl.BlockSpec((seq // page, w, page), lambda b, i: (b, 0, 0)),
            pl.BlockSpec((seq, w), lambda b, i: (b, 0)),
            pl.BlockSpec((n_heads, 2, blk, blk), lambda b, i: (0, 0, 0, 0)),
        ],
        out_specs=pl.BlockSpec((blk, w), lambda b, i: (b * nq + i, 0)),
        out_shape=jax.ShapeDtypeStruct((t, w), BF16),
        scratch_shapes=[pltpu.VMEM((n_heads, 2 * blk, hw), BF16), pltpu.VMEM((n_heads, 2 * blk, LANES), F32),
                        pltpu.VMEM((n_heads, 2 * blk, LANES), F32), pltpu.VMEM((n_heads, 2 * blk, hw), F32)],
        compiler_params=_cparams(("parallel", "parallel")),
        name="attn_p",
    )(*lams, subln_g, q, ktb, vb, bias)


def _attn_s_kernel(pt_ref, lq1, lk1, lq2, lk2, g_ref, q_ref, kn_ref, vn_ref, b_ref, *rest,
                   pps, n_chunks, n_heads, hd, lambda_init):
    k_refs = rest[:pps]
    v_refs = rest[pps:2 * pps]
    o_ref = rest[2 * pps]
    m_ref, l_ref, acc_ref, kbf_ref, vbf_ref = rest[2 * pps + 1:]
    j = pl.program_id(1)
    last = n_chunks - 1
    dl = q_ref.shape[0]
    vd = 2 * hd
    hr = 2 * dl
    page = k_refs[0].shape[3]

    @pl.when(j == 0)
    def _():
        m_ref[...] = jnp.full(m_ref.shape, NEG_INF, F32)
        l_ref[...] = jnp.zeros(l_ref.shape, F32)
        acc_ref[...] = jnp.zeros(acc_ref.shape, F32)

    q8 = q_ref[...]
    qt = jnp.concatenate([q8] * (2 * n_heads), axis=0)
    lane = lax.broadcasted_iota(I32, qt.shape, 1)
    row = lax.broadcasted_iota(I32, qt.shape, 0)
    qbd = jnp.where(lane // hd == row // dl, qt, 0.0).astype(BF16)

    def update(s_list, v_heads):
        m_old = m_ref[...]
        m_new = m_old
        for s in s_list:
            m_new = jnp.maximum(m_new, jnp.max(s, axis=-1, keepdims=True))
        alpha = jnp.exp(m_old - m_new)
        l_new = alpha * l_ref[...]
        acc = alpha * acc_ref[...]
        parts = [acc[h * hr:(h + 1) * hr] for h in range(n_heads)]
        for s, v_of in zip(s_list, v_heads):
            p = jnp.exp(s - m_new)
            l_new = l_new + jnp.sum(p, axis=-1, keepdims=True)
            pb = p.astype(BF16)
            for h in range(n_heads):
                parts[h] = parts[h] + jnp.dot(pb[h * hr:(h + 1) * hr], v_of(h).astype(BF16),
                                              preferred_element_type=F32)
        m_ref[...] = m_new
        l_ref[...] = l_new
        acc_ref[...] = jnp.concatenate(parts, axis=0)

    for p in range(pps):
        kbf_ref[:, p * page:(p + 1) * page] = k_refs[p][0, 0].astype(BF16)
        for h in range(n_heads):
            vbf_ref[h, p * page:(p + 1) * page, :] = v_refs[p][0, 0, pl.ds(h, page, stride=n_heads), :].astype(BF16)
    s = jnp.dot(qbd, kbf_ref[...], preferred_element_type=F32)
    tail = s[:, (pps - 1) * page:] + jnp.where(j == last, b_ref[0], 0.0)
    s = jnp.concatenate([s[:, :(pps - 1) * page], tail], axis=1) if pps > 1 else tail
    update([s], [lambda h: vbf_ref[h]])

    @pl.when(j == last)
    def _():
        pad = jnp.zeros((page - dl, kn_ref.shape[1]), F32)
        kn = jnp.concatenate([kn_ref[...], pad], axis=0).astype(BF16)
        vpad = jnp.zeros((page - dl, vd), F32)
        s_new = lax.dot_general(qbd, kn, (((1,), (1,)), ((), ())), preferred_element_type=F32) + b_ref[1]
        update([s_new], [lambda h: jnp.concatenate([vn_ref[pl.ds(h, dl, stride=n_heads), :], vpad], axis=0)])
        lam = _lambda(lq1, lk1, lq2, lk2, lambda_init)
        o = acc_ref[...] / l_ref[...]
        for h in range(n_heads):
            oh = o[h * hr:h * hr + dl] - lam * o[h * hr + dl:(h + 1) * hr]
            o_ref[:, h * vd:(h + 1) * vd] = _rms(oh, g_ref[...]) * (1.0 - lambda_init)


def _attn_sample(page_table, lams, subln_g, q, k_new, v_new, bias, cache_kt, cache_v, layer, n_heads,
                 lambda_init):
    db, n_pages = page_table.shape
    t, w = q.shape
    dl = t // db
    page = cache_kt.shape[3]
    hd = w // (2 * n_heads)
    pps = math.gcd(PAGES_PER_STEP, n_pages)
    rows = 2 * n_heads * dl
    fix = lambda b, j, pt: (0, 0)
    seq = lambda b, j, pt: (b, 0)

    def page_spec(p, shape):
        return pl.BlockSpec((1, 1) + shape, lambda b, j, pt: (layer, pt[b, j * pps + p], 0, 0))

    grid_spec = pltpu.PrefetchScalarGridSpec(
        num_scalar_prefetch=1,
        grid=(db, n_pages // pps),
        in_specs=[pl.BlockSpec((1, hd), fix)] * 4 + [
            pl.BlockSpec((1, 2 * hd), fix),
            pl.BlockSpec((dl, w), seq), pl.BlockSpec((dl, w), seq), pl.BlockSpec((dl * n_heads, 2 * hd), seq),
            pl.BlockSpec((2, rows, page), lambda b, j, pt: (0, 0, 0)),
        ] + [page_spec(p, (w, page)) for p in range(pps)]
        + [page_spec(p, (page * n_heads, 2 * hd)) for p in range(pps)],
        out_specs=pl.BlockSpec((dl, w), seq),
        scratch_shapes=[pltpu.VMEM((rows, 1), F32), pltpu.VMEM((rows, 1), F32), pltpu.VMEM((rows, 2 * hd), F32),
                        pltpu.VMEM((w, pps * page), BF16), pltpu.VMEM((n_heads, pps * page, 2 * hd), BF16)],
    )
    return pl.pallas_call(
        functools.partial(_attn_s_kernel, pps=pps, n_chunks=n_pages // pps, n_heads=n_heads, hd=hd,
                          lambda_init=lambda_init),
        grid_spec=grid_spec,
        out_shape=jax.ShapeDtypeStruct((t, w), F32),
        compiler_params=_cparams(("parallel", "arbitrary")),
        name="attn_s",
    )(page_table, *lams, subln_g, q, k_new, v_new, bias, *([cache_kt] * pps), *([cache_v] * pps))


def _s5_prep_kernel(are_ref, aim_ref, ldt_ref, bre_ref, bim_ref, lb_ref, bb_ref, pwc_ref, *, seg):
    dt = jnp.exp(ldt_ref[...])
    a_re = jnp.minimum(are_ref[...], -1e-4)
    a_im = aim_ref[...]

    def lam_pow(n):
        mag = jnp.exp(a_re * dt * n)
        ang = a_im * dt * n
        return mag * jnp.cos(ang), mag * jnp.sin(ang)

    lb_re, lb_im = lam_pow(1.0)
    lb_ref[0] = lb_re
    lb_ref[1] = lb_im
    den = a_re * a_re + a_im * a_im
    nr = lb_re - 1.0
    co_re = (nr * a_re + lb_im * a_im) / den
    co_im = (lb_im * a_re - nr * a_im) / den
    bb_ref[0] = co_re * bre_ref[...] - co_im * bim_ref[...]
    bb_ref[1] = co_re * bim_ref[...] + co_im * bre_ref[...]
    n_dbl = pwc_ref.shape[1]
    for s in range(n_dbl):
        c_re, c_im = lam_pow(float(seg * (1 << s)))
        pwc_ref[0, s:s + 1, :] = c_re
        pwc_ref[1, s:s + 1, :] = c_im


def _s5_prep(a_re, a_im, log_dt, b_re, b_im, seg):
    g, p = a_re.shape
    c = b_re.shape[-1]
    gp = g * p
    flat = lambda a: a.reshape(1, gp)
    ldt = jnp.broadcast_to(log_dt[:, None], (g, p)).reshape(1, gp)
    chan = lambda b: jnp.transpose(b, (2, 0, 1)).reshape(c, gp)
    n_dbl = 3
    full = lambda shape: pl.BlockSpec(shape, lambda: (0,) * len(shape))
    return pl.pallas_call(
        functools.partial(_s5_prep_kernel, seg=seg),
        in_specs=[full((1, gp))] * 3 + [full((c, gp))] * 2,
        out_specs=[full((2, 1, gp)), full((2, c, gp)), full((2, n_dbl, gp))],
        out_shape=[jax.ShapeDtypeStruct((2, 1, gp), F32), jax.ShapeDtypeStruct((2, c, gp), F32),
                   jax.ShapeDtypeStruct((2, n_dbl, gp), F32)],
        compiler_params=pltpu.CompilerParams(vmem_limit_bytes=VMEM_LIMIT),
        name="s5_prep",
    )(flat(a_re), flat(a_im), ldt, chan(b_re), chan(b_im))


def _cmul(ar, ai, br, bi):
    return ar * br - ai * bi, ar * bi + ai * br


def _s5_kernel(u_ref, bb_ref, cw_ref, d_ref, lb_ref, pwc_ref, s0_ref, y_ref, st_ref, up_ref, h_ref, hb_ref,
               *, seg, stride, chain):
    cw = lb_ref.shape[2]
    n = SUBLANES
    unroll = math.gcd(seg // 2, 4)
    for i in range(seg):
        up_ref[i * n:(i + 1) * n, :] = u_ref[pl.ds(i, n, stride=stride), :]
    h_ref[...] = jnp.dot(up_ref[...].astype(BF16), bb_ref[0], preferred_element_type=F32)
    lr = jnp.broadcast_to(lb_ref[0], (n, cw))
    li = jnp.broadcast_to(lb_ref[1], (n, cw))

    rows2 = 2 * n

    def step(i, hr, hi):
        r = pl.multiple_of(i * n, n)
        tr, ti = _cmul(lr, li, hr, hi)
        return tr + h_ref[pl.ds(r, n), :cw], ti + h_ref[pl.ds(r, n), cw:]

    def scan(k, carry):
        hr0, hi0 = step(2 * k, *carry)
        hr1, hi1 = step(2 * k + 1, hr0, hi0)
        r = pl.multiple_of(k * rows2, rows2)
        if chain:
            h_ref[pl.ds(r, rows2), :cw] = jnp.concatenate([hr0, hr1], axis=0)
            h_ref[pl.ds(r, rows2), cw:] = jnp.concatenate([hi0, hi1], axis=0)
        else:
            hb_ref[pl.ds(r, rows2), :cw] = jnp.concatenate([hr0, hr1], axis=0).astype(BF16)
            hb_ref[pl.ds(r, rows2), cw:] = jnp.concatenate([hi0, hi1], axis=0).astype(BF16)
        return hr1, hi1

    end_r, end_i = lax.fori_loop(0, seg // 2, scan, (s0_ref[0, 0], s0_ref[0, 1]), unroll=unroll)

    if chain:
        sub = lax.broadcasted_iota(I32, (n, cw), 0)

        def shift(x, d):
            return jnp.where(sub >= d, pltpu.roll(x, d, axis=0), 0.0)

        tr, ti = end_r, end_i
        for s in range(pwc_ref.shape[1]):
            d = 1 << s
            pr = jnp.broadcast_to(pwc_ref[0, s:s + 1, :], (n, cw))
            pi = jnp.broadcast_to(pwc_ref[1, s:s + 1, :], (n, cw))
            ar, ai = _cmul(pr, pi, shift(tr, d), shift(ti, d))
            tr, ti = tr + ar, ti + ai
        end_r, end_i = tr, ti
        cr, ci = shift(tr, 1), shift(ti, 1)

        def fix(k, carry):
            r = pl.multiple_of(k * rows2, rows2)
            ar0, ai0 = _cmul(lr, li, *carry)
            ar1, ai1 = _cmul(lr, li, ar0, ai0)
            hb_ref[pl.ds(r, rows2), :cw] = (h_ref[pl.ds(r, rows2), :cw] + jnp.concatenate([ar0, ar1], axis=0)).astype(BF16)
            hb_ref[pl.ds(r, rows2), cw:] = (h_ref[pl.ds(r, rows2), cw:] + jnp.concatenate([ai0, ai1], axis=0)).astype(BF16)
            return ar1, ai1

        lax.fori_loop(0, seg // 2, fix, (cr, ci), unroll=unroll)

    st_ref[0, 0, 0] = end_r
    st_ref[0, 0, 1] = end_i
    y = jnp.dot(hb_ref[...], cw_ref[0], preferred_element_type=F32) + d_ref[...] * up_ref[...]
    up_ref[...] = jax.nn.gelu(y)
    for i in range(seg):
        y_ref[pl.ds(i, n, stride=stride), :] = up_ref[i * n:(i + 1) * n, :]


def _s5(u, bb_bd, cw_bd, d_vec, lb, pwc, s0, n_seq, seg, stride, chain):
    t, w = u.shape
    ncc, uw, cw2 = bb_bd.shape
    cw = cw2 // 2
    rows = SUBLANES * seg
    return pl.pallas_call(
        functools.partial(_s5_kernel, seg=seg, stride=stride, chain=chain),
        grid=(n_seq, ncc),
        in_specs=[
            pl.BlockSpec((rows, uw), lambda b, c: (b, c)),
            pl.BlockSpec((1, uw, cw2), lambda b, c: (c, 0, 0)),
            pl.BlockSpec((1, cw2, uw), lambda b, c: (c, 0, 0)),
            pl.BlockSpec((1, uw), lambda b, c: (0, c)),
            pl.BlockSpec((2, 1, cw), lambda b, c: (0, 0, c)),
            pl.BlockSpec((2, pwc.shape[1], cw), lambda b, c: (0, 0, c)),
            pl.BlockSpec((1, 2, SUBLANES, cw), lambda b, c: (b, 0, 0, c)),
        ],
        out_specs=[
            pl.BlockSpec((rows, uw), lambda b, c: (b, c)),
            pl.BlockSpec((1, 1, 2, SUBLANES, cw), lambda b, c: (b, c, 0, 0, 0)),
        ],
        out_shape=[jax.ShapeDtypeStruct((t, w), F32),
                   jax.ShapeDtypeStruct((n_seq, ncc, 2, SUBLANES, cw), F32)],
        scratch_shapes=[pltpu.VMEM((rows, uw), F32), pltpu.VMEM((rows, cw2), F32), pltpu.VMEM((rows, cw2), BF16)],
        compiler_params=_cparams(("parallel", "parallel")),
        name="s5_chain" if chain else "s5_step",
    )(u, bb_bd, cw_bd, d_vec, lb, pwc, s0)


def _block_diag_weights(bb, c_re, c_im, cw):
    _, c, gp = bb.shape
    g = c_re.shape[0]
    p = gp // g
    gpc = cw // p
    ncc = g // gpc
    eye = jnp.eye(gpc, dtype=F32)
    b4 = jnp.transpose(bb.reshape(2, c, ncc, gpc, p), (2, 0, 3, 1, 4))
    bbd = b4[:, :, :, :, None, :] * eye[None, None, :, None, :, None]
    bbd = jnp.transpose(bbd.reshape(ncc, 2, gpc * c, cw), (0, 2, 1, 3)).reshape(ncc, gpc * c, 2 * cw)
    cc = jnp.stack([c_re, -c_im]).reshape(2, ncc, gpc, c, p)
    c4 = jnp.transpose(cc, (1, 0, 2, 4, 3))
    cbd = c4[:, :, :, :, None, :] * eye[None, None, :, None, :, None]
    cbd = cbd.reshape(ncc, 2 * cw, gpc * c)
    return bbd.astype(BF16), cbd.astype(BF16)


def _merge_kernel(x_ref, oa_ref, yg_ref, gm_ref, wg_ref, wglu_ref, bglu_ref, wa_ref, ws_ref, wo_ref, gf_ref,
                  wr_ref, br_ref, h_ref, hn_ref, ti_ref, tp_ref, lr_ref, cnt_ref, *, rt):
    x = x_ref[...]
    tm, d = x.shape
    dot = functools.partial(jnp.dot, preferred_element_type=F32)
    xn = _rms(x, gm_ref[...]).astype(BF16)
    gates = jax.nn.sigmoid(dot(xn, wg_ref[...]))
    yg = yg_ref[...]
    ys = yg * jax.nn.sigmoid(dot(yg.astype(BF16), wglu_ref[...]) + bglu_ref[...])
    merged = (gates[:, :d] * dot(oa_ref[...].astype(BF16), wa_ref[...])
              + gates[:, d:] * dot(ys.astype(BF16), ws_ref[...]))
    h = x + dot(merged.astype(BF16), wo_ref[...])
    h_ref[...] = h
    hn = _rms(h, gf_ref[...]).astype(BF16)
    hn_ref[...] = hn
    logit = lax.dot_general(wr_ref[...], hn, (((1,), (1,)), ((), ())), preferred_element_type=F32) + br_ref[...]
    n_e = logit.shape[0]
    eid = lax.broadcasted_iota(I32, logit.shape, 0)
    vals, idxs = [], []
    for _ in range(TOP_K):
        mx = jnp.max(logit, axis=0, keepdims=True)
        ix = jnp.min(jnp.where(logit == mx, eid, n_e), axis=0, keepdims=True)
        vals.append(mx)
        idxs.append(ix)
        logit = jnp.where(eid == ix, -jnp.inf, logit)
    ex = [jnp.exp(v - vals[0]) for v in vals]
    tot = ex[0]
    for e in ex[1:]:
        tot = tot + e
    ti_ref[...] = jnp.concatenate(idxs, axis=0)
    tp_ref[...] = jnp.concatenate(ex, axis=0) / tot
    onehot = [jnp.where(eid == ix, 1.0, 0.0) for ix in idxs]
    before = (lax.broadcasted_iota(I32, (rt, rt), 0) < lax.broadcasted_iota(I32, (rt, rt), 1))
    before = jnp.where(before, 1.0, 0.0).astype(BF16)
    rank_tiles = []
    for sub in range(tm // rt):
        oh = [o[:, sub * rt:(sub + 1) * rt] for o in onehot]
        prefix = dot(jnp.concatenate(oh, axis=0).astype(BF16), before)
        base = jnp.zeros((n_e, 1), F32)
        ranks = []
        for k in range(TOP_K):
            ranks.append(jnp.sum(oh[k] * (prefix[k * n_e:(k + 1) * n_e] + base), axis=0, keepdims=True))
            base = base + jnp.sum(oh[k], axis=1, keepdims=True)
        rank_tiles.append(jnp.concatenate(ranks, axis=0))
        cnt_ref[sub] = base.astype(I32)
    lr_ref[...] = jnp.concatenate(rank_tiles, axis=1).astype(I32)


def _merge(x, oa, yg, w, tm, rt):
    t, d = x.shape
    aw = oa.shape[1]
    sw = yg.shape[1]
    n_e = w["wr"].shape[0]
    row = lambda i: (i, 0)
    col = lambda i: (0, i)
    fix = lambda i: (0, 0)
    full = lambda a: pl.BlockSpec(a.shape, fix)
    names = ["gm", "wg", "wglu", "bglu", "wa", "ws", "wo", "gf", "wr", "br"]
    kt = pl.BlockSpec((TOP_K, tm), col)
    return pl.pallas_call(
        functools.partial(_merge_kernel, rt=rt),
        grid=(t // tm,),
        in_specs=[pl.BlockSpec((tm, d), row), pl.BlockSpec((tm, aw), row), pl.BlockSpec((tm, sw), row)]
        + [full(w[n]) for n in names],
        out_specs=[pl.BlockSpec((tm, d), row), pl.BlockSpec((tm, d), row), kt, kt, kt,
                   pl.BlockSpec((tm // rt, n_e, 1), lambda i: (i, 0, 0))],
        out_shape=[jax.ShapeDtypeStruct((t, d), F32), jax.ShapeDtypeStruct((t, d), BF16),
                   jax.ShapeDtypeStruct((TOP_K, t), I32), jax.ShapeDtypeStruct((TOP_K, t), F32),
                   jax.ShapeDtypeStruct((TOP_K, t), I32), jax.ShapeDtypeStruct((t // rt, n_e, 1), I32)],
        compiler_params=_cparams(("parallel",)),
        name="merge",
    )(x, oa, yg, *[w[n] for n in names])


def _chunks_per_tile(tm, n_experts):
    rows = TOP_K * tm + (SUBLANES - 1) * n_experts
    return -(-rows // (SUBLANES * SUBLANES)) * SUBLANES


CHUNK_GROUP = 8


def _route_plan(cnt, moe_tile):
    n_e = cnt.shape[1]
    cnt8 = (cnt + SUBLANES - 1) // SUBLANES * SUBLANES
    total = jnp.sum(cnt8, axis=0)
    padded = (total + moe_tile - 1) // moe_tile * moe_tile
    pad_end = jnp.cumsum(padded)
    tile_base = (pad_end - padded)[None, :] + jnp.cumsum(cnt8, axis=0) - cnt8
    off_end = jnp.cumsum(cnt8, axis=1)
    off8 = off_end - cnt8
    nch = off_end[:, -1] // SUBLANES
    ngrp = (nch + CHUNK_GROUP - 1) // CHUNK_GROUP
    return dict(padded=padded.astype(I32), pad_end=pad_end.astype(I32), tile_base=tile_base, off8=off8,
                off_end=off_end, nch=nch.astype(I32), ngrp=ngrp.astype(I32), n_e=n_e)


def _chunk_rows(plan, lo, hi, g_max, spare_row):
    off_end = plan["off_end"][lo:hi]
    delta = (plan["tile_base"] - plan["off8"])[lo:hi]
    row0 = jnp.arange(g_max, dtype=I32) * SUBLANES
    grp = jnp.sum((off_end[:, None, :] <= row0[None, :, None]).astype(I32), axis=-1)
    grp = jnp.minimum(grp, plan["n_e"] - 1)
    sel = grp[:, :, None] == jnp.arange(plan["n_e"], dtype=I32)[None, None, :]
    dst = row0[None, :] + jnp.sum(jnp.where(sel, delta[:, None, :], 0), axis=-1)
    g = jnp.arange(g_max, dtype=I32)[None, :]
    nch = plan["nch"][lo:hi, None]
    parity = (jnp.arange(lo, hi, dtype=I32) % 2)[:, None]
    spare = spare_row + (parity * CHUNK_GROUP + g - nch) * SUBLANES
    return jnp.where(g < nch, dst, spare).astype(I32)[:, None, :]


def _dispatch_kernel(nch_ref, pe_ref, pd_ref, dst_ref, hnp_ref, hns_ref, ix_ref, lr_ref, pr_ref, off_ref, pos_ref,
                     xs_hbm, xbuf, zbuf, sems, zsem, *, n, n_p, moe_tile):
    i = pl.program_id(0)
    tm = hnp_ref.shape[0]
    rows = xbuf.shape[1]
    n_e = off_ref.shape[1]
    n_moe_tiles = xs_hbm.shape[0] // moe_tile
    slot = lax.rem(i, 2)

    def chunk(s, g, row):
        r = pl.multiple_of(row, SUBLANES)
        src = xbuf.at[s, pl.ds(pl.multiple_of(g * SUBLANES, SUBLANES), SUBLANES), :]
        return pltpu.make_async_copy(src, xs_hbm.at[pl.ds(r, SUBLANES), :], sems.at[s])

    def drain(s, groups):
        grp_rows = CHUNK_GROUP * SUBLANES
        for bit in range((rows // grp_rows).bit_length()):
            span = grp_rows << bit
            if span > rows:
                continue

            @pl.when((groups >> bit) & 1 == 1)
            def _():
                pltpu.make_async_copy(xbuf.at[s, pl.ds(0, span), :], xs_hbm.at[pl.ds(0, span), :], sems.at[s]).wait()

    @pl.when(i == 0)
    def _():
        zbuf[...] = jnp.zeros(zbuf.shape, zbuf.dtype)
        n_used = pe_ref[n_e - 1] // moe_tile

        def zero_tile(row):
            r = pl.multiple_of(row, SUBLANES)
            return pltpu.make_async_copy(zbuf, xs_hbm.at[pl.ds(r, moe_tile), :], zsem.at[0])

        for e in range(n_e):
            @pl.when(pd_ref[e] > 0)
            def _():
                zero_tile(jnp.maximum(pe_ref[e] - moe_tile, 0)).start()

        def start_rest(t, c):
            zero_tile(t * moe_tile).start()
            return c
        lax.fori_loop(n_used, n_moe_tiles, start_rest, 0)
        for e in range(n_e):
            @pl.when(pd_ref[e] > 0)
            def _():
                zero_tile(0).wait()

        def wait_rest(t, c):
            zero_tile(0).wait()
            return c
        lax.fori_loop(n_used, n_moe_tiles, wait_rest, 0)

    @pl.when(i >= 2)
    def _():
        drain(slot, nch_ref[jnp.maximum(i - 2, 0)])

    hn = jnp.where(i < n_p, hnp_ref[...], hns_ref[...])
    eid = lax.broadcasted_iota(I32, (n_e, tm), 0)
    off = off_ref[0].astype(F32)
    r_iota = lax.broadcasted_iota(I32, (rows, tm), 0)
    prob = jnp.zeros((rows, tm), F32)
    pos_rows = []
    for k in range(TOP_K):
        onehot = jnp.where(eid == ix_ref[k:k + 1, :], 1.0, 0.0)
        pos = jnp.sum(onehot * off, axis=0, keepdims=True).astype(I32) + lr_ref[k:k + 1, :]
        pos_rows.append(pos)
        prob = jnp.where(r_iota == pos, pr_ref[k:k + 1, :], prob)
    sel = jnp.where(prob != 0.0, 1.0, 0.0)
    pos_ref[...] = jnp.concatenate(pos_rows, axis=0)
    d = hn.shape[1]
    xbuf[slot, :, :d] = jnp.dot(sel.astype(BF16), hn, preferred_element_type=F32)
    xbuf[slot, :, d:] = jnp.broadcast_to(jnp.sum(prob, axis=1, keepdims=True), (rows, LANES))

    def send(grp, c):
        for u in range(CHUNK_GROUP):
            g = grp * CHUNK_GROUP + u
            chunk(slot, g, dst_ref[0, 0, g]).start()
        return c
    lax.fori_loop(0, nch_ref[i], send, 0)

    @pl.when(i == n - 1)
    def _():
        drain(slot, nch_ref[i])
        if n >= 2:
            drain(1 - slot, nch_ref[jnp.maximum(i - 1, 0)])


def _dispatch(plan, dst, tm, hn_p, hn_s, top_i, lrank, probs, s_pad, moe_tile):
    d = hn_p.shape[1]
    n_p, n_s = hn_p.shape[0] // tm, hn_s.shape[0] // tm
    n = n_p + n_s
    n_e = plan["n_e"]
    g_max = _chunks_per_tile(tm, n_e)
    rows = g_max * SUBLANES
    off = plan["off8"].astype(I32)[:, :, None]
    kt = pl.BlockSpec((TOP_K, tm), lambda i, *_: (0, i))
    grid_spec = pltpu.PrefetchScalarGridSpec(
        num_scalar_prefetch=3,
        grid=(n,),
        in_specs=[
            pl.BlockSpec((1, 1, g_max), lambda i, *_: (i, 0, 0), memory_space=pltpu.SMEM),
            pl.BlockSpec((tm, d), lambda i, *_: (jnp.minimum(i, n_p - 1), 0)),
            pl.BlockSpec((tm, d), lambda i, *_: (jnp.maximum(i - n_p, 0), 0)),
            kt, kt, kt,
            pl.BlockSpec((1, n_e, 1), lambda i, *_: (i, 0, 0)),
        ],
        out_specs=[kt, pl.BlockSpec(memory_space=pl.ANY)],
        scratch_shapes=[pltpu.VMEM((2, rows, d + LANES), F32), pltpu.VMEM((moe_tile, d + LANES), F32),
                        pltpu.SemaphoreType.DMA((2,)), pltpu.SemaphoreType.DMA((1,))])
    return pl.pallas_call(
        functools.partial(_dispatch_kernel, n=n, n_p=n_p, moe_tile=moe_tile),
        grid_spec=grid_spec,
        out_shape=[jax.ShapeDtypeStruct((TOP_K, n * tm), I32), jax.ShapeDtypeStruct((s_pad, d + LANES), F32)],
        compiler_params=_cparams(("arbitrary",)),
        name="dispatch",
    )(plan["ngrp"], plan["pad_end"], plan["padded"], dst, hn_p, hn_s, top_i, lrank, probs, off)


def _moe_kernel(te_ref, nu_ref, x_ref, wg_ref, bg_ref, wu_ref, bu_ref, wd_ref, bd_ref, o_ref, wgb, wub, wdb):
    i = pl.program_id(0)
    used = i < nu_ref[0]
    e = te_ref[i]
    prev = te_ref[jnp.maximum(i - 1, 0)]

    @pl.when(used & ((i == 0) | (e != prev)))
    def _():
        wgb[...] = wg_ref[0].astype(BF16)
        wub[...] = wu_ref[0].astype(BF16)
        wdb[...] = wd_ref[0].astype(BF16)

    @pl.when(used)
    def _():
        d = o_ref.shape[1]
        x = x_ref[:, :d].astype(BF16)
        gt = jnp.dot(x, wgb[...], preferred_element_type=F32) + bg_ref[0]
        up = jnp.dot(x, wub[...], preferred_element_type=F32) + bu_ref[0]
        gt = jnp.minimum(gt, SWIGLU_LIMIT)
        up = jnp.clip(up, -SWIGLU_LIMIT, SWIGLU_LIMIT)
        hh = (up + 1.0) * (gt * jax.nn.sigmoid(SWIGLU_ALPHA * gt))
        o_ref[...] = (jnp.dot(hh.astype(BF16), wdb[...], preferred_element_type=F32) + bd_ref[0]) * x_ref[:, d:d + 1]

    @pl.when(jnp.logical_not(used))
    def _():
        o_ref[...] = jnp.zeros(o_ref.shape, F32)


def _moe(tile_expert, n_used, xs, wg, bg, wu, bu, wd, bd, tm):
    n_e, d, f = wg.shape
    s_pad = xs.shape[0]
    wmap = lambda i, te, nu: (te[i], 0, 0)
    row = lambda i, te, nu: (i, 0)
    grid_spec = pltpu.PrefetchScalarGridSpec(
        num_scalar_prefetch=2,
        grid=(s_pad // tm,),
        in_specs=[
            pl.BlockSpec((tm, d + LANES), lambda i, te, nu: (jnp.minimum(i, jnp.maximum(nu[0] - 1, 0)), 0)),
            pl.BlockSpec((1, d, f), wmap), pl.BlockSpec((1, 1, f), wmap),
            pl.BlockSpec((1, d, f), wmap), pl.BlockSpec((1, 1, f), wmap),
            pl.BlockSpec((1, f, d), wmap), pl.BlockSpec((1, 1, d), wmap),
        ],
        out_specs=pl.BlockSpec((tm, d), row),
        scratch_shapes=[pltpu.VMEM((d, f), BF16), pltpu.VMEM((d, f), BF16), pltpu.VMEM((f, d), BF16)],
    )
    return pl.pallas_call(
        _moe_kernel,
        grid_spec=grid_spec,
        out_shape=jax.ShapeDtypeStruct((s_pad, d), F32),
        compiler_params=pltpu.CompilerParams(dimension_semantics=("arbitrary",),
                                             vmem_limit_bytes=56 * 1024 * 1024),
        name="moe",
    )(tile_expert, n_used, xs, wg, bg.reshape(n_e, 1, f), wu, bu.reshape(n_e, 1, f), wd, bd.reshape(n_e, 1, d))


def _combine_kernel(nch_ref, dst_ref, nxt_ref, h_ref, pos_ref, g_ref, o_hbm, y_ref, obuf, sems, *, n):
    i = pl.program_id(0)
    tm = h_ref.shape[0]
    rows = obuf.shape[1]
    slot = lax.rem(i, 2)

    def chunk(s, g, row):
        r = pl.multiple_of(row, SUBLANES)
        dst = obuf.at[s, pl.ds(pl.multiple_of(g * SUBLANES, SUBLANES), SUBLANES), :]
        return pltpu.make_async_copy(o_hbm.at[pl.ds(r, SUBLANES), :], dst, sems.at[s])

    def fetch(s, rows_ref, groups):
        def body(grp, c):
            for u in range(CHUNK_GROUP):
                g = grp * CHUNK_GROUP + u
                chunk(s, g, rows_ref[0, 0, g]).start()
            return c
        lax.fori_loop(0, groups, body, 0)

    @pl.when(i == 0)
    def _():
        obuf[...] = jnp.zeros(obuf.shape, obuf.dtype)
        fetch(0, dst_ref, nch_ref[0])

    @pl.when(i + 1 < n)
    def _():
        fetch(1 - slot, nxt_ref, nch_ref[jnp.minimum(i + 1, n - 1)])

    count = nch_ref[i]
    grp_rows = CHUNK_GROUP * SUBLANES
    for bit in range((rows // grp_rows).bit_length()):
        span = grp_rows << bit
        if span > rows:
            continue

        @pl.when((count >> bit) & 1 == 1)
        def _():
            pltpu.make_async_copy(o_hbm.at[pl.ds(0, span), :], obuf.at[slot, pl.ds(0, span), :], sems.at[slot]).wait()

    ob = obuf[slot].astype(BF16)
    lane = lax.broadcasted_iota(I32, (tm, rows), 1)
    w = jnp.zeros((tm, rows), F32)
    for k in range(TOP_K):
        w = w + jnp.where(lane == pos_ref[:, k:k + 1], 1.0, 0.0)
    moe = jnp.dot(w.astype(BF16), ob, preferred_element_type=F32)
    y_ref[...] = _rms(h_ref[...] + moe, g_ref[...])


def _combine(plan, dst_all, lo, tm, h, pos_t, g, o_sorted):
    t, d = h.shape
    n = t // tm
    g_max = _chunks_per_tile(tm, plan["n_e"])
    rows = g_max * SUBLANES
    dst = dst_all[lo:lo + n]
    tok = lambda i, *_: (i, 0)
    grid_spec = pltpu.PrefetchScalarGridSpec(
        num_scalar_prefetch=1,
        grid=(n,),
        in_specs=[
            pl.BlockSpec((1, 1, g_max), lambda i, *_: (i, 0, 0), memory_space=pltpu.SMEM),
            pl.BlockSpec((1, 1, g_max), lambda i, *_: (jnp.minimum(i + 1, n - 1), 0, 0), memory_space=pltpu.SMEM),
            pl.BlockSpec((tm, d), tok), pl.BlockSpec((tm, TOP_K), tok),
            pl.BlockSpec((1, d), lambda i, *_: (0, 0)),
            pl.BlockSpec(memory_space=pl.ANY),
        ],
        out_specs=pl.BlockSpec((tm, d), tok),
        scratch_shapes=[pltpu.VMEM((2, rows, d), F32), pltpu.SemaphoreType.DMA((2,))],
    )
    return pl.pallas_call(
        functools.partial(_combine_kernel, n=n),
        grid_spec=grid_spec,
        out_shape=jax.ShapeDtypeStruct((t, d), F32),
        compiler_params=_cparams(("arbitrary",)),
        name="combine",
    )(plan["ngrp"][lo:lo + n], dst, dst, h, pos_t, g, o_sorted)


def kernel(x_prompt, x_sample, cache_k, cache_v, state_ssm_re, state_ssm_im, page_table, norm_mix_g, w_in, lambda_q1, lambda_k1, lambda_q2, lambda_k2, subln_g, rel_bias, ssm_a_re, ssm_a_im, ssm_log_dt, ssm_b_re, ssm_b_im, ssm_c_re, ssm_c_im, ssm_d, w_glu, b_glu, w_attn_up, w_ssm_up, w_out, norm_ffn_g, w_router, b_router, w_e_gate, b_e_gate, w_e_up, b_e_up, w_e_down, b_e_down, norm_final_g):
    bp, seq, d = x_prompt.shape
    db, dl, _ = x_sample.shape
    depth, n_pool, page, n_heads, _, hd = cache_k.shape
    vd = cache_v.shape[-1]
    n_groups, n_state = ssm_a_re.shape[1:]
    gch = ssm_b_re.shape[-1]
    n_experts = w_router.shape[-1]
    qk_w = n_heads * 2 * hd
    attn_w = n_heads * vd
    ssm_w = n_groups * gch
    mix_w = 2 * qk_w + attn_w + ssm_w
    assert qk_w == attn_w == ssm_w and vd == 2 * hd
    assert seq % SUBLANES == 0 and db % SUBLANES == 0 and dl == SUBLANES
    tp, ts = bp * seq, db * dl
    scale = hd ** -0.5
    seg = seq // SUBLANES
    cw = min(S5_CHUNK, n_groups * n_state)

    ckt = jnp.transpose(cache_k, (0, 1, 3, 4, 5, 2)).reshape(depth, n_pool, qk_w, page)
    cv = cache_v.reshape(depth, n_pool, page * n_heads, vd)

    blk = math.gcd(ATTN_BLOCK, seq)
    qi = np.arange(blk)[:, None]
    ki = np.arange(blk)[None, :]
    bkt_p = np.stack([_bucket_tile(qi - ki), _bucket_tile(blk + qi - ki)])
    sq = np.tile(np.arange(dl), 2)[:, None]
    sk = np.arange(page)[None, :]
    new_dist = np.where(sk < dl, sq - sk, -1)
    bkt_s = np.stack([_bucket_tile(page + sq - sk), _bucket_tile(new_dist)])

    hp = x_prompt.reshape(tp, d)
    hs = x_sample.reshape(ts, d)
    outs = {n: [] for n in ("kp", "vp", "ks", "vs", "rp", "ip", "rs", "is")}
    tm_p = math.gcd(TOKEN_TILE, tp)
    tm_s = math.gcd(TOKEN_TILE, ts)
    for l in range(depth):
        lambda_init = 0.8 - 0.6 * math.exp(-0.3 * l)
        w_l = w_in[l].astype(BF16)
        g_mix = norm_mix_g[l].reshape(1, d)
        lams = [a[l].reshape(1, hd) for a in (lambda_q1, lambda_k1, lambda_q2, lambda_k2)]
        sub_g = subln_g[l].reshape(1, vd)

        w_qvu = jnp.concatenate([w_l[:, :qk_w], w_l[:, 2 * qk_w:mix_w]], axis=1)
        w_kt = jnp.transpose(w_l[:, qk_w:2 * qk_w])
        qp, ktp, vp, up, ktpb, vpb = _in_proj(hp, g_mix, w_qvu, w_kt, scale, tm_p, page, n_heads)
        qs, kts, vs, us, _, _ = _in_proj(hs, g_mix, w_qvu, w_kt, scale, tm_s, None, n_heads)
        ks = jnp.transpose(kts)

        bias_p = _rel_bias_tiles(rel_bias, bkt_p)
        bias_s = _rel_bias_tiles(rel_bias, bkt_s)
        bias_s = jnp.transpose(bias_s, (1, 0, 2, 3)).reshape(2, n_heads * 2 * dl, page)

        oa_p = _attn_prompt(lams, sub_g, qp, ktpb, vpb, bias_p, bp, seq, n_heads, lambda_init)
        oa_s = _attn_sample(page_table, lams, sub_g, qs, ks, vs, bias_s, ckt, cv, l, n_heads, lambda_init)

        lb, bb, pwc = _s5_prep(ssm_a_re[l], ssm_a_im[l], ssm_log_dt[l], ssm_b_re[l], ssm_b_im[l], seg)
        bb_bd, cw_bd = _block_diag_weights(bb, ssm_c_re[l], ssm_c_im[l], cw)
        d_vec = ssm_d[l].reshape(1, ssm_w)
        zeros = jnp.zeros((bp, 2, SUBLANES, n_groups * n_state), F32)
        yg_p, st_p = _s5(up, bb_bd, cw_bd, d_vec, lb, pwc, zeros, bp, seg, seg, True)
        s0 = jnp.stack([state_ssm_re[l], state_ssm_im[l]], axis=1).reshape(
            db // SUBLANES, SUBLANES, 2, n_groups * n_state)
        s0 = jnp.transpose(s0, (0, 2, 1, 3))
        yg_s, st_s = _s5(us, bb_bd, cw_bd, d_vec, lb, pwc, s0, db // SUBLANES, dl, dl, False)

        wts = dict(
            gm=g_mix, wg=w_l[:, mix_w:], wglu=w_glu[l].astype(BF16), bglu=b_glu[l].reshape(1, ssm_w),
            wa=w_attn_up[l].astype(BF16), ws=w_ssm_up[l].astype(BF16), wo=w_out[l].astype(BF16),
            gf=norm_ffn_g[l].reshape(1, d), wr=jnp.transpose(w_router[l]).astype(BF16),
            br=b_router[l].reshape(n_experts, 1))
        rt = math.gcd(ROUTE_TILE, math.gcd(tp, ts))
        h_p, hn_p, ti_p, pr_p, lr_p, cnt_p = _merge(hp, oa_p, yg_p, wts, math.gcd(MERGE_TILE, tp), rt)
        h_s, hn_s, ti_s, pr_s, lr_s, cnt_s = _merge(hs, oa_s, yg_s, wts, math.gcd(MERGE_TILE, ts), rt)

        n_p, n_s = tp // rt, ts // rt
        plan = _route_plan(jnp.concatenate([cnt_p[:, :, 0], cnt_s[:, :, 0]], axis=0), MOE_TILE)
        worst = TOP_K * (tp + ts) + (SUBLANES - 1) * n_experts * (n_p + n_s) + n_experts * (MOE_TILE - 1)
        s_pad = (-(-worst // MOE_TILE) + 1) * MOE_TILE
        dst = _chunk_rows(plan, 0, n_p + n_s, _chunks_per_tile(rt, n_experts), s_pad - MOE_TILE)
        pos, xs = _dispatch(plan, dst, rt, hn_p, hn_s, jnp.concatenate([ti_p, ti_s], axis=1),
                            jnp.concatenate([lr_p, lr_s], axis=1), jnp.concatenate([pr_p, pr_s], axis=1),
                            s_pad, MOE_TILE)
        tile_start = jnp.arange(s_pad // MOE_TILE, dtype=I32) * MOE_TILE
        tile_expert = jnp.minimum(jnp.sum((plan["pad_end"][None, :] <= tile_start[:, None]).astype(I32), axis=1),
                                  n_experts - 1)
        n_used = (plan["pad_end"][-1:] // MOE_TILE).astype(I32)
        o_sorted = _moe(tile_expert, n_used, xs, w_e_gate[l], b_e_gate[l], w_e_up[l], b_e_up[l],
                        w_e_down[l], b_e_down[l], MOE_TILE)

        last = l == depth - 1
        g_out = norm_final_g.reshape(1, d) if last else None
        assert last, "multi-layer stacks need a combine variant without the final norm"
        hp = _combine(plan, dst, 0, rt, h_p, jnp.transpose(pos[:, :tp]), g_out, o_sorted)
        hs = _combine(plan, dst, n_p, rt, h_s, jnp.transpose(pos[:, tp:]), g_out, o_sorted)

        gp = n_groups * n_state
        st_p = jnp.transpose(st_p, (0, 2, 3, 1, 4)).reshape(bp, 2, SUBLANES, gp)[:, :, SUBLANES - 1]
        st_s = jnp.transpose(st_s, (0, 3, 2, 1, 4)).reshape(db, 2, gp)
        outs["kp"].append(jnp.transpose(ktp.reshape(bp, seq // page, n_heads, 2, hd, page), (0, 1, 5, 2, 3, 4)))
        outs["vp"].append(vp.reshape(bp, seq // page, page, n_heads, vd))
        outs["ks"].append(ks.reshape(db, dl, n_heads, 2, hd))
        outs["vs"].append(vs.reshape(db, dl, n_heads, vd))
        outs["rp"].append(st_p[:, 0].reshape(bp, n_groups, n_state))
        outs["ip"].append(st_p[:, 1].reshape(bp, n_groups, n_state))
        outs["rs"].append(st_s[:, 0].reshape(db, n_groups, n_state))
        outs["is"].append(st_s[:, 1].reshape(db, n_groups, n_state))

    st = lambda n: jnp.stack(outs[n])
    return (hp.reshape(bp, seq, d), hs.reshape(db, dl, d), st("kp"), st("vp"), st("ks"), st("vs"),
            st("rp"), st("ip"), st("rs"), st("is"))
```

```python
import functools
import math

import numpy as np
import jax
import jax.numpy as jnp
from jax import lax
from jax.experimental import pallas as pl
from jax.experimental.pallas import tpu as pltpu

F32 = jnp.float32
BF16 = jnp.bfloat16
I32 = jnp.int32

EPS = 1e-6
NEG_INF = -1e30
TOP_K = 4
N_BUCKETS = 32
MAX_EXACT = N_BUCKETS // 2
MAX_DISTANCE = 128
SWIGLU_LIMIT = 7.0
SWIGLU_ALPHA = 1.702

SUBLANES = 8
LANES = 128
VMEM_LIMIT = 48 * 1024 * 1024

ATTN_BLOCK = 512
PAGES_PER_STEP = 16
TOKEN_TILE = 1024
MOE_TILE = 512
MERGE_TILE = 1024
ROUTE_TILE = 256
S5_CHUNK = 512


def _cparams(sem):
    return pltpu.CompilerParams(dimension_semantics=sem, vmem_limit_bytes=VMEM_LIMIT)


def _rms(x, g):
    return x * lax.rsqrt(jnp.mean(x * x, axis=-1, keepdims=True) + EPS) * g


def _in_proj_kernel(x_ref, g_ref, w_ref, wkt_ref, q_ref, k_ref, v_ref, u_ref, kb_ref, vb_ref, *, scale, page, n_heads):
    xn = _rms(x_ref[...], g_ref[...]).astype(BF16)
    p = jnp.dot(xn, w_ref[...], preferred_element_type=F32)
    w3 = p.shape[1] // 3
    tm = p.shape[0]
    vd = w3 // n_heads
    q_ref[...] = (p[:, :w3] * scale).astype(q_ref.dtype)
    v = p[:, w3:2 * w3]
    for h in range(n_heads):
        v_ref[pl.ds(h, tm, stride=n_heads), :] = v[:, h * vd:(h + 1) * vd]
    vb_ref[...] = v.astype(BF16)
    u_ref[...] = p[:, 2 * w3:]
    kt = lax.dot_general(wkt_ref[...], xn, (((1,), (1,)), ((), ())), preferred_element_type=F32)
    if page is None:
        k_ref[...] = kt
        kb_ref[...] = kt.astype(BF16)
    else:
        for pg in range(kt.shape[1] // page):
            tile = kt[:, pg * page:(pg + 1) * page]
            k_ref[pg] = tile
            kb_ref[pg] = tile.astype(BF16)


def _in_proj(x, g, w_qvu, w_kt, scale, tm, page, n_heads):
    t, d = x.shape
    w3 = w_qvu.shape[1] // 3
    row = lambda i: (i, 0)
    fix = lambda i: (0, 0)
    f32o = jax.ShapeDtypeStruct((t, w3), F32)
    if page is None:
        kt_shape, kt_spec = (w3, t), pl.BlockSpec((w3, tm), lambda i: (0, i))
    else:
        kt_shape, kt_spec = (t // page, w3, page), pl.BlockSpec((tm // page, w3, page), lambda i: (i, 0, 0))
    tok = pl.BlockSpec((tm, w3), row)
    return pl.pallas_call(
        functools.partial(_in_proj_kernel, scale=scale, page=page, n_heads=n_heads),
        grid=(t // tm,),
        in_specs=[pl.BlockSpec((tm, d), row), pl.BlockSpec((1, d), fix), pl.BlockSpec((d, 3 * w3), fix),
                  pl.BlockSpec((w3, d), fix)],
        out_specs=[tok, kt_spec, pl.BlockSpec((tm * n_heads, w3 // n_heads), row), tok, kt_spec, tok],
        out_shape=[jax.ShapeDtypeStruct((t, w3), F32 if page is None else BF16), jax.ShapeDtypeStruct(kt_shape, F32),
                   jax.ShapeDtypeStruct((t * n_heads, w3 // n_heads), F32), f32o,
                   jax.ShapeDtypeStruct(kt_shape, BF16), jax.ShapeDtypeStruct((t, w3), BF16)],
        compiler_params=_cparams(("parallel",)),
        name="in_proj",
    )(x, g, w_qvu, w_kt)


def _bucket_np(n):
    nf = np.maximum(n, 1).astype(np.float64)
    large = MAX_EXACT + np.trunc(
        np.log(nf / MAX_EXACT) / math.log(MAX_DISTANCE / MAX_EXACT) * (N_BUCKETS - MAX_EXACT)).astype(np.int64)
    return np.where(n < MAX_EXACT, n, np.minimum(large, N_BUCKETS - 1)).astype(np.int32)


def _bucket_tile(dist):
    return np.where(dist < 0, -1, _bucket_np(np.maximum(dist, 0))).astype(np.int32)


def _rel_bias_kernel(rb_ref, bkt_ref, o_ref):
    h = pl.program_id(0)
    far = rb_ref[N_BUCKETS - 1, h]
    bkt = bkt_ref[...]
    acc = jnp.zeros(bkt.shape, F32)
    for b in range(N_BUCKETS - 1):
        acc = jnp.where(bkt == b, rb_ref[b, h] - far, acc)
    o_ref[0] = jnp.where(bkt < 0, NEG_INF, acc)


def _rel_bias_tiles(rel_bias, buckets):
    n_heads = rel_bias.shape[1]
    n, r, c = buckets.shape
    return pl.pallas_call(
        _rel_bias_kernel,
        grid=(n_heads,),
        in_specs=[pl.BlockSpec(memory_space=pltpu.SMEM), pl.BlockSpec((n, r, c), lambda h: (0, 0, 0))],
        out_specs=pl.BlockSpec((1, n, r, c), lambda h: (h, 0, 0, 0)),
        out_shape=jax.ShapeDtypeStruct((n_heads, n, r, c), F32),
        compiler_params=_cparams(("arbitrary",)),
        name="rel_bias",
    )(rel_bias, jnp.asarray(buckets))


def _lambda(lq1, lk1, lq2, lk2, lambda_init):
    a = jnp.sum(lq1[...] * lk1[...], axis=-1, keepdims=True)
    b = jnp.sum(lq2[...] * lk2[...], axis=-1, keepdims=True)
    return jnp.exp(a) - jnp.exp(b) + lambda_init


def _attn_p_kernel(lq1, lk1, lq2, lk2, g_ref, q_ref, k_ref, v_ref, b_ref, o_ref, qq_ref, m_ref, l_ref, acc_ref,
                   *, blk, page, n_heads, hd, lambda_init):
    i = pl.program_id(1)
    hw = 2 * hd
    ppb = blk // page
    lam = _lambda(lq1, lk1, lq2, lk2, lambda_init)
    lane = lax.broadcasted_iota(I32, (blk, hw), 1)
    for h in range(n_heads):
        q = q_ref[:, h * hw:(h + 1) * hw].astype(BF16)
        zero = jnp.zeros_like(q)
        qq_ref[h, :blk] = jnp.where(lane < hd, q, zero)
        qq_ref[h, blk:] = jnp.where(lane >= hd, q, zero)
    m_ref[...] = jnp.full(m_ref.shape, NEG_INF, F32)
    l_ref[...] = jnp.zeros(l_ref.shape, F32)
    acc_ref[...] = jnp.zeros(acc_ref.shape, F32)

    def step(j, bias_idx, keys=blk):
        off = pl.multiple_of(j * blk, blk)
        for h in range(n_heads):
            kt = jnp.concatenate([k_ref[j * ppb + p, h * hw:(h + 1) * hw, :] for p in range(keys // page)], axis=1)
            vj = v_ref[pl.ds(off, keys), h * hw:(h + 1) * hw]
            s = jnp.dot(qq_ref[h], kt, preferred_element_type=F32)
            if bias_idx is not None:
                s = (s.reshape(2, blk, keys) + b_ref[h, bias_idx, :, :keys][None]).reshape(2 * blk, keys)
            m = m_ref[h]
            m_new = jnp.maximum(m, jnp.max(s, axis=-1, keepdims=True))
            alpha = jnp.exp(m - m_new)
            p = jnp.exp(s - jnp.concatenate([m_new] * (keys // LANES), axis=1))
            l_ref[h] = alpha * l_ref[h] + jnp.sum(p, axis=-1, keepdims=True)
            acc_ref[h] = alpha * acc_ref[h] + jnp.dot(p.astype(BF16), vj, preferred_element_type=F32)
            m_ref[h] = m_new

    def diag_tail(j):
        half = blk // 2
        off = pl.multiple_of(j * blk + half, half)
        for h in range(n_heads):
            kt = jnp.concatenate([k_ref[j * ppb + p, h * hw:(h + 1) * hw, :] for p in range(ppb // 2, ppb)], axis=1)
            vj = v_ref[pl.ds(off, half), h * hw:(h + 1) * hw]
            rows = lambda ref: jnp.concatenate([ref[h, half:blk], ref[h, blk + half:]], axis=0)
            s = jnp.dot(rows(qq_ref), kt, preferred_element_type=F32)
            s = (s.reshape(2, half, half) + b_ref[h, 0, half:, half:][None]).reshape(blk, half)
            m = rows(m_ref)
            m_new = jnp.maximum(m, jnp.max(s, axis=-1, keepdims=True))
            alpha = jnp.exp(m - m_new)
            p = jnp.exp(s - jnp.concatenate([m_new] * (half // LANES), axis=1))
            l_new = alpha * rows(l_ref) + jnp.sum(p, axis=-1, keepdims=True)
            acc_new = alpha * rows(acc_ref) + jnp.dot(p.astype(BF16), vj, preferred_element_type=F32)
            for ref, val in ((m_ref, m_new), (l_ref, l_new), (acc_ref, acc_new)):
                ref[h, half:blk] = val[:half]
                ref[h, blk + half:] = val[half:]

    def far(j, c):
        step(j, None)
        return c

    lax.fori_loop(0, i - 1, far, 0)

    @pl.when(i >= 1)
    def _():
        step(i - 1, 1)

    if ppb % 2 == 0 and (blk // 2) % LANES == 0:
        step(i, 0, keys=blk // 2)
        diag_tail(i)
    else:
        step(i, 0)
    for h in range(n_heads):
        o = acc_ref[h] / l_ref[h]
        o = o[:blk] - lam * o[blk:]
        o_ref[:, h * hw:(h + 1) * hw] = (_rms(o, g_ref[...]) * (1.0 - lambda_init)).astype(o_ref.dtype)


def _attn_prompt(lams, subln_g, q, ktb, vb, bias, n_batch, seq, n_heads, lambda_init):
    t, w = q.shape
    page = ktb.shape[2]
    hw = w // n_heads
    blk = math.gcd(ATTN_BLOCK, seq)
    nq = seq // blk
    fix = lambda b, i: (0, 0)
    return pl.pallas_call(
        functools.partial(_attn_p_kernel, blk=blk, page=page, n_heads=n_heads, hd=hw // 2,
                          lambda_init=lambda_init),
        grid=(n_batch, nq),
        in_specs=[pl.BlockSpec((1, hw // 2), fix)] * 4 + [
            pl.BlockSpec((1, hw), fix),
            pl.BlockSpec((blk, w), lambda b, i: (b * nq + i, 0)),
            pl.BlockSpec((seq // page, w, page), lambda b, i: (b, 0, 0)),
            pl.BlockSpec((seq, w), lambda b, i: (b, 0)),
            pl.BlockSpec((n_heads, 2, blk, blk), lambda b, i: (0, 0, 0, 0)),
        ],
        out_specs=pl.BlockSpec((blk, w), lambda b, i: (b * nq + i, 0)),
        out_shape=jax.ShapeDtypeStruct((t, w), BF16),
        scratch_shapes=[pltpu.VMEM((n_heads, 2 * blk, hw), BF16), pltpu.VMEM((n_heads, 2 * blk, LANES), F32),
                        pltpu.VMEM((n_heads, 2 * blk, LANES), F32), pltpu.VMEM((n_heads, 2 * blk, hw), F32)],
        compiler_params=_cparams(("parallel", "parallel")),
        name="attn_p",
    )(*lams, subln_g, q, ktb, vb, bias)


def _attn_s_kernel(pt_ref, lq1, lk1, lq2, lk2, g_ref, q_ref, kn_ref, vn_ref, b_ref, *rest,
                   pps, n_chunks, n_heads, hd, lambda_init):
    k_refs = rest[:pps]
    v_refs = rest[pps:2 * pps]
    o_ref = rest[2 * pps]
    m_ref, l_ref, acc_ref, kbf_ref, vbf_ref = rest[2 * pps + 1:]
    j = pl.program_id(1)
    last = n_chunks - 1
    dl = q_ref.shape[0]
    vd = 2 * hd
    hr = 2 * dl
    page = k_refs[0].shape[3]

    @pl.when(j == 0)
    def _():
        m_ref[...] = jnp.full(m_ref.shape, NEG_INF, F32)
        l_ref[...] = jnp.zeros(l_ref.shape, F32)
        acc_ref[...] = jnp.zeros(acc_ref.shape, F32)

    q8 = q_ref[...]
    qt = jnp.concatenate([q8] * (2 * n_heads), axis=0)
    lane = lax.broadcasted_iota(I32, qt.shape, 1)
    row = lax.broadcasted_iota(I32, qt.shape, 0)
    qbd = jnp.where(lane // hd == row // dl, qt, 0.0).astype(BF16)

    def update(s_list, v_heads):
        m_old = m_ref[...]
        m_new = m_old
        for s in s_list:
            m_new = jnp.maximum(m_new, jnp.max(s, axis=-1, keepdims=True))
        alpha = jnp.exp(m_old - m_new)
        l_new = alpha * l_ref[...]
        acc = alpha * acc_ref[...]
        parts = [acc[h * hr:(h + 1) * hr] for h in range(n_heads)]
        for s, v_of in zip(s_list, v_heads):
            p = jnp.exp(s - m_new)
            l_new = l_new + jnp.sum(p, axis=-1, keepdims=True)
            pb = p.astype(BF16)
            for h in range(n_heads):
                parts[h] = parts[h] + jnp.dot(pb[h * hr:(h + 1) * hr], v_of(h).astype(BF16),
                                              preferred_element_type=F32)
        m_ref[...] = m_new
        l_ref[...] = l_new
        acc_ref[...] = jnp.concatenate(parts, axis=0)

    for p in range(pps):
        kbf_ref[:, p * page:(p + 1) * page] = k_refs[p][0, 0].astype(BF16)
        for h in range(n_heads):
            vbf_ref[h, p * page:(p + 1) * page, :] = v_refs[p][0, 0, pl.ds(h, page, stride=n_heads), :].astype(BF16)
    s = jnp.dot(qbd, kbf_ref[...], preferred_element_type=F32)
    tail = s[:, (pps - 1) * page:] + jnp.where(j == last, b_ref[0], 0.0)
    s = jnp.concatenate([s[:, :(pps - 1) * page], tail], axis=1) if pps > 1 else tail
    update([s], [lambda h: vbf_ref[h]])

    @pl.when(j == last)
    def _():
        pad = jnp.zeros((page - dl, kn_ref.shape[1]), F32)
        kn = jnp.concatenate([kn_ref[...], pad], axis=0).astype(BF16)
        vpad = jnp.zeros((page - dl, vd), F32)
        s_new = lax.dot_general(qbd, kn, (((1,), (1,)), ((), ())), preferred_element_type=F32) + b_ref[1]
        update([s_new], [lambda h: jnp.concatenate([vn_ref[pl.ds(h, dl, stride=n_heads), :], vpad], axis=0)])
        lam = _lambda(lq1, lk1, lq2, lk2, lambda_init)
        o = acc_ref[...] / l_ref[...]
        for h in range(n_heads):
            oh = o[h * hr:h * hr + dl] - lam * o[h * hr + dl:(h + 1) * hr]
            o_ref[:, h * vd:(h + 1) * vd] = _rms(oh, g_ref[...]) * (1.0 - lambda_init)


def _attn_sample(page_table, lams, subln_g, q, k_new, v_new, bias, cache_kt, cache_v, layer, n_heads,
                 lambda_init):
    db, n_pages = page_table.shape
    t, w = q.shape
    dl = t // db
    page = cache_kt.shape[3]
    hd = w // (2 * n_heads)
    pps = math.gcd(PAGES_PER_STEP, n_pages)
    rows = 2 * n_heads * dl
    fix = lambda b, j, pt: (0, 0)
    seq = lambda b, j, pt: (b, 0)

    def page_spec(p, shape):
        return pl.BlockSpec((1, 1) + shape, lambda b, j, pt: (layer, pt[b, j * pps + p], 0, 0))

    grid_spec = pltpu.PrefetchScalarGridSpec(
        num_scalar_prefetch=1,
        grid=(db, n_pages // pps),
        in_specs=[pl.BlockSpec((1, hd), fix)] * 4 + [
            pl.BlockSpec((1, 2 * hd), fix),
            pl.BlockSpec((dl, w), seq), pl.BlockSpec((dl, w), seq), pl.BlockSpec((dl * n_heads, 2 * hd), seq),
            pl.BlockSpec((2, rows, page), lambda b, j, pt: (0, 0, 0)),
        ] + [page_spec(p, (w, page)) for p in range(pps)]
        + [page_spec(p, (page * n_heads, 2 * hd)) for p in range(pps)],
        out_specs=pl.BlockSpec((dl, w), seq),
        scratch_shapes=[pltpu.VMEM((rows, 1), F32), pltpu.VMEM((rows, 1), F32), pltpu.VMEM((rows, 2 * hd), F32),
                        pltpu.VMEM((w, pps * page), BF16), pltpu.VMEM((n_heads, pps * page, 2 * hd), BF16)],
    )
    return pl.pallas_call(
        functools.partial(_attn_s_kernel, pps=pps, n_chunks=n_pages // pps, n_heads=n_heads, hd=hd,
                          lambda_init=lambda_init),
        grid_spec=grid_spec,
        out_shape=jax.ShapeDtypeStruct((t, w), F32),
        compiler_params=_cparams(("parallel", "arbitrary")),
        name="attn_s",
    )(page_table, *lams, subln_g, q, k_new, v_new, bias, *([cache_kt] * pps), *([cache_v] * pps))


def _s5_prep_kernel(are_ref, aim_ref, ldt_ref, bre_ref, bim_ref, lb_ref, bb_ref, pwc_ref, *, seg):
    dt = jnp.exp(ldt_ref[...])
    a_re = jnp.minimum(are_ref[...], -1e-4)
    a_im = aim_ref[...]

    def lam_pow(n):
        mag = jnp.exp(a_re * dt * n)
        ang = a_im * dt * n
        return mag * jnp.cos(ang), mag * jnp.sin(ang)

    lb_re, lb_im = lam_pow(1.0)
    lb_ref[0] = lb_re
    lb_ref[1] = lb_im
    den = a_re * a_re + a_im * a_im
    nr = lb_re - 1.0
    co_re = (nr * a_re + lb_im * a_im) / den
    co_im = (lb_im * a_re - nr * a_im) / den
    bb_ref[0] = co_re * bre_ref[...] - co_im * bim_ref[...]
    bb_ref[1] = co_re * bim_ref[...] + co_im * bre_ref[...]
    n_dbl = pwc_ref.shape[1]
    for s in range(n_dbl):
        c_re, c_im = lam_pow(float(seg * (1 << s)))
        pwc_ref[0, s:s + 1, :] = c_re
        pwc_ref[1, s:s + 1, :] = c_im


def _s5_prep(a_re, a_im, log_dt, b_re, b_im, seg):
    g, p = a_re.shape
    c = b_re.shape[-1]
    gp = g * p
    flat = lambda a: a.reshape(1, gp)
    ldt = jnp.broadcast_to(log_dt[:, None], (g, p)).reshape(1, gp)
    chan = lambda b: jnp.transpose(b, (2, 0, 1)).reshape(c, gp)
    n_dbl = 3
    full = lambda shape: pl.BlockSpec(shape, lambda: (0,) * len(shape))
    return pl.pallas_call(
        functools.partial(_s5_prep_kernel, seg=seg),
        in_specs=[full((1, gp))] * 3 + [full((c, gp))] * 2,
        out_specs=[full((2, 1, gp)), full((2, c, gp)), full((2, n_dbl, gp))],
        out_shape=[jax.ShapeDtypeStruct((2, 1, gp), F32), jax.ShapeDtypeStruct((2, c, gp), F32),
                   jax.ShapeDtypeStruct((2, n_dbl, gp), F32)],
        compiler_params=pltpu.CompilerParams(vmem_limit_bytes=VMEM_LIMIT),
        name="s5_prep",
    )(flat(a_re), flat(a_im), ldt, chan(b_re), chan(b_im))


def _cmul(ar, ai, br, bi):
    return ar * br - ai * bi, ar * bi + ai * br


def _s5_kernel(u_ref, bb_ref, cw_ref, d_ref, lb_ref, pwc_ref, s0_ref, y_ref, st_ref, up_ref, h_ref, hb_ref,
               *, seg, stride, chain):
    cw = lb_ref.shape[2]
    n = SUBLANES
    unroll = math.gcd(seg // 2, 4)
    for i in range(seg):
        up_ref[i * n:(i + 1) * n, :] = u_ref[pl.ds(i, n, stride=stride), :]
    h_ref[...] = jnp.dot(up_ref[...].astype(BF16), bb_ref[0], preferred_element_type=F32)
    lr = jnp.broadcast_to(lb_ref[0], (n, cw))
    li = jnp.broadcast_to(lb_ref[1], (n, cw))

    rows2 = 2 * n

    def step(i, hr, hi):
        r = pl.multiple_of(i * n, n)
        tr, ti = _cmul(lr, li, hr, hi)
        return tr + h_ref[pl.ds(r, n), :cw], ti + h_ref[pl.ds(r, n), cw:]

    def scan(k, carry):
        hr0, hi0 = step(2 * k, *carry)
        hr1, hi1 = step(2 * k + 1, hr0, hi0)
        r = pl.multiple_of(k * rows2, rows2)
        if chain:
            h_ref[pl.ds(r, rows2), :cw] = jnp.concatenate([hr0, hr1], axis=0)
            h_ref[pl.ds(r, rows2), cw:] = jnp.concatenate([hi0, hi1], axis=0)
        else:
            hb_ref[pl.ds(r, rows2), :cw] = jnp.concatenate([hr0, hr1], axis=0).astype(BF16)
            hb_ref[pl.ds(r, rows2), cw:] = jnp.concatenate([hi0, hi1], axis=0).astype(BF16)
        return hr1, hi1

    end_r, end_i = lax.fori_loop(0, seg // 2, scan, (s0_ref[0, 0], s0_ref[0, 1]), unroll=unroll)

    if chain:
        sub = lax.broadcasted_iota(I32, (n, cw), 0)

        def shift(x, d):
            return jnp.where(sub >= d, pltpu.roll(x, d, axis=0), 0.0)

        tr, ti = end_r, end_i
        for s in range(pwc_ref.shape[1]):
            d = 1 << s
            pr = jnp.broadcast_to(pwc_ref[0, s:s + 1, :], (n, cw))
            pi = jnp.broadcast_to(pwc_ref[1, s:s + 1, :], (n, cw))
            ar, ai = _cmul(pr, pi, shift(tr, d), shift(ti, d))
            tr, ti = tr + ar, ti + ai
        end_r, end_i = tr, ti
        cr, ci = shift(tr, 1), shift(ti, 1)

        def fix(k, carry):
            r = pl.multiple_of(k * rows2, rows2)
            ar0, ai0 = _cmul(lr, li, *carry)
            ar1, ai1 = _cmul(lr, li, ar0, ai0)
            hb_ref[pl.ds(r, rows2), :cw] = (h_ref[pl.ds(r, rows2), :cw] + jnp.concatenate([ar0, ar1], axis=0)).astype(BF16)
            hb_ref[pl.ds(r, rows2), cw:] = (h_ref[pl.ds(r, rows2), cw:] + jnp.concatenate([ai0, ai1], axis=0)).astype(BF16)
            return ar1, ai1

        lax.fori_loop(0, seg // 2, fix, (cr, ci), unroll=unroll)

    st_ref[0, 0, 0] = end_r
    st_ref[0, 0, 1] = end_i
    y = jnp.dot(hb_ref[...], cw_ref[0], preferred_element_type=F32) + d_ref[...] * up_ref[...]
    up_ref[...] = jax.nn.gelu(y)
    for i in range(seg):
        y_ref[pl.ds(i, n, stride=stride), :] = up_ref[i * n:(i + 1) * n, :]


def _s5(u, bb_bd, cw_bd, d_vec, lb, pwc, s0, n_seq, seg, stride, chain):
    t, w = u.shape
    ncc, uw, cw2 = bb_bd.shape
    cw = cw2 // 2
    rows = SUBLANES * seg
    return pl.pallas_call(
        functools.partial(_s5_kernel, seg=seg, stride=stride, chain=chain),
        grid=(n_seq, ncc),
        in_specs=[
            pl.BlockSpec((rows, uw), lambda b, c: (b, c)),
            pl.BlockSpec((1, uw, cw2), lambda b, c: (c, 0, 0)),
            pl.BlockSpec((1, cw2, uw), lambda b, c: (c, 0, 0)),
            pl.BlockSpec((1, uw), lambda b, c: (0, c)),
            pl.BlockSpec((2, 1, cw), lambda b, c: (0, 0, c)),
            pl.BlockSpec((2, pwc.shape[1], cw), lambda b, c: (0, 0, c)),
            pl.BlockSpec((1, 2, SUBLANES, cw), lambda b, c: (b, 0, 0, c)),
        ],
        out_specs=[
            pl.BlockSpec((rows, uw), lambda b, c: (b, c)),
            pl.BlockSpec((1, 1, 2, SUBLANES, cw), lambda b, c: (b, c, 0, 0, 0)),
        ],
        out_shape=[jax.ShapeDtypeStruct((t, w), F32),
                   jax.ShapeDtypeStruct((n_seq, ncc, 2, SUBLANES, cw), F32)],
        scratch_shapes=[pltpu.VMEM((rows, uw), F32), pltpu.VMEM((rows, cw2), F32), pltpu.VMEM((rows, cw2), BF16)],
        compiler_params=_cparams(("parallel", "parallel")),
        name="s5_chain" if chain else "s5_step",
    )(u, bb_bd, cw_bd, d_vec, lb, pwc, s0)


def _block_diag_weights(bb, c_re, c_im, cw):
    _, c, gp = bb.shape
    g = c_re.shape[0]
    p = gp // g
    gpc = cw // p
    ncc = g // gpc
    eye = jnp.eye(gpc, dtype=F32)
    b4 = jnp.transpose(bb.reshape(2, c, ncc, gpc, p), (2, 0, 3, 1, 4))
    bbd = b4[:, :, :, :, None, :] * eye[None, None, :, None, :, None]
    bbd = jnp.transpose(bbd.reshape(ncc, 2, gpc * c, cw), (0, 2, 1, 3)).reshape(ncc, gpc * c, 2 * cw)
    cc = jnp.stack([c_re, -c_im]).reshape(2, ncc, gpc, c, p)
    c4 = jnp.transpose(cc, (1, 0, 2, 4, 3))
    cbd = c4[:, :, :, :, None, :] * eye[None, None, :, None, :, None]
    cbd = cbd.reshape(ncc, 2 * cw, gpc * c)
    return bbd.astype(BF16), cbd.astype(BF16)


def _merge_kernel(x_ref, oa_ref, yg_ref, gm_ref, wg_ref, wglu_ref, bglu_ref, wa_ref, ws_ref, wo_ref, gf_ref,
                  wr_ref, br_ref, h_ref, hn_ref, ti_ref, tp_ref, lr_ref, cnt_ref, *, rt):
    x = x_ref[...]
    tm, d = x.shape
    dot = functools.partial(jnp.dot, preferred_element_type=F32)
    xn = _rms(x, gm_ref[...]).astype(BF16)
    gates = jax.nn.sigmoid(dot(xn, wg_ref[...]))
    yg = yg_ref[...]
    ys = yg * jax.nn.sigmoid(dot(yg.astype(BF16), wglu_ref[...]) + bglu_ref[...])
    merged = (gates[:, :d] * dot(oa_ref[...].astype(BF16), wa_ref[...])
              + gates[:, d:] * dot(ys.astype(BF16), ws_ref[...]))
    h = x + dot(merged.astype(BF16), wo_ref[...])
    h_ref[...] = h
    hn = _rms(h, gf_ref[...]).astype(BF16)
    hn_ref[...] = hn
    logit = lax.dot_general(wr_ref[...], hn, (((1,), (1,)), ((), ())), preferred_element_type=F32) + br_ref[...]
    n_e = logit.shape[0]
    eid = lax.broadcasted_iota(I32, logit.shape, 0)
    vals, idxs = [], []
    for _ in range(TOP_K):
        mx = jnp.max(logit, axis=0, keepdims=True)
        ix = jnp.min(jnp.where(logit == mx, eid, n_e), axis=0, keepdims=True)
        vals.append(mx)
        idxs.append(ix)
        logit = jnp.where(eid == ix, -jnp.inf, logit)
    ex = [jnp.exp(v - vals[0]) for v in vals]
    tot = ex[0]
    for e in ex[1:]:
        tot = tot + e
    ti_ref[...] = jnp.concatenate(idxs, axis=0)
    tp_ref[...] = jnp.concatenate(ex, axis=0) / tot
    onehot = [jnp.where(eid == ix, 1.0, 0.0) for ix in idxs]
    before = (lax.broadcasted_iota(I32, (rt, rt), 0) < lax.broadcasted_iota(I32, (rt, rt), 1))
    before = jnp.where(before, 1.0, 0.0).astype(BF16)
    rank_tiles = []
    for sub in range(tm // rt):
        oh = [o[:, sub * rt:(sub + 1) * rt] for o in onehot]
        prefix = dot(jnp.concatenate(oh, axis=0).astype(BF16), before)
        base = jnp.zeros((n_e, 1), F32)
        ranks = []
        for k in range(TOP_K):
            ranks.append(jnp.sum(oh[k] * (prefix[k * n_e:(k + 1) * n_e] + base), axis=0, keepdims=True))
            base = base + jnp.sum(oh[k], axis=1, keepdims=True)
        rank_tiles.append(jnp.concatenate(ranks, axis=0))
        cnt_ref[sub] = base.astype(I32)
    lr_ref[...] = jnp.concatenate(rank_tiles, axis=1).astype(I32)


def _merge(x, oa, yg, w, tm, rt):
    t, d = x.shape
    aw = oa.shape[1]
    sw = yg.shape[1]
    n_e = w["wr"].shape[0]
    row = lambda i: (i, 0)
    col = lambda i: (0, i)
    fix = lambda i: (0, 0)
    full = lambda a: pl.BlockSpec(a.shape, fix)
    names = ["gm", "wg", "wglu", "bglu", "wa", "ws", "wo", "gf", "wr", "br"]
    kt = pl.BlockSpec((TOP_K, tm), col)
    return pl.pallas_call(
        functools.partial(_merge_kernel, rt=rt),
        grid=(t // tm,),
        in_specs=[pl.BlockSpec((tm, d), row), pl.BlockSpec((tm, aw), row), pl.BlockSpec((tm, sw), row)]
        + [full(w[n]) for n in names],
        out_specs=[pl.BlockSpec((tm, d), row), pl.BlockSpec((tm, d), row), kt, kt, kt,
                   pl.BlockSpec((tm // rt, n_e, 1), lambda i: (i, 0, 0))],
        out_shape=[jax.ShapeDtypeStruct((t, d), F32), jax.ShapeDtypeStruct((t, d), BF16),
                   jax.ShapeDtypeStruct((TOP_K, t), I32), jax.ShapeDtypeStruct((TOP_K, t), F32),
                   jax.ShapeDtypeStruct((TOP_K, t), I32), jax.ShapeDtypeStruct((t // rt, n_e, 1), I32)],
        compiler_params=_cparams(("parallel",)),
        name="merge",
    )(x, oa, yg, *[w[n] for n in names])


def _chunks_per_tile(tm, n_experts):
    rows = TOP_K * tm + (SUBLANES - 1) * n_experts
    return -(-rows // (SUBLANES * SUBLANES)) * SUBLANES


CHUNK_GROUP = 8


def _route_plan(cnt, moe_tile):
    n_e = cnt.shape[1]
    cnt8 = (cnt + SUBLANES - 1) // SUBLANES * SUBLANES
    total = jnp.sum(cnt8, axis=0)
    padded = (total + moe_tile - 1) // moe_tile * moe_tile
    pad_end = jnp.cumsum(padded)
    tile_base = (pad_end - padded)[None, :] + jnp.cumsum(cnt8, axis=0) - cnt8
    off_end = jnp.cumsum(cnt8, axis=1)
    off8 = off_end - cnt8
    nch = off_end[:, -1] // SUBLANES
    ngrp = (nch + CHUNK_GROUP - 1) // CHUNK_GROUP
    return dict(padded=padded.astype(I32), pad_end=pad_end.astype(I32), tile_base=tile_base, off8=off8,
                off_end=off_end, nch=nch.astype(I32), ngrp=ngrp.astype(I32), n_e=n_e)


def _chunk_rows(plan, lo, hi, g_max, spare_row):
    off_end = plan["off_end"][lo:hi]
    delta = (plan["tile_base"] - plan["off8"])[lo:hi]
    row0 = jnp.arange(g_max, dtype=I32) * SUBLANES
    grp = jnp.sum((off_end[:, None, :] <= row0[None, :, None]).astype(I32), axis=-1)
    grp = jnp.minimum(grp, plan["n_e"] - 1)
    sel = grp[:, :, None] == jnp.arange(plan["n_e"], dtype=I32)[None, None, :]
    dst = row0[None, :] + jnp.sum(jnp.where(sel, delta[:, None, :], 0), axis=-1)
    g = jnp.arange(g_max, dtype=I32)[None, :]
    nch = plan["nch"][lo:hi, None]
    parity = (jnp.arange(lo, hi, dtype=I32) % 2)[:, None]
    spare = spare_row + (parity * CHUNK_GROUP + g - nch) * SUBLANES
    return jnp.where(g < nch, dst, spare).astype(I32)[:, None, :]


def _dispatch_kernel(nch_ref, pe_ref, pd_ref, dst_ref, hnp_ref, hns_ref, ix_ref, lr_ref, pr_ref, off_ref, pos_ref,
                     xs_hbm, xbuf, zbuf, sems, zsem, *, n, n_p, moe_tile):
    i = pl.program_id(0)
    tm = hnp_ref.shape[0]
    rows = xbuf.shape[1]
    n_e = off_ref.shape[1]
    n_moe_tiles = xs_hbm.shape[0] // moe_tile
    slot = lax.rem(i, 2)

    def chunk(s, g, row):
        r = pl.multiple_of(row, SUBLANES)
        src = xbuf.at[s, pl.ds(pl.multiple_of(g * SUBLANES, SUBLANES), SUBLANES), :]
        return pltpu.make_async_copy(src, xs_hbm.at[pl.ds(r, SUBLANES), :], sems.at[s])

    def drain(s, groups):
        grp_rows = CHUNK_GROUP * SUBLANES
        for bit in range((rows // grp_rows).bit_length()):
            span = grp_rows << bit
            if span > rows:
                continue

            @pl.when((groups >> bit) & 1 == 1)
            def _():
                pltpu.make_async_copy(xbuf.at[s, pl.ds(0, span), :], xs_hbm.at[pl.ds(0, span), :], sems.at[s]).wait()

    @pl.when(i == 0)
    def _():
        zbuf[...] = jnp.zeros(zbuf.shape, zbuf.dtype)
        n_used = pe_ref[n_e - 1] // moe_tile

        def zero_tile(row):
            r = pl.multiple_of(row, SUBLANES)
            return pltpu.make_async_copy(zbuf, xs_hbm.at[pl.ds(r, moe_tile), :], zsem.at[0])

        for e in range(n_e):
            @pl.when(pd_ref[e] > 0)
            def _():
                zero_tile(jnp.maximum(pe_ref[e] - moe_tile, 0)).start()

        def start_rest(t, c):
            zero_tile(t * moe_tile).start()
            return c
        lax.fori_loop(n_used, n_moe_tiles, start_rest, 0)
        for e in range(n_e):
            @pl.when(pd_ref[e] > 0)
            def _():
                zero_tile(0).wait()

        def wait_rest(t, c):
            zero_tile(0).wait()
            return c
        lax.fori_loop(n_used, n_moe_tiles, wait_rest, 0)

    @pl.when(i >= 2)
    def _():
        drain(slot, nch_ref[jnp.maximum(i - 2, 0)])

    hn = jnp.where(i < n_p, hnp_ref[...], hns_ref[...])
    eid = lax.broadcasted_iota(I32, (n_e, tm), 0)
    off = off_ref[0].astype(F32)
    r_iota = lax.broadcasted_iota(I32, (rows, tm), 0)
    prob = jnp.zeros((rows, tm), F32)
    pos_rows = []
    for k in range(TOP_K):
        onehot = jnp.where(eid == ix_ref[k:k + 1, :], 1.0, 0.0)
        pos = jnp.sum(onehot * off, axis=0, keepdims=True).astype(I32) + lr_ref[k:k + 1, :]
        pos_rows.append(pos)
        prob = jnp.where(r_iota == pos, pr_ref[k:k + 1, :], prob)
    sel = jnp.where(prob != 0.0, 1.0, 0.0)
    pos_ref[...] = jnp.concatenate(pos_rows, axis=0)
    d = hn.shape[1]
    xbuf[slot, :, :d] = jnp.dot(sel.astype(BF16), hn, preferred_element_type=F32)
    xbuf[slot, :, d:] = jnp.broadcast_to(jnp.sum(prob, axis=1, keepdims=True), (rows, LANES))

    def send(grp, c):
        for u in range(CHUNK_GROUP):
            g = grp * CHUNK_GROUP + u
            chunk(slot, g, dst_ref[0, 0, g]).start()
        return c
    lax.fori_loop(0, nch_ref[i], send, 0)

    @pl.when(i == n - 1)
    def _():
        drain(slot, nch_ref[i])
        if n >= 2:
            drain(1 - slot, nch_ref[jnp.maximum(i - 1, 0)])


def _dispatch(plan, dst, tm, hn_p, hn_s, top_i, lrank, probs, s_pad, moe_tile):
    d = hn_p.shape[1]
    n_p, n_s = hn_p.shape[0] // tm, hn_s.shape[0] // tm
    n = n_p + n_s
    n_e = plan["n_e"]
    g_max = _chunks_per_tile(tm, n_e)
    rows = g_max * SUBLANES
    off = plan["off8"].astype(I32)[:, :, None]
    kt = pl.BlockSpec((TOP_K, tm), lambda i, *_: (0, i))
    grid_spec = pltpu.PrefetchScalarGridSpec(
        num_scalar_prefetch=3,
        grid=(n,),
        in_specs=[
            pl.BlockSpec((1, 1, g_max), lambda i, *_: (i, 0, 0), memory_space=pltpu.SMEM),
            pl.BlockSpec((tm, d), lambda i, *_: (jnp.minimum(i, n_p - 1), 0)),
            pl.BlockSpec((tm, d), lambda i, *_: (jnp.maximum(i - n_p, 0), 0)),
            kt, kt, kt,
            pl.BlockSpec((1, n_e, 1), lambda i, *_: (i, 0, 0)),
        ],
        out_specs=[kt, pl.BlockSpec(memory_space=pl.ANY)],
        scratch_shapes=[pltpu.VMEM((2, rows, d + LANES), F32), pltpu.VMEM((moe_tile, d + LANES), F32),
                        pltpu.SemaphoreType.DMA((2,)), pltpu.SemaphoreType.DMA((1,))])
    return pl.pallas_call(
        functools.partial(_dispatch_kernel, n=n, n_p=n_p, moe_tile=moe_tile),
        grid_spec=grid_spec,
        out_shape=[jax.ShapeDtypeStruct((TOP_K, n * tm), I32), jax.ShapeDtypeStruct((s_pad, d + LANES), F32)],
        compiler_params=_cparams(("arbitrary",)),
        name="dispatch",
    )(plan["ngrp"], plan["pad_end"], plan["padded"], dst, hn_p, hn_s, top_i, lrank, probs, off)


def _moe_kernel(te_ref, nu_ref, x_ref, wg_ref, bg_ref, wu_ref, bu_ref, wd_ref, bd_ref, o_ref, wgb, wub, wdb):
    i = pl.program_id(0)
    used = i < nu_ref[0]
    e = te_ref[i]
    prev = te_ref[jnp.maximum(i - 1, 0)]

    @pl.when(used & ((i == 0) | (e != prev)))
    def _():
        wgb[...] = wg_ref[0].astype(BF16)
        wub[...] = wu_ref[0].astype(BF16)
        wdb[...] = wd_ref[0].astype(BF16)

    @pl.when(used)
    def _():
        d = o_ref.shape[1]
        x = x_ref[:, :d].astype(BF16)
        gt = jnp.dot(x, wgb[...], preferred_element_type=F32) + bg_ref[0]
        up = jnp.dot(x, wub[...], preferred_element_type=F32) + bu_ref[0]
        gt = jnp.minimum(gt, SWIGLU_LIMIT)
        up = jnp.clip(up, -SWIGLU_LIMIT, SWIGLU_LIMIT)
        hh = (up + 1.0) * (gt * jax.nn.sigmoid(SWIGLU_ALPHA * gt))
        o_ref[...] = (jnp.dot(hh.astype(BF16), wdb[...], preferred_element_type=F32) + bd_ref[0]) * x_ref[:, d:d + 1]

    @pl.when(jnp.logical_not(used))
    def _():
        o_ref[...] = jnp.zeros(o_ref.shape, F32)


def _moe(tile_expert, n_used, xs, wg, bg, wu, bu, wd, bd, tm):
    n_e, d, f = wg.shape
    s_pad = xs.shape[0]
    wmap = lambda i, te, nu: (te[i], 0, 0)
    row = lambda i, te, nu: (i, 0)
    grid_spec = pltpu.PrefetchScalarGridSpec(
        num_scalar_prefetch=2,
        grid=(s_pad // tm,),
        in_specs=[
            pl.BlockSpec((tm, d + LANES), lambda i, te, nu: (jnp.minimum(i, jnp.maximum(nu[0] - 1, 0)), 0)),
            pl.BlockSpec((1, d, f), wmap), pl.BlockSpec((1, 1, f), wmap),
            pl.BlockSpec((1, d, f), wmap), pl.BlockSpec((1, 1, f), wmap),
            pl.BlockSpec((1, f, d), wmap), pl.BlockSpec((1, 1, d), wmap),
        ],
        out_specs=pl.BlockSpec((tm, d), row),
        scratch_shapes=[pltpu.VMEM((d, f), BF16), pltpu.VMEM((d, f), BF16), pltpu.VMEM((f, d), BF16)],
    )
    return pl.pallas_call(
        _moe_kernel,
        grid_spec=grid_spec,
        out_shape=jax.ShapeDtypeStruct((s_pad, d), F32),
        compiler_params=pltpu.CompilerParams(dimension_semantics=("arbitrary",),
                                             vmem_limit_bytes=56 * 1024 * 1024),
        name="moe",
    )(tile_expert, n_used, xs, wg, bg.reshape(n_e, 1, f), wu, bu.reshape(n_e, 1, f), wd, bd.reshape(n_e, 1, d))


def _combine_kernel(nch_ref, dst_ref, nxt_ref, h_ref, pos_ref, g_ref, o_hbm, y_ref, obuf, sems, *, n):
    i = pl.program_id(0)
    tm = h_ref.shape[0]
    rows = obuf.shape[1]
    slot = lax.rem(i, 2)

    def chunk(s, g, row):
        r = pl.multiple_of(row, SUBLANES)
        dst = obuf.at[s, pl.ds(pl.multiple_of(g * SUBLANES, SUBLANES), SUBLANES), :]
        return pltpu.make_async_copy(o_hbm.at[pl.ds(r, SUBLANES), :], dst, sems.at[s])

    def fetch(s, rows_ref, groups):
        def body(grp, c):
            for u in range(CHUNK_GROUP):
                g = grp * CHUNK_GROUP + u
                chunk(s, g, rows_ref[0, 0, g]).start()
            return c
        lax.fori_loop(0, groups, body, 0)

    @pl.when(i == 0)
    def _():
        obuf[...] = jnp.zeros(obuf.shape, obuf.dtype)
        fetch(0, dst_ref, nch_ref[0])

    @pl.when(i + 1 < n)
    def _():
        fetch(1 - slot, nxt_ref, nch_ref[jnp.minimum(i + 1, n - 1)])

    count = nch_ref[i]
    grp_rows = CHUNK_GROUP * SUBLANES
    for bit in range((rows // grp_rows).bit_length()):
        span = grp_rows << bit
        if span > rows:
            continue

        @pl.when((count >> bit) & 1 == 1)
        def _():
            pltpu.make_async_copy(o_hbm.at[pl.ds(0, span), :], obuf.at[slot, pl.ds(0, span), :], sems.at[slot]).wait()

    ob = obuf[slot].astype(BF16)
    lane = lax.broadcasted_iota(I32, (tm, rows), 1)
    w = jnp.zeros((tm, rows), F32)
    for k in range(TOP_K):
        w = w + jnp.where(lane == pos_ref[:, k:k + 1], 1.0, 0.0)
    moe = jnp.dot(w.astype(BF16), ob, preferred_element_type=F32)
    y_ref[...] = _rms(h_ref[...] + moe, g_ref[...])


def _combine(plan, dst_all, lo, tm, h, pos_t, g, o_sorted):
    t, d = h.shape
    n = t // tm
    g_max = _chunks_per_tile(tm, plan["n_e"])
    rows = g_max * SUBLANES
    dst = dst_all[lo:lo + n]
    tok = lambda i, *_: (i, 0)
    grid_spec = pltpu.PrefetchScalarGridSpec(
        num_scalar_prefetch=1,
        grid=(n,),
        in_specs=[
            pl.BlockSpec((1, 1, g_max), lambda i, *_: (i, 0, 0), memory_space=pltpu.SMEM),
            pl.BlockSpec((1, 1, g_max), lambda i, *_: (jnp.minimum(i + 1, n - 1), 0, 0), memory_space=pltpu.SMEM),
            pl.BlockSpec((tm, d), tok), pl.BlockSpec((tm, TOP_K), tok),
            pl.BlockSpec((1, d), lambda i, *_: (0, 0)),
            pl.BlockSpec(memory_space=pl.ANY),
        ],
        out_specs=pl.BlockSpec((tm, d), tok),
        scratch_shapes=[pltpu.VMEM((2, rows, d), F32), pltpu.SemaphoreType.DMA((2,))],
    )
    return pl.pallas_call(
        functools.partial(_combine_kernel, n=n),
        grid_spec=grid_spec,
        out_shape=jax.ShapeDtypeStruct((t, d), F32),
        compiler_params=_cparams(("arbitrary",)),
        name="combine",
    )(plan["ngrp"][lo:lo + n], dst, dst, h, pos_t, g, o_sorted)


def kernel(x_prompt, x_sample, cache_k, cache_v, state_ssm_re, state_ssm_im, page_table, norm_mix_g, w_in, lambda_q1, lambda_k1, lambda_q2, lambda_k2, subln_g, rel_bias, ssm_a_re, ssm_a_im, ssm_log_dt, ssm_b_re, ssm_b_im, ssm_c_re, ssm_c_im, ssm_d, w_glu, b_glu, w_attn_up, w_ssm_up, w_out, norm_ffn_g, w_router, b_router, w_e_gate, b_e_gate, w_e_up, b_e_up, w_e_down, b_e_down, norm_final_g):
    bp, seq, d = x_prompt.shape
    db, dl, _ = x_sample.shape
    depth, n_pool, page, n_heads, _, hd = cache_k.shape
    vd = cache_v.shape[-1]
    n_groups, n_state = ssm_a_re.shape[1:]
    gch = ssm_b_re.shape[-1]
    n_experts = w_router.shape[-1]
    qk_w = n_heads * 2 * hd
    attn_w = n_heads * vd
    ssm_w = n_groups * gch
    mix_w = 2 * qk_w + attn_w + ssm_w
    assert qk_w == attn_w == ssm_w and vd == 2 * hd
    assert seq % SUBLANES == 0 and db % SUBLANES == 0 and dl == SUBLANES
    tp, ts = bp * seq, db * dl
    scale = hd ** -0.5
    seg = seq // SUBLANES
    cw = min(S5_CHUNK, n_groups * n_state)

    ckt = jnp.transpose(cache_k, (0, 1, 3, 4, 5, 2)).reshape(depth, n_pool, qk_w, page)
    cv = cache_v.reshape(depth, n_pool, page * n_heads, vd)

    blk = math.gcd(ATTN_BLOCK, seq)
    qi = np.arange(blk)[:, None]
    ki = np.arange(blk)[None, :]
    bkt_p = np.stack([_bucket_tile(qi - ki), _bucket_tile(blk + qi - ki)])
    sq = np.tile(np.arange(dl), 2)[:, None]
    sk = np.arange(page)[None, :]
    new_dist = np.where(sk < dl, sq - sk, -1)
    bkt_s = np.stack([_bucket_tile(page + sq - sk), _bucket_tile(new_dist)])

    hp = x_prompt.reshape(tp, d)
    hs = x_sample.reshape(ts, d)
    outs = {n: [] for n in ("kp", "vp", "ks", "vs", "rp", "ip", "rs", "is")}
    tm_p = math.gcd(TOKEN_TILE, tp)
    tm_s = math.gcd(TOKEN_TILE, ts)
    for l in range(depth):
        lambda_init = 0.8 - 0.6 * math.exp(-0.3 * l)
        w_l = w_in[l].astype(BF16)
        g_mix = norm_mix_g[l].reshape(1, d)
        lams = [a[l].reshape(1, hd) for a in (lambda_q1, lambda_k1, lambda_q2, lambda_k2)]
        sub_g = subln_g[l].reshape(1, vd)

        w_qvu = jnp.concatenate([w_l[:, :qk_w], w_l[:, 2 * qk_w:mix_w]], axis=1)
        w_kt = jnp.transpose(w_l[:, qk_w:2 * qk_w])
        qp, ktp, vp, up, ktpb, vpb = _in_proj(hp, g_mix, w_qvu, w_kt, scale, tm_p, page, n_heads)
        qs, kts, vs, us, _, _ = _in_proj(hs, g_mix, w_qvu, w_kt, scale, tm_s, None, n_heads)
        ks = jnp.transpose(kts)

        bias_p = _rel_bias_tiles(rel_bias, bkt_p)
        bias_s = _rel_bias_tiles(rel_bias, bkt_s)
        bias_s = jnp.transpose(bias_s, (1, 0, 2, 3)).reshape(2, n_heads * 2 * dl, page)

        oa_p = _attn_prompt(lams, sub_g, qp, ktpb, vpb, bias_p, bp, seq, n_heads, lambda_init)
        oa_s = _attn_sample(page_table, lams, sub_g, qs, ks, vs, bias_s, ckt, cv, l, n_heads, lambda_init)

        lb, bb, pwc = _s5_prep(ssm_a_re[l], ssm_a_im[l], ssm_log_dt[l], ssm_b_re[l], ssm_b_im[l], seg)
        bb_bd, cw_bd = _block_diag_weights(bb, ssm_c_re[l], ssm_c_im[l], cw)
        d_vec = ssm_d[l].reshape(1, ssm_w)
        zeros = jnp.zeros((bp, 2, SUBLANES, n_groups * n_state), F32)
        yg_p, st_p = _s5(up, bb_bd, cw_bd, d_vec, lb, pwc, zeros, bp, seg, seg, True)
        s0 = jnp.stack([state_ssm_re[l], state_ssm_im[l]], axis=1).reshape(
            db // SUBLANES, SUBLANES, 2, n_groups * n_state)
        s0 = jnp.transpose(s0, (0, 2, 1, 3))
        yg_s, st_s = _s5(us, bb_bd, cw_bd, d_vec, lb, pwc, s0, db // SUBLANES, dl, dl, False)

        wts = dict(
            gm=g_mix, wg=w_l[:, mix_w:], wglu=w_glu[l].astype(BF16), bglu=b_glu[l].reshape(1, ssm_w),
            wa=w_attn_up[l].astype(BF16), ws=w_ssm_up[l].astype(BF16), wo=w_out[l].astype(BF16),
            gf=norm_ffn_g[l].reshape(1, d), wr=jnp.transpose(w_router[l]).astype(BF16),
            br=b_router[l].reshape(n_experts, 1))
        rt = math.gcd(ROUTE_TILE, math.gcd(tp, ts))
        h_p, hn_p, ti_p, pr_p, lr_p, cnt_p = _merge(hp, oa_p, yg_p, wts, math.gcd(MERGE_TILE, tp), rt)
        h_s, hn_s, ti_s, pr_s, lr_s, cnt_s = _merge(hs, oa_s, yg_s, wts, math.gcd(MERGE_TILE, ts), rt)

        n_p, n_s = tp // rt, ts // rt
        plan = _route_plan(jnp.concatenate([cnt_p[:, :, 0], cnt_s[:, :, 0]], axis=0), MOE_TILE)
        worst = TOP_K * (tp + ts) + (SUBLANES - 1) * n_experts * (n_p + n_s) + n_experts * (MOE_TILE - 1)
        s_pad = (-(-worst // MOE_TILE) + 1) * MOE_TILE
        dst = _chunk_rows(plan, 0, n_p + n_s, _chunks_per_tile(rt, n_experts), s_pad - MOE_TILE)
        pos, xs = _dispatch(plan, dst, rt, hn_p, hn_s, jnp.concatenate([ti_p, ti_s], axis=1),
                            jnp.concatenate([lr_p, lr_s], axis=1), jnp.concatenate([pr_p, pr_s], axis=1),
                            s_pad, MOE_TILE)
        tile_start = jnp.arange(s_pad // MOE_TILE, dtype=I32) * MOE_TILE
        tile_expert = jnp.minimum(jnp.sum((plan["pad_end"][None, :] <= tile_start[:, None]).astype(I32), axis=1),
                                  n_experts - 1)
        n_used = (plan["pad_end"][-1:] // MOE_TILE).astype(I32)
        o_sorted = _moe(tile_expert, n_used, xs, w_e_gate[l], b_e_gate[l], w_e_up[l], b_e_up[l],
                        w_e_down[l], b_e_down[l], MOE_TILE)

        last = l == depth - 1
        g_out = norm_final_g.reshape(1, d) if last else None
        assert last, "multi-layer stacks need a combine variant without the final norm"
        hp = _combine(plan, dst, 0, rt, h_p, jnp.transpose(pos[:, :tp]), g_out, o_sorted)
        hs = _combine(plan, dst, n_p, rt, h_s, jnp.transpose(pos[:, tp:]), g_out, o_sorted)

        gp = n_groups * n_state
        st_p = jnp.transpose(st_p, (0, 2, 3, 1, 4)).reshape(bp, 2, SUBLANES, gp)[:, :, SUBLANES - 1]
        st_s = jnp.transpose(st_s, (0, 3, 2, 1, 4)).reshape(db, 2, gp)
        outs["kp"].append(jnp.transpose(ktp.reshape(bp, seq // page, n_heads, 2, hd, page), (0, 1, 5, 2, 3, 4)))
        outs["vp"].append(vp.reshape(bp, seq // page, page, n_heads, vd))
        outs["ks"].append(ks.reshape(db, dl, n_heads, 2, hd))
        outs["vs"].append(vs.reshape(db, dl, n_heads, vd))
        outs["rp"].append(st_p[:, 0].reshape(bp, n_groups, n_state))
        outs["ip"].append(st_p[:, 1].reshape(bp, n_groups, n_state))
        outs["rs"].append(st_s[:, 0].reshape(db, n_groups, n_state))
        outs["is"].append(st_s[:, 1].reshape(db, n_groups, n_state))

    st = lambda n: jnp.stack(outs[n])
    return (hp.reshape(bp, seq, d), hs.reshape(db, dl, d), st("kp"), st("vp"), st("ks"), st("vs"),
            st("rp"), st("ip"), st("rs"), st("is"))
```

```python
import functools
import math

import numpy as np
import jax
import jax.numpy as jnp
from jax import lax
from jax.experimental import pallas as pl
from jax.experimental.pallas import tpu as pltpu

F32 = jnp.float32
BF16 = jnp.bfloat16
I32 = jnp.int32

EPS = 1e-6
NEG_INF = -1e30
TOP_K = 4
N_BUCKETS = 32
MAX_EXACT = N_BUCKETS // 2
MAX_DISTANCE = 128
SWIGLU_LIMIT = 7.0
SWIGLU_ALPHA = 1.702

SUBLANES = 8
LANES = 128
VMEM_LIMIT = 48 * 1024 * 1024

ATTN_BLOCK = 512
PAGES_PER_STEP = 16
TOKEN_TILE = 1024
MOE_TILE = 512
MERGE_TILE = 1024
ROUTE_TILE = 256
S5_CHUNK = 512


def _cparams(sem):
    return pltpu.CompilerParams(dimension_semantics=sem, vmem_limit_bytes=VMEM_LIMIT)


def _rms(x, g):
    return x * lax.rsqrt(jnp.mean(x * x, axis=-1, keepdims=True) + EPS) * g


def _in_proj_kernel(x_ref, g_ref, w_ref, wkt_ref, q_ref, k_ref, v_ref, u_ref, kb_ref, vb_ref, *, scale, page, n_heads):
    xn = _rms(x_ref[...], g_ref[...]).astype(BF16)
    p = jnp.dot(xn, w_ref[...], preferred_element_type=F32)
    w3 = p.shape[1] // 3
    tm = p.shape[0]
    vd = w3 // n_heads
    q_ref[...] = (p[:, :w3] * scale).astype(q_ref.dtype)
    v = p[:, w3:2 * w3]
    for h in range(n_heads):
        v_ref[pl.ds(h, tm, stride=n_heads), :] = v[:, h * vd:(h + 1) * vd]
    vb_ref[...] = v.astype(BF16)
    u_ref[...] = p[:, 2 * w3:]
    kt = lax.dot_general(wkt_ref[...], xn, (((1,), (1,)), ((), ())), preferred_element_type=F32)
    if page is None:
        k_ref[...] = kt
        kb_ref[...] = kt.astype(BF16)
    else:
        for pg in range(kt.shape[1] // page):
            tile = kt[:, pg * page:(pg + 1) * page]
            k_ref[pg] = tile
            kb_ref[pg] = tile.astype(BF16)


def _in_proj(x, g, w_qvu, w_kt, scale, tm, page, n_heads):
    t, d = x.shape
    w3 = w_qvu.shape[1] // 3
    row = lambda i: (i, 0)
    fix = lambda i: (0, 0)
    f32o = jax.ShapeDtypeStruct((t, w3), F32)
    if page is None:
        kt_shape, kt_spec = (w3, t), pl.BlockSpec((w3, tm), lambda i: (0, i))
    else:
        kt_shape, kt_spec = (t // page, w3, page), pl.BlockSpec((tm // page, w3, page), lambda i: (i, 0, 0))
    tok = pl.BlockSpec((tm, w3), row)
    return pl.pallas_call(
        functools.partial(_in_proj_kernel, scale=scale, page=page, n_heads=n_heads),
        grid=(t // tm,),
        in_specs=[pl.BlockSpec((tm, d), row), pl.BlockSpec((1, d), fix), pl.BlockSpec((d, 3 * w3), fix),
                  pl.BlockSpec((w3, d), fix)],
        out_specs=[tok, kt_spec, pl.BlockSpec((tm * n_heads, w3 // n_heads), row), tok, kt_spec, tok],
        out_shape=[jax.ShapeDtypeStruct((t, w3), F32 if page is None else BF16), jax.ShapeDtypeStruct(kt_shape, F32),
                   jax.ShapeDtypeStruct((t * n_heads, w3 // n_heads), F32), f32o,
                   jax.ShapeDtypeStruct(kt_shape, BF16), jax.ShapeDtypeStruct((t, w3), BF16)],
        compiler_params=_cparams(("parallel",)),
        name="in_proj",
    )(x, g, w_qvu, w_kt)


def _bucket_np(n):
    nf = np.maximum(n, 1).astype(np.float64)
    large = MAX_EXACT + np.trunc(
        np.log(nf / MAX_EXACT) / math.log(MAX_DISTANCE / MAX_EXACT) * (N_BUCKETS - MAX_EXACT)).astype(np.int64)
    return np.where(n < MAX_EXACT, n, np.minimum(large, N_BUCKETS - 1)).astype(np.int32)


def _bucket_tile(dist):
    return np.where(dist < 0, -1, _bucket_np(np.maximum(dist, 0))).astype(np.int32)


def _rel_bias_kernel(rb_ref, bkt_ref, o_ref, *, live, masked):
    h = pl.program_id(0)
    far = rb_ref[N_BUCKETS - 1, h]
    o_ref[0] = jnp.zeros(o_ref.shape[1:], F32)
    for n, r0, c0, nr, nc in masked:
        o_ref[0, n, r0:r0 + nr, c0:c0 + nc] = jnp.full((nr, nc), NEG_INF, F32)
    for n, r0, c0, nr, nc in live:
        bkt = bkt_ref[n, r0:r0 + nr, c0:c0 + nc]
        acc = jnp.zeros(bkt.shape, F32)
        for b in range(N_BUCKETS - 1):
            acc = jnp.where(bkt == b, rb_ref[b, h] - far, acc)
        o_ref[0, n, r0:r0 + nr, c0:c0 + nc] = jnp.where(bkt < 0, NEG_INF, acc)


def _rel_bias_tiles(rel_bias, buckets):
    n_heads = rel_bias.shape[1]
    n, r, c = buckets.shape
    wr, wc = (LANES, LANES) if r % LANES == 0 and c % LANES == 0 else (r, c)
    live, masked = [], []
    for t in range(n):
        for r0 in range(0, r, wr):
            for c0 in range(0, c, wc):
                win = buckets[t, r0:r0 + wr, c0:c0 + wc]
                if (win < 0).all():
                    masked.append((t, r0, c0, wr, wc))
                elif not (win >= N_BUCKETS - 1).all():
                    live.append((t, r0, c0, wr, wc))
    return pl.pallas_call(
        functools.partial(_rel_bias_kernel, live=tuple(live), masked=tuple(masked)),
        grid=(n_heads,),
        in_specs=[pl.BlockSpec(memory_space=pltpu.SMEM), pl.BlockSpec((n, r, c), lambda h: (0, 0, 0))],
        out_specs=pl.BlockSpec((1, n, r, c), lambda h: (h, 0, 0, 0)),
        out_shape=jax.ShapeDtypeStruct((n_heads, n, r, c), F32),
        compiler_params=_cparams(("arbitrary",)),
        name="rel_bias",
    )(rel_bias, jnp.asarray(buckets))


def _lambda(lq1, lk1, lq2, lk2, lambda_init):
    a = jnp.sum(lq1[...] * lk1[...], axis=-1, keepdims=True)
    b = jnp.sum(lq2[...] * lk2[...], axis=-1, keepdims=True)
    return jnp.exp(a) - jnp.exp(b) + lambda_init


def _attn_p_kernel(lq1, lk1, lq2, lk2, g_ref, q_ref, k_ref, v_ref, b_ref, o_ref, qq_ref, m_ref, l_ref, acc_ref,
                   *, blk, page, n_heads, hd, lambda_init):
    i = pl.program_id(1)
    hw = 2 * hd
    ppb = blk // page
    lam = _lambda(lq1, lk1, lq2, lk2, lambda_init)
    lane = lax.broadcasted_iota(I32, (blk, hw), 1)
    for h in range(n_heads):
        q = q_ref[:, h * hw:(h + 1) * hw].astype(BF16)
        zero = jnp.zeros_like(q)
        qq_ref[h, :blk] = jnp.where(lane < hd, q, zero)
        qq_ref[h, blk:] = jnp.where(lane >= hd, q, zero)
    m_ref[...] = jnp.full(m_ref.shape, NEG_INF, F32)
    l_ref[...] = jnp.zeros(l_ref.shape, F32)
    acc_ref[...] = jnp.zeros(acc_ref.shape, F32)

    def step(j, bias_idx, keys=blk):
        off = pl.multiple_of(j * blk, blk)
        for h in range(n_heads):
            kt = jnp.concatenate([k_ref[j * ppb + p, h * hw:(h + 1) * hw, :] for p in range(keys // page)], axis=1)
            vj = v_ref[pl.ds(off, keys), h * hw:(h + 1) * hw]
            s = jnp.dot(qq_ref[h], kt, preferred_element_type=F32)
            if bias_idx is not None:
                s = (s.reshape(2, blk, keys) + b_ref[h, bias_idx, :, :keys][None]).reshape(2 * blk, keys)
            m = m_ref[h]
            m_new = jnp.maximum(m, jnp.max(s, axis=-1, keepdims=True))
            alpha = jnp.exp(m - m_new)
            p = jnp.exp(s - jnp.concatenate([m_new] * (keys // LANES), axis=1))
            l_ref[h] = alpha * l_ref[h] + jnp.sum(p, axis=-1, keepdims=True)
            acc_ref[h] = alpha * acc_ref[h] + jnp.dot(p.astype(BF16), vj, preferred_element_type=F32)
            m_ref[h] = m_new

    def diag_tail(j):
        half = blk // 2
        off = pl.multiple_of(j * blk + half, half)
        for h in range(n_heads):
            kt = jnp.concatenate([k_ref[j * ppb + p, h * hw:(h + 1) * hw, :] for p in range(ppb // 2, ppb)], axis=1)
            vj = v_ref[pl.ds(off, half), h * hw:(h + 1) * hw]
            rows = lambda ref: jnp.concatenate([ref[h, half:blk], ref[h, blk + half:]], axis=0)
            s = jnp.dot(rows(qq_ref), kt, preferred_element_type=F32)
            s = (s.reshape(2, half, half) + b_ref[h, 0, half:, half:][None]).reshape(blk, half)
            m = rows(m_ref)
            m_new = jnp.maximum(m, jnp.max(s, axis=-1, keepdims=True))
            alpha = jnp.exp(m - m_new)
            p = jnp.exp(s - jnp.concatenate([m_new] * (half // LANES), axis=1))
            l_new = alpha * rows(l_ref) + jnp.sum(p, axis=-1, keepdims=True)
            acc_new = alpha * rows(acc_ref) + jnp.dot(p.astype(BF16), vj, preferred_element_type=F32)
            for ref, val in ((m_ref, m_new), (l_ref, l_new), (acc_ref, acc_new)):
                ref[h, half:blk] = val[:half]
                ref[h, blk + half:] = val[half:]

    def far(j, c):
        step(j, None)
        return c

    lax.fori_loop(0, i - 1, far, 0)

    @pl.when(i >= 1)
    def _():
        step(i - 1, 1)

    if ppb % 2 == 0 and (blk // 2) % LANES == 0:
        step(i, 0, keys=blk // 2)
        diag_tail(i)
    else:
        step(i, 0)
    for h in range(n_heads):
        o = acc_ref[h] / l_ref[h]
        o = o[:blk] - lam * o[blk:]
        o_ref[:, h * hw:(h + 1) * hw] = (_rms(o, g_ref[...]) * (1.0 - lambda_init)).astype(o_ref.dtype)


def _attn_prompt(lams, subln_g, q, ktb, vb, bias, n_batch, seq, n_heads, lambda_init):
    t, w = q.shape
    page = ktb.shape[2]
    hw = w // n_heads
    blk = math.gcd(ATTN_BLOCK, seq)
    nq = seq // blk
    fix = lambda b, i: (0, 0)
    return pl.pallas_call(
        functools.partial(_attn_p_kernel, blk=blk, page=page, n_heads=n_heads, hd=hw // 2,
                          lambda_init=lambda_init),
        grid=(n_batch, nq),
        in_specs=[pl.BlockSpec((1, hw // 2), fix)] * 4 + [
            pl.BlockSpec((1, hw), fix),
            pl.BlockSpec((blk, w), lambda b, i: (b * nq + i, 0)),
            pl.BlockSpec((seq // page, w, page), lambda b, i: (b, 0, 0)),
            pl.BlockSpec((seq, w), lambda b, i: (b, 0)),
            pl.BlockSpec((n_heads, 2, blk, blk), lambda b, i: (0, 0, 0, 0)),
        ],
        out_specs=pl.BlockSpec((blk, w), lambda b, i: (b * nq + i, 0)),
        out_shape=jax.ShapeDtypeStruct((t, w), BF16),
        scratch_shapes=[pltpu.VMEM((n_heads, 2 * blk, hw), BF16), pltpu.VMEM((n_heads, 2 * blk, LANES), F32),
                        pltpu.VMEM((n_heads, 2 * blk, LANES), F32), pltpu.VMEM((n_heads, 2 * blk, hw), F32)],
        compiler_params=_cparams(("parallel", "parallel")),
        name="attn_p",
    )(*lams, subln_g, q, ktb, vb, bias)


def _attn_s_kernel(pt_ref, lq1, lk1, lq2, lk2, g_ref, q_ref, kn_ref, vn_ref, b_ref, *rest,
                   pps, n_chunks, n_heads, hd, lambda_init):
    k_refs = rest[:pps]
    v_refs = rest[pps:2 * pps]
    o_ref = rest[2 * pps]
    m_ref, l_ref, acc_ref, kbf_ref, vbf_ref = rest[2 * pps + 1:]
    j = pl.program_id(1)
    last = n_chunks - 1
    dl = q_ref.shape[0]
    vd = 2 * hd
    hr = 2 * dl
    page = k_refs[0].shape[3]

    @pl.when(j == 0)
    def _():
        m_ref[...] = jnp.full(m_ref.shape, NEG_INF, F32)
        l_ref[...] = jnp.zeros(l_ref.shape, F32)
        acc_ref[...] = jnp.zeros(acc_ref.shape, F32)

    q8 = q_ref[...]
    qt = jnp.concatenate([q8] * (2 * n_heads), axis=0)
    lane = lax.broadcasted_iota(I32, qt.shape, 1)
    row = lax.broadcasted_iota(I32, qt.shape, 0)
    qbd = jnp.where(lane // hd == row // dl, qt, 0.0).astype(BF16)

    def update(s_list, v_heads):
        m_old = m_ref[...]
        m_new = m_old
        for s in s_list:
            m_new = jnp.maximum(m_new, jnp.max(s, axis=-1, keepdims=True))
        alpha = jnp.exp(m_old - m_new)
        l_new = alpha * l_ref[...]
        acc = alpha * acc_ref[...]
        parts = [acc[h * hr:(h + 1) * hr] for h in range(n_heads)]
        for s, v_of in zip(s_list, v_heads):
            p = jnp.exp(s - m_new)
            l_new = l_new + jnp.sum(p, axis=-1, keepdims=True)
            pb = p.astype(BF16)
            for h in range(n_heads):
                parts[h] = parts[h] + jnp.dot(pb[h * hr:(h + 1) * hr], v_of(h).astype(BF16),
                                              preferred_element_type=F32)
        m_ref[...] = m_new
        l_ref[...] = l_new
        acc_ref[...] = jnp.concatenate(parts, axis=0)

    for p in range(pps):
        kbf_ref[:, p * page:(p + 1) * page] = k_refs[p][0, 0].astype(BF16)
        for h in range(n_heads):
            vbf_ref[h, p * page:(p + 1) * page, :] = v_refs[p][0, 0, pl.ds(h, page, stride=n_heads), :].astype(BF16)
    s = jnp.dot(qbd, kbf_ref[...], preferred_element_type=F32)
    tail = s[:, (pps - 1) * page:] + jnp.where(j == last, b_ref[0], 0.0)
    s = jnp.concatenate([s[:, :(pps - 1) * page], tail], axis=1) if pps > 1 else tail
    update([s], [lambda h: vbf_ref[h]])

    @pl.when(j == last)
    def _():
        pad = jnp.zeros((page - dl, kn_ref.shape[1]), F32)
        kn = jnp.concatenate([kn_ref[...], pad], axis=0).astype(BF16)
        vpad = jnp.zeros((page - dl, vd), F32)
        s_new = lax.dot_general(qbd, kn, (((1,), (1,)), ((), ())), preferred_element_type=F32) + b_ref[1]
        update([s_new], [lambda h: jnp.concatenate([vn_ref[pl.ds(h, dl, stride=n_heads), :], vpad], axis=0)])
        lam = _lambda(lq1, lk1, lq2, lk2, lambda_init)
        o = acc_ref[...] / l_ref[...]
        for h in range(n_heads):
            oh = o[h * hr:h * hr + dl] - lam * o[h * hr + dl:(h + 1) * hr]
            o_ref[:, h * vd:(h + 1) * vd] = _rms(oh, g_ref[...]) * (1.0 - lambda_init)


def _attn_sample(page_table, lams, subln_g, q, k_new, v_new, bias, cache_kt, cache_v, layer, n_heads,
                 lambda_init):
    db, n_pages = page_table.shape
    t, w = q.shape
    dl = t // db
    page = cache_kt.shape[3]
    hd = w // (2 * n_heads)
    pps = math.gcd(PAGES_PER_STEP, n_pages)
    rows = 2 * n_heads * dl
    fix = lambda b, j, pt: (0, 0)
    seq = lambda b, j, pt: (b, 0)

    def page_spec(p, shape):
        return pl.BlockSpec((1, 1) + shape, lambda b, j, pt: (layer, pt[b, j * pps + p], 0, 0))

    grid_spec = pltpu.PrefetchScalarGridSpec(
        num_scalar_prefetch=1,
        grid=(db, n_pages // pps),
        in_specs=[pl.BlockSpec((1, hd), fix)] * 4 + [
            pl.BlockSpec((1, 2 * hd), fix),
            pl.BlockSpec((dl, w), seq), pl.BlockSpec((dl, w), seq), pl.BlockSpec((dl * n_heads, 2 * hd), seq),
            pl.BlockSpec((2, rows, page), lambda b, j, pt: (0, 0, 0)),
        ] + [page_spec(p, (w, page)) for p in range(pps)]
        + [page_spec(p, (page * n_heads, 2 * hd)) for p in range(pps)],
        out_specs=pl.BlockSpec((dl, w), seq),
        scratch_shapes=[pltpu.VMEM((rows, 1), F32), pltpu.VMEM((rows, 1), F32), pltpu.VMEM((rows, 2 * hd), F32),
                        pltpu.VMEM((w, pps * page), BF16), pltpu.VMEM((n_heads, pps * page, 2 * hd), BF16)],
    )
    return pl.pallas_call(
        functools.partial(_attn_s_kernel, pps=pps, n_chunks=n_pages // pps, n_heads=n_heads, hd=hd,
                          lambda_init=lambda_init),
        grid_spec=grid_spec,
        out_shape=jax.ShapeDtypeStruct((t, w), F32),
        compiler_params=_cparams(("parallel", "arbitrary")),
        name="attn_s",
    )(page_table, *lams, subln_g, q, k_new, v_new, bias, *([cache_kt] * pps), *([cache_v] * pps))


def _s5_prep_kernel(are_ref, aim_ref, ldt_ref, bre_ref, bim_ref, lb_ref, bb_ref, pwc_ref, *, seg):
    dt = jnp.exp(ldt_ref[...])
    a_re = jnp.minimum(are_ref[...], -1e-4)
    a_im = aim_ref[...]

    def lam_pow(n):
        mag = jnp.exp(a_re * dt * n)
        ang = a_im * dt * n
        return mag * jnp.cos(ang), mag * jnp.sin(ang)

    lb_re, lb_im = lam_pow(1.0)
    lb_ref[0] = lb_re
    lb_ref[1] = lb_im
    den = a_re * a_re + a_im * a_im
    nr = lb_re - 1.0
    co_re = (nr * a_re + lb_im * a_im) / den
    co_im = (lb_im * a_re - nr * a_im) / den
    bb_ref[0] = co_re * bre_ref[...] - co_im * bim_ref[...]
    bb_ref[1] = co_re * bim_ref[...] + co_im * bre_ref[...]
    n_dbl = pwc_ref.shape[1]
    for s in range(n_dbl):
        c_re, c_im = lam_pow(float(seg * (1 << s)))
        pwc_ref[0, s:s + 1, :] = c_re
        pwc_ref[1, s:s + 1, :] = c_im


def _s5_prep(a_re, a_im, log_dt, b_re, b_im, seg):
    g, p = a_re.shape
    c = b_re.shape[-1]
    gp = g * p
    flat = lambda a: a.reshape(1, gp)
    ldt = jnp.broadcast_to(log_dt[:, None], (g, p)).reshape(1, gp)
    chan = lambda b: jnp.transpose(b, (2, 0, 1)).reshape(c, gp)
    n_dbl = 3
    full = lambda shape: pl.BlockSpec(shape, lambda: (0,) * len(shape))
    return pl.pallas_call(
        functools.partial(_s5_prep_kernel, seg=seg),
        in_specs=[full((1, gp))] * 3 + [full((c, gp))] * 2,
        out_specs=[full((2, 1, gp)), full((2, c, gp)), full((2, n_dbl, gp))],
        out_shape=[jax.ShapeDtypeStruct((2, 1, gp), F32), jax.ShapeDtypeStruct((2, c, gp), F32),
                   jax.ShapeDtypeStruct((2, n_dbl, gp), F32)],
        compiler_params=pltpu.CompilerParams(vmem_limit_bytes=VMEM_LIMIT),
        name="s5_prep",
    )(flat(a_re), flat(a_im), ldt, chan(b_re), chan(b_im))


def _cmul(ar, ai, br, bi):
    return ar * br - ai * bi, ar * bi + ai * br


def _s5_kernel(u_ref, bb_ref, cw_ref, d_ref, lb_ref, pwc_ref, s0_ref, y_ref, st_ref, up_ref, h_ref, hb_ref,
               *, seg, stride, chain):
    cw = lb_ref.shape[2]
    n = SUBLANES
    unroll = math.gcd(seg // 2, 4)
    for i in range(seg):
        up_ref[i * n:(i + 1) * n, :] = u_ref[pl.ds(i, n, stride=stride), :]
    h_ref[...] = jnp.dot(up_ref[...].astype(BF16), bb_ref[0], preferred_element_type=F32)
    lr = jnp.broadcast_to(lb_ref[0], (n, cw))
    li = jnp.broadcast_to(lb_ref[1], (n, cw))

    rows2 = 2 * n

    def step(i, hr, hi):
        r = pl.multiple_of(i * n, n)
        tr, ti = _cmul(lr, li, hr, hi)
        return tr + h_ref[pl.ds(r, n), :cw], ti + h_ref[pl.ds(r, n), cw:]

    def scan(k, carry):
        hr0, hi0 = step(2 * k, *carry)
        hr1, hi1 = step(2 * k + 1, hr0, hi0)
        r = pl.multiple_of(k * rows2, rows2)
        if chain:
            h_ref[pl.ds(r, rows2), :cw] = jnp.concatenate([hr0, hr1], axis=0)
            h_ref[pl.ds(r, rows2), cw:] = jnp.concatenate([hi0, hi1], axis=0)
        else:
            hb_ref[pl.ds(r, rows2), :cw] = jnp.concatenate([hr0, hr1], axis=0).astype(BF16)
            hb_ref[pl.ds(r, rows2), cw:] = jnp.concatenate([hi0, hi1], axis=0).astype(BF16)
        return hr1, hi1

    end_r, end_i = lax.fori_loop(0, seg // 2, scan, (s0_ref[0, 0], s0_ref[0, 1]), unroll=unroll)

    if chain:
        sub = lax.broadcasted_iota(I32, (n, cw), 0)

        def shift(x, d):
            return jnp.where(sub >= d, pltpu.roll(x, d, axis=0), 0.0)

        tr, ti = end_r, end_i
        for s in range(pwc_ref.shape[1]):
            d = 1 << s
            pr = jnp.broadcast_to(pwc_ref[0, s:s + 1, :], (n, cw))
            pi = jnp.broadcast_to(pwc_ref[1, s:s + 1, :], (n, cw))
            ar, ai = _cmul(pr, pi, shift(tr, d), shift(ti, d))
            tr, ti = tr + ar, ti + ai
        end_r, end_i = tr, ti
        cr, ci = shift(tr, 1), shift(ti, 1)

        def fix(k, carry):
            r = pl.multiple_of(k * rows2, rows2)
            ar0, ai0 = _cmul(lr, li, *carry)
            ar1, ai1 = _cmul(lr, li, ar0, ai0)
            hb_ref[pl.ds(r, rows2), :cw] = (h_ref[pl.ds(r, rows2), :cw] + jnp.concatenate([ar0, ar1], axis=0)).astype(BF16)
            hb_ref[pl.ds(r, rows2), cw:] = (h_ref[pl.ds(r, rows2), cw:] + jnp.concatenate([ai0, ai1], axis=0)).astype(BF16)
            return ar1, ai1

        lax.fori_loop(0, seg // 2, fix, (cr, ci), unroll=unroll)

    st_ref[0, 0, 0] = end_r
    st_ref[0, 0, 1] = end_i
    y = jnp.dot(hb_ref[...], cw_ref[0], preferred_element_type=F32) + d_ref[...] * up_ref[...]
    up_ref[...] = jax.nn.gelu(y)
    for i in range(seg):
        y_ref[pl.ds(i, n, stride=stride), :] = up_ref[i * n:(i + 1) * n, :]


def _s5(u, bb_bd, cw_bd, d_vec, lb, pwc, s0, n_seq, seg, stride, chain):
    t, w = u.shape
    ncc, uw, cw2 = bb_bd.shape
    cw = cw2 // 2
    rows = SUBLANES * seg
    return pl.pallas_call(
        functools.partial(_s5_kernel, seg=seg, stride=stride, chain=chain),
        grid=(n_seq, ncc),
        in_specs=[
            pl.BlockSpec((rows, uw), lambda b, c: (b, c)),
            pl.BlockSpec((1, uw, cw2), lambda b, c: (c, 0, 0)),
            pl.BlockSpec((1, cw2, uw), lambda b, c: (c, 0, 0)),
            pl.BlockSpec((1, uw), lambda b, c: (0, c)),
            pl.BlockSpec((2, 1, cw), lambda b, c: (0, 0, c)),
            pl.BlockSpec((2, pwc.shape[1], cw), lambda b, c: (0, 0, c)),
            pl.BlockSpec((1, 2, SUBLANES, cw), lambda b, c: (b, 0, 0, c)),
        ],
        out_specs=[
            pl.BlockSpec((rows, uw), lambda b, c: (b, c)),
            pl.BlockSpec((1, 1, 2, SUBLANES, cw), lambda b, c: (b, c, 0, 0, 0)),
        ],
        out_shape=[jax.ShapeDtypeStruct((t, w), F32),
                   jax.ShapeDtypeStruct((n_seq, ncc, 2, SUBLANES, cw), F32)],
        scratch_shapes=[pltpu.VMEM((rows, uw), F32), pltpu.VMEM((rows, cw2), F32), pltpu.VMEM((rows, cw2), BF16)],
        compiler_params=_cparams(("parallel", "parallel")),
        name="s5_chain" if chain else "s5_step",
    )(u, bb_bd, cw_bd, d_vec, lb, pwc, s0)


def _block_diag_weights(bb, c_re, c_im, cw):
    _, c, gp = bb.shape
    g = c_re.shape[0]
    p = gp // g
    gpc = cw // p
    ncc = g // gpc
    eye = jnp.eye(gpc, dtype=F32)
    b4 = jnp.transpose(bb.reshape(2, c, ncc, gpc, p), (2, 0, 3, 1, 4))
    bbd = b4[:, :, :, :, None, :] * eye[None, None, :, None, :, None]
    bbd = jnp.transpose(bbd.reshape(ncc, 2, gpc * c, cw), (0, 2, 1, 3)).reshape(ncc, gpc * c, 2 * cw)
    cc = jnp.stack([c_re, -c_im]).reshape(2, ncc, gpc, c, p)
    c4 = jnp.transpose(cc, (1, 0, 2, 4, 3))
    cbd = c4[:, :, :, :, None, :] * eye[None, None, :, None, :, None]
    cbd = cbd.reshape(ncc, 2 * cw, gpc * c)
    return bbd.astype(BF16), cbd.astype(BF16)


def _merge_kernel(x_ref, oa_ref, yg_ref, gm_ref, wg_ref, wglu_ref, bglu_ref, wa_ref, ws_ref, wo_ref, gf_ref,
                  wr_ref, br_ref, h_ref, hn_ref, ti_ref, tp_ref, lr_ref, cnt_ref, *, rt):
    x = x_ref[...]
    tm, d = x.shape
    dot = functools.partial(jnp.dot, preferred_element_type=F32)
    xn = _rms(x, gm_ref[...]).astype(BF16)
    gates = jax.nn.sigmoid(dot(xn, wg_ref[...]))
    yg = yg_ref[...]
    ys = yg * jax.nn.sigmoid(dot(yg.astype(BF16), wglu_ref[...]) + bglu_ref[...])
    merged = (gates[:, :d] * dot(oa_ref[...].astype(BF16), wa_ref[...])
              + gates[:, d:] * dot(ys.astype(BF16), ws_ref[...]))
    h = x + dot(merged.astype(BF16), wo_ref[...])
    h_ref[...] = h
    hn = _rms(h, gf_ref[...]).astype(BF16)
    hn_ref[...] = hn
    logit = lax.dot_general(wr_ref[...], hn, (((1,), (1,)), ((), ())), preferred_element_type=F32) + br_ref[...]
    n_e = logit.shape[0]
    eid = lax.broadcasted_iota(I32, logit.shape, 0)
    vals, idxs = [], []
    for _ in range(TOP_K):
        mx = jnp.max(logit, axis=0, keepdims=True)
        ix = jnp.min(jnp.where(logit == mx, eid, n_e), axis=0, keepdims=True)
        vals.append(mx)
        idxs.append(ix)
        logit = jnp.where(eid == ix, -jnp.inf, logit)
    ex = [jnp.exp(v - vals[0]) for v in vals]
    tot = ex[0]
    for e in ex[1:]:
        tot = tot + e
    ti_ref[...] = jnp.concatenate(idxs, axis=0)
    tp_ref[...] = jnp.concatenate(ex, axis=0) / tot
    onehot = [jnp.where(eid == ix, 1.0, 0.0) for ix in idxs]
    before = (lax.broadcasted_iota(I32, (rt, rt), 0) < lax.broadcasted_iota(I32, (rt, rt), 1))
    before = jnp.where(before, 1.0, 0.0).astype(BF16)
    rank_tiles = []
    for sub in range(tm // rt):
        oh = [o[:, sub * rt:(sub + 1) * rt] for o in onehot]
        prefix = dot(jnp.concatenate(oh, axis=0).astype(BF16), before)
        base = jnp.zeros((n_e, 1), F32)
        ranks = []
        for k in range(TOP_K):
            ranks.append(jnp.sum(oh[k] * (prefix[k * n_e:(k + 1) * n_e] + base), axis=0, keepdims=True))
            base = base + jnp.sum(oh[k], axis=1, keepdims=True)
        rank_tiles.append(jnp.concatenate(ranks, axis=0))
        cnt_ref[sub] = base.astype(I32)
    lr_ref[...] = jnp.concatenate(rank_tiles, axis=1).astype(I32)


def _merge(x, oa, yg, w, tm, rt):
    t, d = x.shape
    aw = oa.shape[1]
    sw = yg.shape[1]
    n_e = w["wr"].shape[0]
    row = lambda i: (i, 0)
    col = lambda i: (0, i)
    fix = lambda i: (0, 0)
    full = lambda a: pl.BlockSpec(a.shape, fix)
    names = ["gm", "wg", "wglu", "bglu", "wa", "ws", "wo", "gf", "wr", "br"]
    kt = pl.BlockSpec((TOP_K, tm), col)
    return pl.pallas_call(
        functools.partial(_merge_kernel, rt=rt),
        grid=(t // tm,),
        in_specs=[pl.BlockSpec((tm, d), row), pl.BlockSpec((tm, aw), row), pl.BlockSpec((tm, sw), row)]
        + [full(w[n]) for n in names],
        out_specs=[pl.BlockSpec((tm, d), row), pl.BlockSpec((tm, d), row), kt, kt, kt,
                   pl.BlockSpec((tm // rt, n_e, 1), lambda i: (i, 0, 0))],
        out_shape=[jax.ShapeDtypeStruct((t, d), F32), jax.ShapeDtypeStruct((t, d), BF16),
                   jax.ShapeDtypeStruct((TOP_K, t), I32), jax.ShapeDtypeStruct((TOP_K, t), F32),
                   jax.ShapeDtypeStruct((TOP_K, t), I32), jax.ShapeDtypeStruct((t // rt, n_e, 1), I32)],
        compiler_params=_cparams(("parallel",)),
        name="merge",
    )(x, oa, yg, *[w[n] for n in names])


def _chunks_per_tile(tm, n_experts):
    rows = TOP_K * tm + (SUBLANES - 1) * n_experts
    return -(-rows // (SUBLANES * SUBLANES)) * SUBLANES


CHUNK_GROUP = 8


def _route_plan(cnt, moe_tile):
    n_e = cnt.shape[1]
    cnt8 = (cnt + SUBLANES - 1) // SUBLANES * SUBLANES
    total = jnp.sum(cnt8, axis=0)
    padded = (total + moe_tile - 1) // moe_tile * moe_tile
    pad_end = jnp.cumsum(padded)
    tile_base = (pad_end - padded)[None, :] + jnp.cumsum(cnt8, axis=0) - cnt8
    off_end = jnp.cumsum(cnt8, axis=1)
    off8 = off_end - cnt8
    nch = off_end[:, -1] // SUBLANES
    ngrp = (nch + CHUNK_GROUP - 1) // CHUNK_GROUP
    return dict(padded=padded.astype(I32), pad_end=pad_end.astype(I32), tile_base=tile_base, off8=off8,
                off_end=off_end, nch=nch.astype(I32), ngrp=ngrp.astype(I32), n_e=n_e)


def _chunk_rows(plan, lo, hi, g_max, spare_row):
    off_end = plan["off_end"][lo:hi]
    delta = (plan["tile_base"] - plan["off8"])[lo:hi]
    row0 = jnp.arange(g_max, dtype=I32) * SUBLANES
    grp = jnp.sum((off_end[:, None, :] <= row0[None, :, None]).astype(I32), axis=-1)
    grp = jnp.minimum(grp, plan["n_e"] - 1)
    sel = grp[:, :, None] == jnp.arange(plan["n_e"], dtype=I32)[None, None, :]
    dst = row0[None, :] + jnp.sum(jnp.where(sel, delta[:, None, :], 0), axis=-1)
    g = jnp.arange(g_max, dtype=I32)[None, :]
    nch = plan["nch"][lo:hi, None]
    parity = (jnp.arange(lo, hi, dtype=I32) % 2)[:, None]
    spare = spare_row + (parity * CHUNK_GROUP + g - nch) * SUBLANES
    return jnp.where(g < nch, dst, spare).astype(I32)[:, None, :]


def _dispatch_kernel(nch_ref, pe_ref, pd_ref, dst_ref, hnp_ref, hns_ref, ix_ref, lr_ref, pr_ref, off_ref, pos_ref,
                     xs_hbm, xbuf, zbuf, sems, zsem, *, n, n_p, moe_tile):
    i = pl.program_id(0)
    tm = hnp_ref.shape[0]
    rows = xbuf.shape[1]
    n_e = off_ref.shape[1]
    n_moe_tiles = xs_hbm.shape[0] // moe_tile
    slot = lax.rem(i, 2)

    def chunk(s, g, row):
        r = pl.multiple_of(row, SUBLANES)
        src = xbuf.at[s, pl.ds(pl.multiple_of(g * SUBLANES, SUBLANES), SUBLANES), :]
        return pltpu.make_async_copy(src, xs_hbm.at[pl.ds(r, SUBLANES), :], sems.at[s])

    def drain(s, groups):
        grp_rows = CHUNK_GROUP * SUBLANES
        for bit in range((rows // grp_rows).bit_length()):
            span = grp_rows << bit
            if span > rows:
                continue

            @pl.when((groups >> bit) & 1 == 1)
            def _():
                pltpu.make_async_copy(xbuf.at[s, pl.ds(0, span), :], xs_hbm.at[pl.ds(0, span), :], sems.at[s]).wait()

    @pl.when(i == 0)
    def _():
        zbuf[...] = jnp.zeros(zbuf.shape, zbuf.dtype)
        n_used = pe_ref[n_e - 1] // moe_tile

        def zero_tile(row):
            r = pl.multiple_of(row, SUBLANES)
            return pltpu.make_async_copy(zbuf, xs_hbm.at[pl.ds(r, moe_tile), :], zsem.at[0])

        for e in range(n_e):
            @pl.when(pd_ref[e] > 0)
            def _():
                zero_tile(jnp.maximum(pe_ref[e] - moe_tile, 0)).start()

        def start_rest(t, c):
            zero_tile(t * moe_tile).start()
            return c
        lax.fori_loop(n_used, n_moe_tiles, start_rest, 0)
        for e in range(n_e):
            @pl.when(pd_ref[e] > 0)
            def _():
                zero_tile(0).wait()

        def wait_rest(t, c):
            zero_tile(0).wait()
            return c
        lax.fori_loop(n_used, n_moe_tiles, wait_rest, 0)

    @pl.when(i >= 2)
    def _():
        drain(slot, nch_ref[jnp.maximum(i - 2, 0)])

    hn = jnp.where(i < n_p, hnp_ref[...], hns_ref[...])
    eid = lax.broadcasted_iota(I32, (n_e, tm), 0)
    off = off_ref[0].astype(F32)
    r_iota = lax.broadcasted_iota(I32, (rows, tm), 0)
    prob = jnp.zeros((rows, tm), F32)
    pos_rows = []
    for k in range(TOP_K):
        onehot = jnp.where(eid == ix_ref[k:k + 1, :], 1.0, 0.0)
        pos = jnp.sum(onehot * off, axis=0, keepdims=True).astype(I32) + lr_ref[k:k + 1, :]
        pos_rows.append(pos)
        prob = jnp.where(r_iota == pos, pr_ref[k:k + 1, :], prob)
    sel = jnp.where(prob != 0.0, 1.0, 0.0)
    pos_ref[...] = jnp.concatenate(pos_rows, axis=0)
    d = hn.shape[1]
    xbuf[slot, :, :d] = jnp.dot(sel.astype(BF16), hn, preferred_element_type=F32)
    xbuf[slot, :, d:] = jnp.broadcast_to(jnp.sum(prob, axis=1, keepdims=True), (rows, LANES))

    def send(grp, c):
        for u in range(CHUNK_GROUP):
            g = grp * CHUNK_GROUP + u
            chunk(slot, g, dst_ref[0, 0, g]).start()
        return c
    lax.fori_loop(0, nch_ref[i], send, 0)

    @pl.when(i == n - 1)
    def _():
        drain(slot, nch_ref[i])
        if n >= 2:
            drain(1 - slot, nch_ref[jnp.maximum(i - 1, 0)])


def _dispatch(plan, dst, tm, hn_p, hn_s, top_i, lrank, probs, s_pad, moe_tile):
    d = hn_p.shape[1]
    n_p, n_s = hn_p.shape[0] // tm, hn_s.shape[0] // tm
    n = n_p + n_s
    n_e = plan["n_e"]
    g_max = _chunks_per_tile(tm, n_e)
    rows = g_max * SUBLANES
    off = plan["off8"].astype(I32)[:, :, None]
    kt = pl.BlockSpec((TOP_K, tm), lambda i, *_: (0, i))
    grid_spec = pltpu.PrefetchScalarGridSpec(
        num_scalar_prefetch=3,
        grid=(n,),
        in_specs=[
            pl.BlockSpec((1, 1, g_max), lambda i, *_: (i, 0, 0), memory_space=pltpu.SMEM),
            pl.BlockSpec((tm, d), lambda i, *_: (jnp.minimum(i, n_p - 1), 0)),
            pl.BlockSpec((tm, d), lambda i, *_: (jnp.maximum(i - n_p, 0), 0)),
            kt, kt, kt,
            pl.BlockSpec((1, n_e, 1), lambda i, *_: (i, 0, 0)),
        ],
        out_specs=[kt, pl.BlockSpec(memory_space=pl.ANY)],
        scratch_shapes=[pltpu.VMEM((2, rows, d + LANES), F32), pltpu.VMEM((moe_tile, d + LANES), F32),
                        pltpu.SemaphoreType.DMA((2,)), pltpu.SemaphoreType.DMA((1,))])
    return pl.pallas_call(
        functools.partial(_dispatch_kernel, n=n, n_p=n_p, moe_tile=moe_tile),
        grid_spec=grid_spec,
        out_shape=[jax.ShapeDtypeStruct((TOP_K, n * tm), I32), jax.ShapeDtypeStruct((s_pad, d + LANES), F32)],
        compiler_params=_cparams(("arbitrary",)),
        name="dispatch",
    )(plan["ngrp"], plan["pad_end"], plan["padded"], dst, hn_p, hn_s, top_i, lrank, probs, off)


def _moe_kernel(te_ref, nu_ref, x_ref, wg_ref, bg_ref, wu_ref, bu_ref, wd_ref, bd_ref, o_ref, wgb, wub, wdb):
    i = pl.program_id(0)
    used = i < nu_ref[0]
    e = te_ref[i]
    prev = te_ref[jnp.maximum(i - 1, 0)]

    @pl.when(used & ((i == 0) | (e != prev)))
    def _():
        wgb[...] = wg_ref[0].astype(BF16)
        wub[...] = wu_ref[0].astype(BF16)
        wdb[...] = wd_ref[0].astype(BF16)

    @pl.when(used)
    def _():
        d = o_ref.shape[1]
        x = x_ref[:, :d].astype(BF16)
        gt = jnp.dot(x, wgb[...], preferred_element_type=F32) + bg_ref[0]
        up = jnp.dot(x, wub[...], preferred_element_type=F32) + bu_ref[0]
        gt = jnp.minimum(gt, SWIGLU_LIMIT)
        up = jnp.clip(up, -SWIGLU_LIMIT, SWIGLU_LIMIT)
        hh = (up + 1.0) * (gt * jax.nn.sigmoid(SWIGLU_ALPHA * gt))
        o_ref[...] = (jnp.dot(hh.astype(BF16), wdb[...], preferred_element_type=F32) + bd_ref[0]) * x_ref[:, d:d + 1]

    @pl.when(jnp.logical_not(used))
    def _():
        o_ref[...] = jnp.zeros(o_ref.shape, F32)


def _moe(tile_expert, n_used, xs, wg, bg, wu, bu, wd, bd, tm):
    n_e, d, f = wg.shape
    s_pad = xs.shape[0]
    wmap = lambda i, te, nu: (te[i], 0, 0)
    row = lambda i, te, nu: (i, 0)
    grid_spec = pltpu.PrefetchScalarGridSpec(
        num_scalar_prefetch=2,
        grid=(s_pad // tm,),
        in_specs=[
            pl.BlockSpec((tm, d + LANES), lambda i, te, nu: (jnp.minimum(i, jnp.maximum(nu[0] - 1, 0)), 0)),
            pl.BlockSpec((1, d, f), wmap), pl.BlockSpec((1, 1, f), wmap),
            pl.BlockSpec((1, d, f), wmap), pl.BlockSpec((1, 1, f), wmap),
            pl.BlockSpec((1, f, d), wmap), pl.BlockSpec((1, 1, d), wmap),
        ],
        out_specs=pl.BlockSpec((tm, d), row),
        scratch_shapes=[pltpu.VMEM((d, f), BF16), pltpu.VMEM((d, f), BF16), pltpu.VMEM((f, d), BF16)],
    )
    return pl.pallas_call(
        _moe_kernel,
        grid_spec=grid_spec,
        out_shape=jax.ShapeDtypeStruct((s_pad, d), F32),
        compiler_params=pltpu.CompilerParams(dimension_semantics=("arbitrary",),
                                             vmem_limit_bytes=56 * 1024 * 1024),
        name="moe",
    )(tile_expert, n_used, xs, wg, bg.reshape(n_e, 1, f), wu, bu.reshape(n_e, 1, f), wd, bd.reshape(n_e, 1, d))


def _combine_kernel(nch_ref, dst_ref, nxt_ref, h_ref, pos_ref, g_ref, o_hbm, y_ref, obuf, sems, *, n):
    i = pl.program_id(0)
    tm = h_ref.shape[0]
    rows = obuf.shape[1]
    slot = lax.rem(i, 2)

    def chunk(s, g, row):
        r = pl.multiple_of(row, SUBLANES)
        dst = obuf.at[s, pl.ds(pl.multiple_of(g * SUBLANES, SUBLANES), SUBLANES), :]
        return pltpu.make_async_copy(o_hbm.at[pl.ds(r, SUBLANES), :], dst, sems.at[s])

    def fetch(s, rows_ref, groups):
        def body(grp, c):
            for u in range(CHUNK_GROUP):
                g = grp * CHUNK_GROUP + u
                chunk(s, g, rows_ref[0, 0, g]).start()
            return c
        lax.fori_loop(0, groups, body, 0)

    @pl.when(i == 0)
    def _():
        obuf[...] = jnp.zeros(obuf.shape, obuf.dtype)
        fetch(0, dst_ref, nch_ref[0])

    @pl.when(i + 1 < n)
    def _():
        fetch(1 - slot, nxt_ref, nch_ref[jnp.minimum(i + 1, n - 1)])

    count = nch_ref[i]
    grp_rows = CHUNK_GROUP * SUBLANES
    for bit in range((rows // grp_rows).bit_length()):
        span = grp_rows << bit
        if span > rows:
            continue

        @pl.when((count >> bit) & 1 == 1)
        def _():
            pltpu.make_async_copy(o_hbm.at[pl.ds(0, span), :], obuf.at[slot, pl.ds(0, span), :], sems.at[slot]).wait()

    ob = obuf[slot].astype(BF16)
    lane = lax.broadcasted_iota(I32, (tm, rows), 1)
    w = jnp.zeros((tm, rows), F32)
    for k in range(TOP_K):
        w = w + jnp.where(lane == pos_ref[:, k:k + 1], 1.0, 0.0)
    moe = jnp.dot(w.astype(BF16), ob, preferred_element_type=F32)
    y_ref[...] = _rms(h_ref[...] + moe, g_ref[...])


def _combine(plan, dst_all, lo, tm, h, pos_t, g, o_sorted):
    t, d = h.shape
    n = t // tm
    g_max = _chunks_per_tile(tm, plan["n_e"])
    rows = g_max * SUBLANES
    dst = dst_all[lo:lo + n]
    tok = lambda i, *_: (i, 0)
    grid_spec = pltpu.PrefetchScalarGridSpec(
        num_scalar_prefetch=1,
        grid=(n,),
        in_specs=[
            pl.BlockSpec((1, 1, g_max), lambda i, *_: (i, 0, 0), memory_space=pltpu.SMEM),
            pl.BlockSpec((1, 1, g_max), lambda i, *_: (jnp.minimum(i + 1, n - 1), 0, 0), memory_space=pltpu.SMEM),
            pl.BlockSpec((tm, d), tok), pl.BlockSpec((tm, TOP_K), tok),
            pl.BlockSpec((1, d), lambda i, *_: (0, 0)),
            pl.BlockSpec(memory_space=pl.ANY),
        ],
        out_specs=pl.BlockSpec((tm, d), tok),
        scratch_shapes=[pltpu.VMEM((2, rows, d), F32), pltpu.SemaphoreType.DMA((2,))],
    )
    return pl.pallas_call(
        functools.partial(_combine_kernel, n=n),
        grid_spec=grid_spec,
        out_shape=jax.ShapeDtypeStruct((t, d), F32),
        compiler_params=_cparams(("arbitrary",)),
        name="combine",
    )(plan["ngrp"][lo:lo + n], dst, dst, h, pos_t, g, o_sorted)


def kernel(x_prompt, x_sample, cache_k, cache_v, state_ssm_re, state_ssm_im, page_table, norm_mix_g, w_in, lambda_q1, lambda_k1, lambda_q2, lambda_k2, subln_g, rel_bias, ssm_a_re, ssm_a_im, ssm_log_dt, ssm_b_re, ssm_b_im, ssm_c_re, ssm_c_im, ssm_d, w_glu, b_glu, w_attn_up, w_ssm_up, w_out, norm_ffn_g, w_router, b_router, w_e_gate, b_e_gate, w_e_up, b_e_up, w_e_down, b_e_down, norm_final_g):
    bp, seq, d = x_prompt.shape
    db, dl, _ = x_sample.shape
    depth, n_pool, page, n_heads, _, hd = cache_k.shape
    vd = cache_v.shape[-1]
    n_groups, n_state = ssm_a_re.shape[1:]
    gch = ssm_b_re.shape[-1]
    n_experts = w_router.shape[-1]
    qk_w = n_heads * 2 * hd
    attn_w = n_heads * vd
    ssm_w = n_groups * gch
    mix_w = 2 * qk_w + attn_w + ssm_w
    assert qk_w == attn_w == ssm_w and vd == 2 * hd
    assert seq % SUBLANES == 0 and db % SUBLANES == 0 and dl == SUBLANES
    tp, ts = bp * seq, db * dl
    scale = hd ** -0.5
    seg = seq // SUBLANES
    cw = min(S5_CHUNK, n_groups * n_state)

    ckt = jnp.transpose(cache_k, (0, 1, 3, 4, 5, 2)).reshape(depth, n_pool, qk_w, page)
    cv = cache_v.reshape(depth, n_pool, page * n_heads, vd)

    blk = math.gcd(ATTN_BLOCK, seq)
    qi = np.arange(blk)[:, None]
    ki = np.arange(blk)[None, :]
    bkt_p = np.stack([_bucket_tile(qi - ki), _bucket_tile(blk + qi - ki)])
    sq = np.tile(np.arange(dl), 2)[:, None]
    sk = np.arange(page)[None, :]
    new_dist = np.where(sk < dl, sq - sk, -1)
    bkt_s = np.stack([_bucket_tile(page + sq - sk), _bucket_tile(new_dist)])

    hp = x_prompt.reshape(tp, d)
    hs = x_sample.reshape(ts, d)
    outs = {n: [] for n in ("kp", "vp", "ks", "vs", "rp", "ip", "rs", "is")}
    tm_p = math.gcd(TOKEN_TILE, tp)
    tm_s = math.gcd(TOKEN_TILE, ts)
    for l in range(depth):
        lambda_init = 0.8 - 0.6 * math.exp(-0.3 * l)
        w_l = w_in[l].astype(BF16)
        g_mix = norm_mix_g[l].reshape(1, d)
        lams = [a[l].reshape(1, hd) for a in (lambda_q1, lambda_k1, lambda_q2, lambda_k2)]
        sub_g = subln_g[l].reshape(1, vd)

        w_qvu = jnp.concatenate([w_l[:, :qk_w], w_l[:, 2 * qk_w:mix_w]], axis=1)
        w_kt = jnp.transpose(w_l[:, qk_w:2 * qk_w])
        qp, ktp, vp, up, ktpb, vpb = _in_proj(hp, g_mix, w_qvu, w_kt, scale, tm_p, page, n_heads)
        qs, kts, vs, us, _, _ = _in_proj(hs, g_mix, w_qvu, w_kt, scale, tm_s, None, n_heads)
        ks = jnp.transpose(kts)

        bias_p = _rel_bias_tiles(rel_bias, bkt_p)
        bias_s = _rel_bias_tiles(rel_bias, bkt_s)
        bias_s = jnp.transpose(bias_s, (1, 0, 2, 3)).reshape(2, n_heads * 2 * dl, page)

        oa_p = _attn_prompt(lams, sub_g, qp, ktpb, vpb, bias_p, bp, seq, n_heads, lambda_init)
        oa_s = _attn_sample(page_table, lams, sub_g, qs, ks, vs, bias_s, ckt, cv, l, n_heads, lambda_init)

        lb, bb, pwc = _s5_prep(ssm_a_re[l], ssm_a_im[l], ssm_log_dt[l], ssm_b_re[l], ssm_b_im[l], seg)
        bb_bd, cw_bd = _block_diag_weights(bb, ssm_c_re[l], ssm_c_im[l], cw)
        d_vec = ssm_d[l].reshape(1, ssm_w)
        zeros = jnp.zeros((bp, 2, SUBLANES, n_groups * n_state), F32)
        yg_p, st_p = _s5(up, bb_bd, cw_bd, d_vec, lb, pwc, zeros, bp, seg, seg, True)
        s0 = jnp.stack([state_ssm_re[l], state_ssm_im[l]], axis=1).reshape(
            db // SUBLANES, SUBLANES, 2, n_groups * n_state)
        s0 = jnp.transpose(s0, (0, 2, 1, 3))
        yg_s, st_s = _s5(us, bb_bd, cw_bd, d_vec, lb, pwc, s0, db // SUBLANES, dl, dl, False)

        wts = dict(
            gm=g_mix, wg=w_l[:, mix_w:], wglu=w_glu[l].astype(BF16), bglu=b_glu[l].reshape(1, ssm_w),
            wa=w_attn_up[l].astype(BF16), ws=w_ssm_up[l].astype(BF16), wo=w_out[l].astype(BF16),
            gf=norm_ffn_g[l].reshape(1, d), wr=jnp.transpose(w_router[l]).astype(BF16),
            br=b_router[l].reshape(n_experts, 1))
        rt = math.gcd(ROUTE_TILE, math.gcd(tp, ts))
        h_p, hn_p, ti_p, pr_p, lr_p, cnt_p = _merge(hp, oa_p, yg_p, wts, math.gcd(MERGE_TILE, tp), rt)
        h_s, hn_s, ti_s, pr_s, lr_s, cnt_s = _merge(hs, oa_s, yg_s, wts, math.gcd(MERGE_TILE, ts), rt)

        n_p, n_s = tp // rt, ts // rt
        plan = _route_plan(jnp.concatenate([cnt_p[:, :, 0], cnt_s[:, :, 0]], axis=0), MOE_TILE)
        worst = TOP_K * (tp + ts) + (SUBLANES - 1) * n_experts * (n_p + n_s) + n_experts * (MOE_TILE - 1)
        s_pad = (-(-worst // MOE_TILE) + 1) * MOE_TILE
        dst = _chunk_rows(plan, 0, n_p + n_s, _chunks_per_tile(rt, n_experts), s_pad - MOE_TILE)
        pos, xs = _dispatch(plan, dst, rt, hn_p, hn_s, jnp.concatenate([ti_p, ti_s], axis=1),
                            jnp.concatenate([lr_p, lr_s], axis=1), jnp.concatenate([pr_p, pr_s], axis=1),
                            s_pad, MOE_TILE)
        tile_start = jnp.arange(s_pad // MOE_TILE, dtype=I32) * MOE_TILE
        tile_expert = jnp.minimum(jnp.sum((plan["pad_end"][None, :] <= tile_start[:, None]).astype(I32), axis=1),
                                  n_experts - 1)
        n_used = (plan["pad_end"][-1:] // MOE_TILE).astype(I32)
        o_sorted = _moe(tile_expert, n_used, xs, w_e_gate[l], b_e_gate[l], w_e_up[l], b_e_up[l],
                        w_e_down[l], b_e_down[l], MOE_TILE)

        last = l == depth - 1
        g_out = norm_final_g.reshape(1, d) if last else None
        assert last, "multi-layer stacks need a combine variant without the final norm"
        hp = _combine(plan, dst, 0, rt, h_p, jnp.transpose(pos[:, :tp]), g_out, o_sorted)
        hs = _combine(plan, dst, n_p, rt, h_s, jnp.transpose(pos[:, tp:]), g_out, o_sorted)

        gp = n_groups * n_state
        st_p = jnp.transpose(st_p, (0, 2, 3, 1, 4)).reshape(bp, 2, SUBLANES, gp)[:, :, SUBLANES - 1]
        st_s = jnp.transpose(st_s, (0, 3, 2, 1, 4)).reshape(db, 2, gp)
        outs["kp"].append(jnp.transpose(ktp.reshape(bp, seq // page, n_heads, 2, hd, page), (0, 1, 5, 2, 3, 4)))
        outs["vp"].append(vp.reshape(bp, seq // page, page, n_heads, vd))
        outs["ks"].append(ks.reshape(db, dl, n_heads, 2, hd))
        outs["vs"].append(vs.reshape(db, dl, n_heads, vd))
        outs["rp"].append(st_p[:, 0].reshape(bp, n_groups, n_state))
        outs["ip"].append(st_p[:, 1].reshape(bp, n_groups, n_state))
        outs["rs"].append(st_s[:, 0].reshape(db, n_groups, n_state))
        outs["is"].append(st_s[:, 1].reshape(db, n_groups, n_state))

    st = lambda n: jnp.stack(outs[n])
    return (hp.reshape(bp, seq, d), hs.reshape(db, dl, d), st("kp"), st("vp"), st("ks"), st("vs"),
            st("rp"), st("ip"), st("rs"), st("is"))
```

```python
import functools
import math

import numpy as np
import jax
import jax.numpy as jnp
from jax import lax
from jax.experimental import pallas as pl
from jax.experimental.pallas import tpu as pltpu

F32 = jnp.float32
BF16 = jnp.bfloat16
I32 = jnp.int32

EPS = 1e-6
NEG_INF = -1e30
TOP_K = 4
N_BUCKETS = 32
MAX_EXACT = N_BUCKETS // 2
MAX_DISTANCE = 128
SWIGLU_LIMIT = 7.0
SWIGLU_ALPHA = 1.702

SUBLANES = 8
LANES = 128
VMEM_LIMIT = 48 * 1024 * 1024

ATTN_BLOCK = 512
PAGES_PER_STEP = 16
TOKEN_TILE = 1024
MOE_TILE = 512
MERGE_TILE = 1024
ROUTE_TILE = 256
S5_CHUNK = 512


def _cparams(sem):
    return pltpu.CompilerParams(dimension_semantics=sem, vmem_limit_bytes=VMEM_LIMIT)


def _rms(x, g):
    return x * lax.rsqrt(jnp.mean(x * x, axis=-1, keepdims=True) + EPS) * g


def _in_proj_kernel(x_ref, g_ref, w_ref, wkt_ref, q_ref, k_ref, v_ref, u_ref, kb_ref, vb_ref, *, scale, page, n_heads):
    xn = _rms(x_ref[...], g_ref[...]).astype(BF16)
    p = jnp.dot(xn, w_ref[...], preferred_element_type=F32)
    w3 = p.shape[1] // 3
    tm = p.shape[0]
    vd = w3 // n_heads
    q_ref[...] = (p[:, :w3] * scale).astype(q_ref.dtype)
    v = p[:, w3:2 * w3]
    for h in range(n_heads):
        v_ref[pl.ds(h, tm, stride=n_heads), :] = v[:, h * vd:(h + 1) * vd]
    vb_ref[...] = v.astype(BF16)
    u_ref[...] = p[:, 2 * w3:]
    kt = lax.dot_general(wkt_ref[...], xn, (((1,), (1,)), ((), ())), preferred_element_type=F32)
    if page is None:
        k_ref[...] = kt
        kb_ref[...] = kt.astype(BF16)
    else:
        for pg in range(kt.shape[1] // page):
            tile = kt[:, pg * page:(pg + 1) * page]
            k_ref[pg] = tile
            kb_ref[pg] = tile.astype(BF16)


def _in_proj(x, g, w_qvu, w_kt, scale, tm, page, n_heads):
    t, d = x.shape
    w3 = w_qvu.shape[1] // 3
    row = lambda i: (i, 0)
    fix = lambda i: (0, 0)
    f32o = jax.ShapeDtypeStruct((t, w3), F32)
    if page is None:
        kt_shape, kt_spec = (w3, t), pl.BlockSpec((w3, tm), lambda i: (0, i))
    else:
        kt_shape, kt_spec = (t // page, w3, page), pl.BlockSpec((tm // page, w3, page), lambda i: (i, 0, 0))
    tok = pl.BlockSpec((tm, w3), row)
    return pl.pallas_call(
        functools.partial(_in_proj_kernel, scale=scale, page=page, n_heads=n_heads),
        grid=(t // tm,),
        in_specs=[pl.BlockSpec((tm, d), row), pl.BlockSpec((1, d), fix), pl.BlockSpec((d, 3 * w3), fix),
                  pl.BlockSpec((w3, d), fix)],
        out_specs=[tok, kt_spec, pl.BlockSpec((tm * n_heads, w3 // n_heads), row), tok, kt_spec, tok],
        out_shape=[jax.ShapeDtypeStruct((t, w3), F32 if page is None else BF16), jax.ShapeDtypeStruct(kt_shape, F32),
                   jax.ShapeDtypeStruct((t * n_heads, w3 // n_heads), F32), f32o,
                   jax.ShapeDtypeStruct(kt_shape, BF16), jax.ShapeDtypeStruct((t, w3), BF16)],
        compiler_params=_cparams(("parallel",)),
        name="in_proj",
    )(x, g, w_qvu, w_kt)


def _bucket_np(n):
    nf = np.maximum(n, 1).astype(np.float64)
    large = MAX_EXACT + np.trunc(
        np.log(nf / MAX_EXACT) / math.log(MAX_DISTANCE / MAX_EXACT) * (N_BUCKETS - MAX_EXACT)).astype(np.int64)
    return np.where(n < MAX_EXACT, n, np.minimum(large, N_BUCKETS - 1)).astype(np.int32)


def _bucket_tile(dist):
    return np.where(dist < 0, -1, _bucket_np(np.maximum(dist, 0))).astype(np.int32)


def _rel_bias_kernel(rb_ref, bkt_ref, o_ref, *, live, masked):
    h = pl.program_id(0)
    far = rb_ref[N_BUCKETS - 1, h]
    o_ref[0] = jnp.zeros(o_ref.shape[1:], F32)
    for n, r0, c0, nr, nc in masked:
        o_ref[0, n, r0:r0 + nr, c0:c0 + nc] = jnp.full((nr, nc), NEG_INF, F32)
    for n, r0, c0, nr, nc in live:
        bkt = bkt_ref[n, r0:r0 + nr, c0:c0 + nc]
        acc = jnp.zeros(bkt.shape, F32)
        for b in range(N_BUCKETS - 1):
            acc = jnp.where(bkt == b, rb_ref[b, h] - far, acc)
        o_ref[0, n, r0:r0 + nr, c0:c0 + nc] = jnp.where(bkt < 0, NEG_INF, acc)


def _rel_bias_tiles(rel_bias, buckets):
    n_heads = rel_bias.shape[1]
    n, r, c = buckets.shape
    wr, wc = (LANES, LANES) if r % LANES == 0 and c % LANES == 0 else (r, c)
    live, masked = [], []
    for t in range(n):
        for r0 in range(0, r, wr):
            for c0 in range(0, c, wc):
                win = buckets[t, r0:r0 + wr, c0:c0 + wc]
                if (win < 0).all():
                    masked.append((t, r0, c0, wr, wc))
                elif not (win >= N_BUCKETS - 1).all():
                    live.append((t, r0, c0, wr, wc))
    return pl.pallas_call(
        functools.partial(_rel_bias_kernel, live=tuple(live), masked=tuple(masked)),
        grid=(n_heads,),
        in_specs=[pl.BlockSpec(memory_space=pltpu.SMEM), pl.BlockSpec((n, r, c), lambda h: (0, 0, 0))],
        out_specs=pl.BlockSpec((1, n, r, c), lambda h: (h, 0, 0, 0)),
        out_shape=jax.ShapeDtypeStruct((n_heads, n, r, c), F32),
        compiler_params=_cparams(("arbitrary",)),
        name="rel_bias",
    )(rel_bias, jnp.asarray(buckets))


def _lambda(lq1, lk1, lq2, lk2, lambda_init):
    a = jnp.sum(lq1[...] * lk1[...], axis=-1, keepdims=True)
    b = jnp.sum(lq2[...] * lk2[...], axis=-1, keepdims=True)
    return jnp.exp(a) - jnp.exp(b) + lambda_init


def _attn_p_kernel(lq1, lk1, lq2, lk2, g_ref, q_ref, k_ref, v_ref, b_ref, o_ref, qq_ref, m_ref, l_ref, acc_ref,
                   *, blk, page, n_heads, hd, lambda_init):
    i = pl.program_id(1)
    hw = 2 * hd
    ppb = blk // page
    lam = _lambda(lq1, lk1, lq2, lk2, lambda_init)
    lane = lax.broadcasted_iota(I32, (blk, hw), 1)
    for h in range(n_heads):
        q = q_ref[:, h * hw:(h + 1) * hw].astype(BF16)
        zero = jnp.zeros_like(q)
        qq_ref[h, :blk] = jnp.where(lane < hd, q, zero)
        qq_ref[h, blk:] = jnp.where(lane >= hd, q, zero)
    m_ref[...] = jnp.full(m_ref.shape, NEG_INF, F32)
    l_ref[...] = jnp.zeros(l_ref.shape, F32)
    acc_ref[...] = jnp.zeros(acc_ref.shape, F32)

    def step(j, bias_idx, keys=blk):
        off = pl.multiple_of(j * blk, blk)
        for h in range(n_heads):
            kt = jnp.concatenate([k_ref[j * ppb + p, h * hw:(h + 1) * hw, :] for p in range(keys // page)], axis=1)
            vj = v_ref[pl.ds(off, keys), h * hw:(h + 1) * hw]
            s = jnp.dot(qq_ref[h], kt, preferred_element_type=F32)
            if bias_idx is not None:
                s = (s.reshape(2, blk, keys) + b_ref[h, bias_idx, :, :keys][None]).reshape(2 * blk, keys)
            m = m_ref[h]
            m_new = jnp.maximum(m, jnp.max(s, axis=-1, keepdims=True))
            alpha = jnp.exp(m - m_new)
            p = jnp.exp(s - jnp.concatenate([m_new] * (keys // LANES), axis=1))
            l_ref[h] = alpha * l_ref[h] + jnp.sum(p, axis=-1, keepdims=True)
            acc_ref[h] = alpha * acc_ref[h] + jnp.dot(p.astype(BF16), vj, preferred_element_type=F32)
            m_ref[h] = m_new

    def diag_tail(j):
        half = blk // 2
        off = pl.multiple_of(j * blk + half, half)
        for h in range(n_heads):
            kt = jnp.concatenate([k_ref[j * ppb + p, h * hw:(h + 1) * hw, :] for p in range(ppb // 2, ppb)], axis=1)
            vj = v_ref[pl.ds(off, half), h * hw:(h + 1) * hw]
            rows = lambda ref: jnp.concatenate([ref[h, half:blk], ref[h, blk + half:]], axis=0)
            s = jnp.dot(rows(qq_ref), kt, preferred_element_type=F32)
            s = (s.reshape(2, half, half) + b_ref[h, 0, half:, half:][None]).reshape(blk, half)
            m = rows(m_ref)
            m_new = jnp.maximum(m, jnp.max(s, axis=-1, keepdims=True))
            alpha = jnp.exp(m - m_new)
            p = jnp.exp(s - jnp.concatenate([m_new] * (half // LANES), axis=1))
            l_new = alpha * rows(l_ref) + jnp.sum(p, axis=-1, keepdims=True)
            acc_new = alpha * rows(acc_ref) + jnp.dot(p.astype(BF16), vj, preferred_element_type=F32)
            for ref, val in ((m_ref, m_new), (l_ref, l_new), (acc_ref, acc_new)):
                ref[h, half:blk] = val[:half]
                ref[h, blk + half:] = val[half:]

    def far(j, c):
        step(j, None)
        return c

    lax.fori_loop(0, i - 1, far, 0)

    @pl.when(i >= 1)
    def _():
        step(i - 1, 1)

    if ppb % 2 == 0 and (blk // 2) % LANES == 0:
        step(i, 0, keys=blk // 2)
        diag_tail(i)
    else:
        step(i, 0)
    for h in range(n_heads):
        o = acc_ref[h] / l_ref[h]
        o = o[:blk] - lam * o[blk:]
        o_ref[:, h * hw:(h + 1) * hw] = (_rms(o, g_ref[...]) * (1.0 - lambda_init)).astype(o_ref.dtype)


def _attn_prompt(lams, subln_g, q, ktb, vb, bias, n_batch, seq, n_heads, lambda_init):
    t, w = q.shape
    page = ktb.shape[2]
    hw = w // n_heads
    blk = math.gcd(ATTN_BLOCK, seq)
    nq = seq // blk
    fix = lambda b, i: (0, 0)
    return pl.pallas_call(
        functools.partial(_attn_p_kernel, blk=blk, page=page, n_heads=n_heads, hd=hw // 2,
                          lambda_init=lambda_init),
        grid=(n_batch, nq),
        in_specs=[pl.BlockSpec((1, hw // 2), fix)] * 4 + [
            pl.BlockSpec((1, hw), fix),
            pl.BlockSpec((blk, w), lambda b, i: (b * nq + i, 0)),
            pl.BlockSpec((seq // page, w, page), lambda b, i: (b, 0, 0)),
            pl.BlockSpec((seq, w), lambda b, i: (b, 0)),
            pl.BlockSpec((n_heads, 2, blk, blk), lambda b, i: (0, 0, 0, 0)),
        ],
        out_specs=pl.BlockSpec((blk, w), lambda b, i: (b * nq + i, 0)),
        out_shape=jax.ShapeDtypeStruct((t, w), BF16),
        scratch_shapes=[pltpu.VMEM((n_heads, 2 * blk, hw), BF16), pltpu.VMEM((n_heads, 2 * blk, LANES), F32),
                        pltpu.VMEM((n_heads, 2 * blk, LANES), F32), pltpu.VMEM((n_heads, 2 * blk, hw), F32)],
        compiler_params=_cparams(("parallel", "parallel")),
        name="attn_p",
    )(*lams, subln_g, q, ktb, vb, bias)


def _attn_s_kernel(pt_ref, lq1, lk1, lq2, lk2, g_ref, q_ref, kn_ref, vn_ref, b_ref, *rest,
                   pps, n_chunks, n_heads, hd, lambda_init):
    k_refs = rest[:pps]
    v_refs = rest[pps:2 * pps]
    o_ref = rest[2 * pps]
    m_ref, l_ref, acc_ref, kbf_ref, vbf_ref = rest[2 * pps + 1:]
    j = pl.program_id(1)
    last = n_chunks - 1
    dl = q_ref.shape[0]
    vd = 2 * hd
    hr = 2 * dl
    page = k_refs[0].shape[3]

    @pl.when(j == 0)
    def _():
        m_ref[...] = jnp.full(m_ref.shape, NEG_INF, F32)
        l_ref[...] = jnp.zeros(l_ref.shape, F32)
        acc_ref[...] = jnp.zeros(acc_ref.shape, F32)

    q8 = q_ref[...]
    qt = jnp.concatenate([q8] * (2 * n_heads), axis=0)
    lane = lax.broadcasted_iota(I32, qt.shape, 1)
    row = lax.broadcasted_iota(I32, qt.shape, 0)
    qbd = jnp.where(lane // hd == row // dl, qt, 0.0).astype(BF16)

    def update(s_list, v_heads):
        m_old = m_ref[...]
        m_new = m_old
        for s in s_list:
            m_new = jnp.maximum(m_new, jnp.max(s, axis=-1, keepdims=True))
        alpha = jnp.exp(m_old - m_new)
        l_new = alpha * l_ref[...]
        acc = alpha * acc_ref[...]
        parts = [acc[h * hr:(h + 1) * hr] for h in range(n_heads)]
        for s, v_of in zip(s_list, v_heads):
            p = jnp.exp(s - m_new)
            l_new = l_new + jnp.sum(p, axis=-1, keepdims=True)
            pb = p.astype(BF16)
            for h in range(n_heads):
                parts[h] = parts[h] + jnp.dot(pb[h * hr:(h + 1) * hr], v_of(h).astype(BF16),
                                              preferred_element_type=F32)
        m_ref[...] = m_new
        l_ref[...] = l_new
        acc_ref[...] = jnp.concatenate(parts, axis=0)

    for p in range(pps):
        kbf_ref[:, p * page:(p + 1) * page] = k_refs[p][0, 0].astype(BF16)
        for h in range(n_heads):
            vbf_ref[h, p * page:(p + 1) * page, :] = v_refs[p][0, 0, pl.ds(h, page, stride=n_heads), :].astype(BF16)
    s = jnp.dot(qbd, kbf_ref[...], preferred_element_type=F32)
    tail = s[:, (pps - 1) * page:] + jnp.where(j == last, b_ref[0], 0.0)
    s = jnp.concatenate([s[:, :(pps - 1) * page], tail], axis=1) if pps > 1 else tail
    update([s], [lambda h: vbf_ref[h]])

    @pl.when(j == last)
    def _():
        pad = jnp.zeros((page - dl, kn_ref.shape[1]), F32)
        kn = jnp.concatenate([kn_ref[...], pad], axis=0).astype(BF16)
        vpad = jnp.zeros((page - dl, vd), F32)
        s_new = lax.dot_general(qbd, kn, (((1,), (1,)), ((), ())), preferred_element_type=F32) + b_ref[1]
        update([s_new], [lambda h: jnp.concatenate([vn_ref[pl.ds(h, dl, stride=n_heads), :], vpad], axis=0)])
        lam = _lambda(lq1, lk1, lq2, lk2, lambda_init)
        o = acc_ref[...] / l_ref[...]
        for h in range(n_heads):
            oh = o[h * hr:h * hr + dl] - lam * o[h * hr + dl:(h + 1) * hr]
            o_ref[:, h * vd:(h + 1) * vd] = _rms(oh, g_ref[...]) * (1.0 - lambda_init)


def _attn_sample(page_table, lams, subln_g, q, k_new, v_new, bias, cache_kt, cache_v, layer, n_heads,
                 lambda_init):
    db, n_pages = page_table.shape
    t, w = q.shape
    dl = t // db
    page = cache_kt.shape[3]
    hd = w // (2 * n_heads)
    pps = math.gcd(PAGES_PER_STEP, n_pages)
    rows = 2 * n_heads * dl
    fix = lambda b, j, pt: (0, 0)
    seq = lambda b, j, pt: (b, 0)

    def page_spec(p, shape):
        return pl.BlockSpec((1, 1) + shape, lambda b, j, pt: (layer, pt[b, j * pps + p], 0, 0))

    grid_spec = pltpu.PrefetchScalarGridSpec(
        num_scalar_prefetch=1,
        grid=(db, n_pages // pps),
        in_specs=[pl.BlockSpec((1, hd), fix)] * 4 + [
            pl.BlockSpec((1, 2 * hd), fix),
            pl.BlockSpec((dl, w), seq), pl.BlockSpec((dl, w), seq), pl.BlockSpec((dl * n_heads, 2 * hd), seq),
            pl.BlockSpec((2, rows, page), lambda b, j, pt: (0, 0, 0)),
        ] + [page_spec(p, (w, page)) for p in range(pps)]
        + [page_spec(p, (page * n_heads, 2 * hd)) for p in range(pps)],
        out_specs=pl.BlockSpec((dl, w), seq),
        scratch_shapes=[pltpu.VMEM((rows, 1), F32), pltpu.VMEM((rows, 1), F32), pltpu.VMEM((rows, 2 * hd), F32),
                        pltpu.VMEM((w, pps * page), BF16), pltpu.VMEM((n_heads, pps * page, 2 * hd), BF16)],
    )
    return pl.pallas_call(
        functools.partial(_attn_s_kernel, pps=pps, n_chunks=n_pages // pps, n_heads=n_heads, hd=hd,
                          lambda_init=lambda_init),
        grid_spec=grid_spec,
        out_shape=jax.ShapeDtypeStruct((t, w), F32),
        compiler_params=_cparams(("parallel", "arbitrary")),
        name="attn_s",
    )(page_table, *lams, subln_g, q, k_new, v_new, bias, *([cache_kt] * pps), *([cache_v] * pps))


def _s5_prep_kernel(are_ref, aim_ref, ldt_ref, bre_ref, bim_ref, lb_ref, bb_ref, pwc_ref, *, seg):
    dt = jnp.exp(ldt_ref[...])
    a_re = jnp.minimum(are_ref[...], -1e-4)
    a_im = aim_ref[...]

    def lam_pow(n):
        mag = jnp.exp(a_re * dt * n)
        ang = a_im * dt * n
        return mag * jnp.cos(ang), mag * jnp.sin(ang)

    lb_re, lb_im = lam_pow(1.0)
    lb_ref[0] = lb_re
    lb_ref[1] = lb_im
    den = a_re * a_re + a_im * a_im
    nr = lb_re - 1.0
    co_re = (nr * a_re + lb_im * a_im) / den
    co_im = (lb_im * a_re - nr * a_im) / den
    bb_ref[0] = co_re * bre_ref[...] - co_im * bim_ref[...]
    bb_ref[1] = co_re * bim_ref[...] + co_im * bre_ref[...]
    n_dbl = pwc_ref.shape[1]
    for s in range(n_dbl):
        c_re, c_im = lam_pow(float(seg * (1 << s)))
        pwc_ref[0, s:s + 1, :] = c_re
        pwc_ref[1, s:s + 1, :] = c_im


def _s5_prep(a_re, a_im, log_dt, b_re, b_im, seg):
    g, p = a_re.shape
    c = b_re.shape[-1]
    gp = g * p
    flat = lambda a: a.reshape(1, gp)
    ldt = jnp.broadcast_to(log_dt[:, None], (g, p)).reshape(1, gp)
    chan = lambda b: jnp.transpose(b, (2, 0, 1)).reshape(c, gp)
    n_dbl = 3
    full = lambda shape: pl.BlockSpec(shape, lambda: (0,) * len(shape))
    return pl.pallas_call(
        functools.partial(_s5_prep_kernel, seg=seg),
        in_specs=[full((1, gp))] * 3 + [full((c, gp))] * 2,
        out_specs=[full((2, 1, gp)), full((2, c, gp)), full((2, n_dbl, gp))],
        out_shape=[jax.ShapeDtypeStruct((2, 1, gp), F32), jax.ShapeDtypeStruct((2, c, gp), F32),
                   jax.ShapeDtypeStruct((2, n_dbl, gp), F32)],
        compiler_params=pltpu.CompilerParams(vmem_limit_bytes=VMEM_LIMIT),
        name="s5_prep",
    )(flat(a_re), flat(a_im), ldt, chan(b_re), chan(b_im))


def _cmul(ar, ai, br, bi):
    return ar * br - ai * bi, ar * bi + ai * br


def _s5_kernel(u_ref, bb_ref, cw_ref, d_ref, lb_ref, pwc_ref, s0_ref, y_ref, st_ref, up_ref, h_ref, hb_ref,
               *, seg, stride, chain):
    cw = lb_ref.shape[2]
    n = SUBLANES
    unroll = math.gcd(seg // 2, 4)
    for i in range(seg):
        up_ref[i * n:(i + 1) * n, :] = u_ref[pl.ds(i, n, stride=stride), :]
    h_ref[...] = jnp.dot(up_ref[...].astype(BF16), bb_ref[0], preferred_element_type=F32)
    lr = jnp.broadcast_to(lb_ref[0], (n, cw))
    li = jnp.broadcast_to(lb_ref[1], (n, cw))

    rows2 = 2 * n

    def step(i, hr, hi):
        r = pl.multiple_of(i * n, n)
        tr, ti = _cmul(lr, li, hr, hi)
        return tr + h_ref[pl.ds(r, n), :cw], ti + h_ref[pl.ds(r, n), cw:]

    def scan(k, carry):
        hr0, hi0 = step(2 * k, *carry)
        hr1, hi1 = step(2 * k + 1, hr0, hi0)
        r = pl.multiple_of(k * rows2, rows2)
        if chain:
            h_ref[pl.ds(r, rows2), :cw] = jnp.concatenate([hr0, hr1], axis=0)
            h_ref[pl.ds(r, rows2), cw:] = jnp.concatenate([hi0, hi1], axis=0)
        else:
            hb_ref[pl.ds(r, rows2), :cw] = jnp.concatenate([hr0, hr1], axis=0).astype(BF16)
            hb_ref[pl.ds(r, rows2), cw:] = jnp.concatenate([hi0, hi1], axis=0).astype(BF16)
        return hr1, hi1

    end_r, end_i = lax.fori_loop(0, seg // 2, scan, (s0_ref[0, 0], s0_ref[0, 1]), unroll=unroll)

    if chain:
        sub = lax.broadcasted_iota(I32, (n, cw), 0)

        def shift(x, d):
            return jnp.where(sub >= d, pltpu.roll(x, d, axis=0), 0.0)

        tr, ti = end_r, end_i
        for s in range(pwc_ref.shape[1]):
            d = 1 << s
            pr = jnp.broadcast_to(pwc_ref[0, s:s + 1, :], (n, cw))
            pi = jnp.broadcast_to(pwc_ref[1, s:s + 1, :], (n, cw))
            ar, ai = _cmul(pr, pi, shift(tr, d), shift(ti, d))
            tr, ti = tr + ar, ti + ai
        end_r, end_i = tr, ti
        cr, ci = shift(tr, 1), shift(ti, 1)

        def fix(k, carry):
            r = pl.multiple_of(k * rows2, rows2)
            ar0, ai0 = _cmul(lr, li, *carry)
            ar1, ai1 = _cmul(lr, li, ar0, ai0)
            hb_ref[pl.ds(r, rows2), :cw] = (h_ref[pl.ds(r, rows2), :cw] + jnp.concatenate([ar0, ar1], axis=0)).astype(BF16)
            hb_ref[pl.ds(r, rows2), cw:] = (h_ref[pl.ds(r, rows2), cw:] + jnp.concatenate([ai0, ai1], axis=0)).astype(BF16)
            return ar1, ai1

        lax.fori_loop(0, seg // 2, fix, (cr, ci), unroll=unroll)

    st_ref[0, 0, 0] = end_r
    st_ref[0, 0, 1] = end_i
    y = jnp.dot(hb_ref[...], cw_ref[0], preferred_element_type=F32) + d_ref[...] * up_ref[...]
    up_ref[...] = jax.nn.gelu(y)
    for i in range(seg):
        y_ref[pl.ds(i, n, stride=stride), :] = up_ref[i * n:(i + 1) * n, :]


def _s5(u, bb_bd, cw_bd, d_vec, lb, pwc, s0, n_seq, seg, stride, chain):
    t, w = u.shape
    ncc, uw, cw2 = bb_bd.shape
    cw = cw2 // 2
    rows = SUBLANES * seg
    return pl.pallas_call(
        functools.partial(_s5_kernel, seg=seg, stride=stride, chain=chain),
        grid=(n_seq, ncc),
        in_specs=[
            pl.BlockSpec((rows, uw), lambda b, c: (b, c)),
            pl.BlockSpec((1, uw, cw2), lambda b, c: (c, 0, 0)),
            pl.BlockSpec((1, cw2, uw), lambda b, c: (c, 0, 0)),
            pl.BlockSpec((1, uw), lambda b, c: (0, c)),
            pl.BlockSpec((2, 1, cw), lambda b, c: (0, 0, c)),
            pl.BlockSpec((2, pwc.shape[1], cw), lambda b, c: (0, 0, c)),
            pl.BlockSpec((1, 2, SUBLANES, cw), lambda b, c: (b, 0, 0, c)),
        ],
        out_specs=[
            pl.BlockSpec((rows, uw), lambda b, c: (b, c)),
            pl.BlockSpec((1, 1, 2, SUBLANES, cw), lambda b, c: (b, c, 0, 0, 0)),
        ],
        out_shape=[jax.ShapeDtypeStruct((t, w), F32),
                   jax.ShapeDtypeStruct((n_seq, ncc, 2, SUBLANES, cw), F32)],
        scratch_shapes=[pltpu.VMEM((rows, uw), F32), pltpu.VMEM((rows, cw2), F32), pltpu.VMEM((rows, cw2), BF16)],
        compiler_params=_cparams(("parallel", "parallel")),
        name="s5_chain" if chain else "s5_step",
    )(u, bb_bd, cw_bd, d_vec, lb, pwc, s0)


def _block_diag_weights(bb, c_re, c_im, cw):
    _, c, gp = bb.shape
    g = c_re.shape[0]
    p = gp // g
    gpc = cw // p
    ncc = g // gpc
    eye = jnp.eye(gpc, dtype=F32)
    b4 = jnp.transpose(bb.reshape(2, c, ncc, gpc, p), (2, 0, 3, 1, 4))
    bbd = b4[:, :, :, :, None, :] * eye[None, None, :, None, :, None]
    bbd = jnp.transpose(bbd.reshape(ncc, 2, gpc * c, cw), (0, 2, 1, 3)).reshape(ncc, gpc * c, 2 * cw)
    cc = jnp.stack([c_re, -c_im]).reshape(2, ncc, gpc, c, p)
    c4 = jnp.transpose(cc, (1, 0, 2, 4, 3))
    cbd = c4[:, :, :, :, None, :] * eye[None, None, :, None, :, None]
    cbd = cbd.reshape(ncc, 2 * cw, gpc * c)
    return bbd.astype(BF16), cbd.astype(BF16)


def _merge_kernel(x_ref, oa_ref, yg_ref, gm_ref, wg_ref, wglu_ref, bglu_ref, wa_ref, ws_ref, wo_ref, gf_ref,
                  wr_ref, br_ref, h_ref, hn_ref, ti_ref, tp_ref, lr_ref, cnt_ref, *, rt):
    x = x_ref[...]
    tm, d = x.shape
    dot = functools.partial(jnp.dot, preferred_element_type=F32)
    xn = _rms(x, gm_ref[...]).astype(BF16)
    gates = jax.nn.sigmoid(dot(xn, wg_ref[...]))
    yg = yg_ref[...]
    ys = yg * jax.nn.sigmoid(dot(yg.astype(BF16), wglu_ref[...]) + bglu_ref[...])
    merged = (gates[:, :d] * dot(oa_ref[...].astype(BF16), wa_ref[...])
              + gates[:, d:] * dot(ys.astype(BF16), ws_ref[...]))
    h = x + dot(merged.astype(BF16), wo_ref[...])
    h_ref[...] = h
    hn = _rms(h, gf_ref[...]).astype(BF16)
    hn_ref[...] = hn
    logit = lax.dot_general(wr_ref[...], hn, (((1,), (1,)), ((), ())), preferred_element_type=F32) + br_ref[...]
    n_e = logit.shape[0]
    eid = lax.broadcasted_iota(I32, logit.shape, 0)
    vals, idxs = [], []
    for _ in range(TOP_K):
        mx = jnp.max(logit, axis=0, keepdims=True)
        ix = jnp.min(jnp.where(logit == mx, eid, n_e), axis=0, keepdims=True)
        vals.append(mx)
        idxs.append(ix)
        logit = jnp.where(eid == ix, -jnp.inf, logit)
    ex = [jnp.exp(v - vals[0]) for v in vals]
    tot = ex[0]
    for e in ex[1:]:
        tot = tot + e
    ti_ref[...] = jnp.concatenate(idxs, axis=0)
    tp_ref[...] = jnp.concatenate(ex, axis=0) / tot
    onehot = [jnp.where(eid == ix, 1.0, 0.0) for ix in idxs]
    before = (lax.broadcasted_iota(I32, (rt, rt), 0) < lax.broadcasted_iota(I32, (rt, rt), 1))
    before = jnp.where(before, 1.0, 0.0).astype(BF16)
    rank_tiles = []
    for sub in range(tm // rt):
        oh = [o[:, sub * rt:(sub + 1) * rt] for o in onehot]
        prefix = dot(jnp.concatenate(oh, axis=0).astype(BF16), before)
        base = jnp.zeros((n_e, 1), F32)
        ranks = []
        for k in range(TOP_K):
            ranks.append(jnp.sum(oh[k] * (prefix[k * n_e:(k + 1) * n_e] + base), axis=0, keepdims=True))
            base = base + jnp.sum(oh[k], axis=1, keepdims=True)
        rank_tiles.append(jnp.concatenate(ranks, axis=0))
        cnt_ref[sub] = base.astype(I32)
    lr_ref[...] = jnp.concatenate(rank_tiles, axis=1).astype(I32)


def _merge(x, oa, yg, w, tm, rt):
    t, d = x.shape
    aw = oa.shape[1]
    sw = yg.shape[1]
    n_e = w["wr"].shape[0]
    row = lambda i: (i, 0)
    col = lambda i: (0, i)
    fix = lambda i: (0, 0)
    full = lambda a: pl.BlockSpec(a.shape, fix)
    names = ["gm", "wg", "wglu", "bglu", "wa", "ws", "wo", "gf", "wr", "br"]
    kt = pl.BlockSpec((TOP_K, tm), col)
    return pl.pallas_call(
        functools.partial(_merge_kernel, rt=rt),
        grid=(t // tm,),
        in_specs=[pl.BlockSpec((tm, d), row), pl.BlockSpec((tm, aw), row), pl.BlockSpec((tm, sw), row)]
        + [full(w[n]) for n in names],
        out_specs=[pl.BlockSpec((tm, d), row), pl.BlockSpec((tm, d), row), kt, kt, kt,
                   pl.BlockSpec((tm // rt, n_e, 1), lambda i: (i, 0, 0))],
        out_shape=[jax.ShapeDtypeStruct((t, d), F32), jax.ShapeDtypeStruct((t, d), BF16),
                   jax.ShapeDtypeStruct((TOP_K, t), I32), jax.ShapeDtypeStruct((TOP_K, t), F32),
                   jax.ShapeDtypeStruct((TOP_K, t), I32), jax.ShapeDtypeStruct((t // rt, n_e, 1), I32)],
        compiler_params=_cparams(("parallel",)),
        name="merge",
    )(x, oa, yg, *[w[n] for n in names])


def _chunks_per_tile(tm, n_experts):
    rows = TOP_K * tm + (SUBLANES - 1) * n_experts
    return -(-rows // (SUBLANES * SUBLANES)) * SUBLANES


CHUNK_GROUP = 8


def _route_plan(cnt, moe_tile):
    n_e = cnt.shape[1]
    cnt8 = (cnt + SUBLANES - 1) // SUBLANES * SUBLANES
    total = jnp.sum(cnt8, axis=0)
    padded = (total + moe_tile - 1) // moe_tile * moe_tile
    pad_end = jnp.cumsum(padded)
    tile_base = (pad_end - padded)[None, :] + jnp.cumsum(cnt8, axis=0) - cnt8
    off_end = jnp.cumsum(cnt8, axis=1)
    off8 = off_end - cnt8
    nch = off_end[:, -1] // SUBLANES
    ngrp = (nch + CHUNK_GROUP - 1) // CHUNK_GROUP
    return dict(padded=padded.astype(I32), pad_end=pad_end.astype(I32), tile_base=tile_base, off8=off8,
                off_end=off_end, nch=nch.astype(I32), ngrp=ngrp.astype(I32), n_e=n_e)


def _chunk_rows(plan, lo, hi, g_max, spare_row):
    off_end = plan["off_end"][lo:hi]
    delta = (plan["tile_base"] - plan["off8"])[lo:hi]
    row0 = jnp.arange(g_max, dtype=I32) * SUBLANES
    grp = jnp.sum((off_end[:, None, :] <= row0[None, :, None]).astype(I32), axis=-1)
    grp = jnp.minimum(grp, plan["n_e"] - 1)
    sel = grp[:, :, None] == jnp.arange(plan["n_e"], dtype=I32)[None, None, :]
    dst = row0[None, :] + jnp.sum(jnp.where(sel, delta[:, None, :], 0), axis=-1)
    g = jnp.arange(g_max, dtype=I32)[None, :]
    nch = plan["nch"][lo:hi, None]
    parity = (jnp.arange(lo, hi, dtype=I32) % 2)[:, None]
    spare = spare_row + (parity * CHUNK_GROUP + g - nch) * SUBLANES
    return jnp.where(g < nch, dst, spare).astype(I32)[:, None, :]


def _dispatch_kernel(nch_ref, pe_ref, pd_ref, dst_ref, hnp_ref, hns_ref, ix_ref, lr_ref, pr_ref, off_ref, pos_ref,
                     xs_hbm, xbuf, zbuf, sems, zsem, *, n, n_p, moe_tile):
    i = pl.program_id(0)
    tm = hnp_ref.shape[0]
    rows = xbuf.shape[1]
    n_e = off_ref.shape[1]
    n_moe_tiles = xs_hbm.shape[0] // moe_tile
    slot = lax.rem(i, 2)

    def chunk(s, g, row):
        r = pl.multiple_of(row, SUBLANES)
        src = xbuf.at[s, pl.ds(pl.multiple_of(g * SUBLANES, SUBLANES), SUBLANES), :]
        return pltpu.make_async_copy(src, xs_hbm.at[pl.ds(r, SUBLANES), :], sems.at[s])

    def drain(s, groups):
        grp_rows = CHUNK_GROUP * SUBLANES
        for bit in range((rows // grp_rows).bit_length()):
            span = grp_rows << bit
            if span > rows:
                continue

            @pl.when((groups >> bit) & 1 == 1)
            def _():
                pltpu.make_async_copy(xbuf.at[s, pl.ds(0, span), :], xs_hbm.at[pl.ds(0, span), :], sems.at[s]).wait()

    @pl.when(i == 0)
    def _():
        zbuf[...] = jnp.zeros(zbuf.shape, zbuf.dtype)
        n_used = pe_ref[n_e - 1] // moe_tile

        def zero_tile(row):
            r = pl.multiple_of(row, SUBLANES)
            return pltpu.make_async_copy(zbuf, xs_hbm.at[pl.ds(r, moe_tile), :], zsem.at[0])

        for e in range(n_e):
            @pl.when(pd_ref[e] > 0)
            def _():
                zero_tile(jnp.maximum(pe_ref[e] - moe_tile, 0)).start()

        def start_rest(t, c):
            zero_tile(t * moe_tile).start()
            return c
        lax.fori_loop(n_used, n_moe_tiles, start_rest, 0)
        for e in range(n_e):
            @pl.when(pd_ref[e] > 0)
            def _():
                zero_tile(0).wait()

        def wait_rest(t, c):
            zero_tile(0).wait()
            return c
        lax.fori_loop(n_used, n_moe_tiles, wait_rest, 0)

    @pl.when(i >= 2)
    def _():
        drain(slot, nch_ref[jnp.maximum(i - 2, 0)])

    hn = jnp.where(i < n_p, hnp_ref[...], hns_ref[...])
    eid = lax.broadcasted_iota(I32, (n_e, tm), 0)
    off = off_ref[0].astype(F32)
    r_iota = lax.broadcasted_iota(I32, (rows, tm), 0)
    prob = jnp.zeros((rows, tm), F32)
    pos_rows = []
    for k in range(TOP_K):
        onehot = jnp.where(eid == ix_ref[k:k + 1, :], 1.0, 0.0)
        pos = jnp.sum(onehot * off, axis=0, keepdims=True).astype(I32) + lr_ref[k:k + 1, :]
        pos_rows.append(pos)
        prob = jnp.where(r_iota == pos, pr_ref[k:k + 1, :], prob)
    sel = jnp.where(prob != 0.0, 1.0, 0.0)
    pos_ref[...] = jnp.concatenate(pos_rows, axis=0)
    d = hn.shape[1]
    xbuf[slot, :, :d] = jnp.dot(sel.astype(BF16), hn, preferred_element_type=F32)
    xbuf[slot, :, d:] = jnp.broadcast_to(jnp.sum(prob, axis=1, keepdims=True), (rows, LANES))

    def send(grp, c):
        for u in range(CHUNK_GROUP):
            g = grp * CHUNK_GROUP + u
            chunk(slot, g, dst_ref[0, 0, g]).start()
        return c
    lax.fori_loop(0, nch_ref[i], send, 0)

    @pl.when(i == n - 1)
    def _():
        drain(slot, nch_ref[i])
        if n >= 2:
            drain(1 - slot, nch_ref[jnp.maximum(i - 1, 0)])


def _dispatch(plan, dst, tm, hn_p, hn_s, top_i, lrank, probs, s_pad, moe_tile):
    d = hn_p.shape[1]
    n_p, n_s = hn_p.shape[0] // tm, hn_s.shape[0] // tm
    n = n_p + n_s
    n_e = plan["n_e"]
    g_max = _chunks_per_tile(tm, n_e)
    rows = g_max * SUBLANES
    off = plan["off8"].astype(I32)[:, :, None]
    kt = pl.BlockSpec((TOP_K, tm), lambda i, *_: (0, i))
    grid_spec = pltpu.PrefetchScalarGridSpec(
        num_scalar_prefetch=3,
        grid=(n,),
        in_specs=[
            pl.BlockSpec((1, 1, g_max), lambda i, *_: (i, 0, 0), memory_space=pltpu.SMEM),
            pl.BlockSpec((tm, d), lambda i, *_: (jnp.minimum(i, n_p - 1), 0)),
            pl.BlockSpec((tm, d), lambda i, *_: (jnp.maximum(i - n_p, 0), 0)),
            kt, kt, kt,
            pl.BlockSpec((1, n_e, 1), lambda i, *_: (i, 0, 0)),
        ],
        out_specs=[kt, pl.BlockSpec(memory_space=pl.ANY)],
        scratch_shapes=[pltpu.VMEM((2, rows, d + LANES), F32), pltpu.VMEM((moe_tile, d + LANES), F32),
                        pltpu.SemaphoreType.DMA((2,)), pltpu.SemaphoreType.DMA((1,))])
    return pl.pallas_call(
        functools.partial(_dispatch_kernel, n=n, n_p=n_p, moe_tile=moe_tile),
        grid_spec=grid_spec,
        out_shape=[jax.ShapeDtypeStruct((TOP_K, n * tm), I32), jax.ShapeDtypeStruct((s_pad, d + LANES), F32)],
        compiler_params=_cparams(("arbitrary",)),
        name="dispatch",
    )(plan["ngrp"], plan["pad_end"], plan["padded"], dst, hn_p, hn_s, top_i, lrank, probs, off)


def _moe_kernel(te_ref, nu_ref, x_ref, wg_ref, bg_ref, wu_ref, bu_ref, wd_ref, bd_ref, o_ref, wgb, wub, wdb):
    i = pl.program_id(0)
    used = i < nu_ref[0]
    e = te_ref[i]
    prev = te_ref[jnp.maximum(i - 1, 0)]

    @pl.when(used & ((i == 0) | (e != prev)))
    def _():
        wgb[...] = wg_ref[0].astype(BF16)
        wub[...] = wu_ref[0].astype(BF16)
        wdb[...] = wd_ref[0].astype(BF16)

    @pl.when(used)
    def _():
        d = o_ref.shape[1]
        x = x_ref[:, :d].astype(BF16)
        gt = jnp.dot(x, wgb[...], preferred_element_type=F32) + bg_ref[0]
        up = jnp.dot(x, wub[...], preferred_element_type=F32) + bu_ref[0]
        gt = jnp.minimum(gt, SWIGLU_LIMIT)
        up = jnp.clip(up, -SWIGLU_LIMIT, SWIGLU_LIMIT)
        hh = (up + 1.0) * (gt * jax.nn.sigmoid(SWIGLU_ALPHA * gt))
        o_ref[...] = (jnp.dot(hh.astype(BF16), wdb[...], preferred_element_type=F32) + bd_ref[0]) * x_ref[:, d:d + 1]

    @pl.when(jnp.logical_not(used))
    def _():
        o_ref[...] = jnp.zeros(o_ref.shape, F32)


def _moe(tile_expert, n_used, xs, wg, bg, wu, bu, wd, bd, tm):
    n_e, d, f = wg.shape
    s_pad = xs.shape[0]
    wmap = lambda i, te, nu: (te[i], 0, 0)
    row = lambda i, te, nu: (i, 0)
    grid_spec = pltpu.PrefetchScalarGridSpec(
        num_scalar_prefetch=2,
        grid=(s_pad // tm,),
        in_specs=[
            pl.BlockSpec((tm, d + LANES), lambda i, te, nu: (jnp.minimum(i, jnp.maximum(nu[0] - 1, 0)), 0)),
            pl.BlockSpec((1, d, f), wmap), pl.BlockSpec((1, 1, f), wmap),
            pl.BlockSpec((1, d, f), wmap), pl.BlockSpec((1, 1, f), wmap),
            pl.BlockSpec((1, f, d), wmap), pl.BlockSpec((1, 1, d), wmap),
        ],
        out_specs=pl.BlockSpec((tm, d), row),
        scratch_shapes=[pltpu.VMEM((d, f), BF16), pltpu.VMEM((d, f), BF16), pltpu.VMEM((f, d), BF16)],
    )
    return pl.pallas_call(
        _moe_kernel,
        grid_spec=grid_spec,
        out_shape=jax.ShapeDtypeStruct((s_pad, d), F32),
        compiler_params=pltpu.CompilerParams(dimension_semantics=("arbitrary",),
                                             vmem_limit_bytes=56 * 1024 * 1024),
        name="moe",
    )(tile_expert, n_used, xs, wg, bg.reshape(n_e, 1, f), wu, bu.reshape(n_e, 1, f), wd, bd.reshape(n_e, 1, d))


def _combine_kernel(nch_ref, dst_ref, nxt_ref, h_ref, pos_ref, g_ref, o_hbm, y_ref, obuf, sems, *, n):
    i = pl.program_id(0)
    tm = h_ref.shape[0]
    rows = obuf.shape[1]
    slot = lax.rem(i, 2)

    def chunk(s, g, row):
        r = pl.multiple_of(row, SUBLANES)
        dst = obuf.at[s, pl.ds(pl.multiple_of(g * SUBLANES, SUBLANES), SUBLANES), :]
        return pltpu.make_async_copy(o_hbm.at[pl.ds(r, SUBLANES), :], dst, sems.at[s])

    def fetch(s, rows_ref, groups):
        def body(grp, c):
            for u in range(CHUNK_GROUP):
                g = grp * CHUNK_GROUP + u
                chunk(s, g, rows_ref[0, 0, g]).start()
            return c
        lax.fori_loop(0, groups, body, 0)

    @pl.when(i == 0)
    def _():
        obuf[...] = jnp.zeros(obuf.shape, obuf.dtype)
        fetch(0, dst_ref, nch_ref[0])

    @pl.when(i + 1 < n)
    def _():
        fetch(1 - slot, nxt_ref, nch_ref[jnp.minimum(i + 1, n - 1)])

    count = nch_ref[i]
    grp_rows = CHUNK_GROUP * SUBLANES
    for bit in range((rows // grp_rows).bit_length()):
        span = grp_rows << bit
        if span > rows:
            continue

        @pl.when((count >> bit) & 1 == 1)
        def _():
            pltpu.make_async_copy(o_hbm.at[pl.ds(0, span), :], obuf.at[slot, pl.ds(0, span), :], sems.at[slot]).wait()

    ob = obuf[slot].astype(BF16)
    lane = lax.broadcasted_iota(I32, (tm, rows), 1)
    w = jnp.zeros((tm, rows), F32)
    for k in range(TOP_K):
        w = jnp.where(lane == pos_ref[:, k:k + 1], 1.0, w)
    moe = jnp.dot(w.astype(BF16), ob, preferred_element_type=F32)
    y_ref[...] = _rms(h_ref[...] + moe, g_ref[...])


def _combine(plan, dst_all, lo, tm, h, pos_t, g, o_sorted):
    t, d = h.shape
    n = t // tm
    g_max = _chunks_per_tile(tm, plan["n_e"])
    rows = g_max * SUBLANES
    dst = dst_all[lo:lo + n]
    tok = lambda i, *_: (i, 0)
    grid_spec = pltpu.PrefetchScalarGridSpec(
        num_scalar_prefetch=1,
        grid=(n,),
        in_specs=[
            pl.BlockSpec((1, 1, g_max), lambda i, *_: (i, 0, 0), memory_space=pltpu.SMEM),
            pl.BlockSpec((1, 1, g_max), lambda i, *_: (jnp.minimum(i + 1, n - 1), 0, 0), memory_space=pltpu.SMEM),
            pl.BlockSpec((tm, d), tok), pl.BlockSpec((tm, TOP_K), tok),
            pl.BlockSpec((1, d), lambda i, *_: (0, 0)),
            pl.BlockSpec(memory_space=pl.ANY),
        ],
        out_specs=pl.BlockSpec((tm, d), tok),
        scratch_shapes=[pltpu.VMEM((2, rows, d), F32), pltpu.SemaphoreType.DMA((2,))],
    )
    return pl.pallas_call(
        functools.partial(_combine_kernel, n=n),
        grid_spec=grid_spec,
        out_shape=jax.ShapeDtypeStruct((t, d), F32),
        compiler_params=_cparams(("arbitrary",)),
        name="combine",
    )(plan["ngrp"][lo:lo + n], dst, dst, h, pos_t, g, o_sorted)


def kernel(x_prompt, x_sample, cache_k, cache_v, state_ssm_re, state_ssm_im, page_table, norm_mix_g, w_in, lambda_q1, lambda_k1, lambda_q2, lambda_k2, subln_g, rel_bias, ssm_a_re, ssm_a_im, ssm_log_dt, ssm_b_re, ssm_b_im, ssm_c_re, ssm_c_im, ssm_d, w_glu, b_glu, w_attn_up, w_ssm_up, w_out, norm_ffn_g, w_router, b_router, w_e_gate, b_e_gate, w_e_up, b_e_up, w_e_down, b_e_down, norm_final_g):
    bp, seq, d = x_prompt.shape
    db, dl, _ = x_sample.shape
    depth, n_pool, page, n_heads, _, hd = cache_k.shape
    vd = cache_v.shape[-1]
    n_groups, n_state = ssm_a_re.shape[1:]
    gch = ssm_b_re.shape[-1]
    n_experts = w_router.shape[-1]
    qk_w = n_heads * 2 * hd
    attn_w = n_heads * vd
    ssm_w = n_groups * gch
    mix_w = 2 * qk_w + attn_w + ssm_w
    assert qk_w == attn_w == ssm_w and vd == 2 * hd
    assert seq % SUBLANES == 0 and db % SUBLANES == 0 and dl == SUBLANES
    tp, ts = bp * seq, db * dl
    scale = hd ** -0.5
    seg = seq // SUBLANES
    cw = min(S5_CHUNK, n_groups * n_state)

    ckt = jnp.transpose(cache_k, (0, 1, 3, 4, 5, 2)).reshape(depth, n_pool, qk_w, page)
    cv = cache_v.reshape(depth, n_pool, page * n_heads, vd)

    blk = math.gcd(ATTN_BLOCK, seq)
    qi = np.arange(blk)[:, None]
    ki = np.arange(blk)[None, :]
    bkt_p = np.stack([_bucket_tile(qi - ki), _bucket_tile(blk + qi - ki)])
    sq = np.tile(np.arange(dl), 2)[:, None]
    sk = np.arange(page)[None, :]
    new_dist = np.where(sk < dl, sq - sk, -1)
    bkt_s = np.stack([_bucket_tile(page + sq - sk), _bucket_tile(new_dist)])

    hp = x_prompt.reshape(tp, d)
    hs = x_sample.reshape(ts, d)
    outs = {n: [] for n in ("kp", "vp", "ks", "vs", "rp", "ip", "rs", "is")}
    tm_p = math.gcd(TOKEN_TILE, tp)
    tm_s = math.gcd(TOKEN_TILE, ts)
    for l in range(depth):
        lambda_init = 0.8 - 0.6 * math.exp(-0.3 * l)
        w_l = w_in[l].astype(BF16)
        g_mix = norm_mix_g[l].reshape(1, d)
        lams = [a[l].reshape(1, hd) for a in (lambda_q1, lambda_k1, lambda_q2, lambda_k2)]
        sub_g = subln_g[l].reshape(1, vd)

        w_qvu = jnp.concatenate([w_l[:, :qk_w], w_l[:, 2 * qk_w:mix_w]], axis=1)
        w_kt = jnp.transpose(w_l[:, qk_w:2 * qk_w])
        qp, ktp, vp, up, ktpb, vpb = _in_proj(hp, g_mix, w_qvu, w_kt, scale, tm_p, page, n_heads)
        qs, kts, vs, us, _, _ = _in_proj(hs, g_mix, w_qvu, w_kt, scale, tm_s, None, n_heads)
        ks = jnp.transpose(kts)

        bias_p = _rel_bias_tiles(rel_bias, bkt_p)
        bias_s = _rel_bias_tiles(rel_bias, bkt_s)
        bias_s = jnp.transpose(bias_s, (1, 0, 2, 3)).reshape(2, n_heads * 2 * dl, page)

        oa_p = _attn_prompt(lams, sub_g, qp, ktpb, vpb, bias_p, bp, seq, n_heads, lambda_init)
        oa_s = _attn_sample(page_table, lams, sub_g, qs, ks, vs, bias_s, ckt, cv, l, n_heads, lambda_init)

        lb, bb, pwc = _s5_prep(ssm_a_re[l], ssm_a_im[l], ssm_log_dt[l], ssm_b_re[l], ssm_b_im[l], seg)
        bb_bd, cw_bd = _block_diag_weights(bb, ssm_c_re[l], ssm_c_im[l], cw)
        d_vec = ssm_d[l].reshape(1, ssm_w)
        zeros = jnp.zeros((bp, 2, SUBLANES, n_groups * n_state), F32)
        yg_p, st_p = _s5(up, bb_bd, cw_bd, d_vec, lb, pwc, zeros, bp, seg, seg, True)
        s0 = jnp.stack([state_ssm_re[l], state_ssm_im[l]], axis=1).reshape(
            db // SUBLANES, SUBLANES, 2, n_groups * n_state)
        s0 = jnp.transpose(s0, (0, 2, 1, 3))
        yg_s, st_s = _s5(us, bb_bd, cw_bd, d_vec, lb, pwc, s0, db // SUBLANES, dl, dl, False)

        wts = dict(
            gm=g_mix, wg=w_l[:, mix_w:], wglu=w_glu[l].astype(BF16), bglu=b_glu[l].reshape(1, ssm_w),
            wa=w_attn_up[l].astype(BF16), ws=w_ssm_up[l].astype(BF16), wo=w_out[l].astype(BF16),
            gf=norm_ffn_g[l].reshape(1, d), wr=jnp.transpose(w_router[l]).astype(BF16),
            br=b_router[l].reshape(n_experts, 1))
        rt = math.gcd(ROUTE_TILE, math.gcd(tp, ts))
        h_p, hn_p, ti_p, pr_p, lr_p, cnt_p = _merge(hp, oa_p, yg_p, wts, math.gcd(MERGE_TILE, tp), rt)
        h_s, hn_s, ti_s, pr_s, lr_s, cnt_s = _merge(hs, oa_s, yg_s, wts, math.gcd(MERGE_TILE, ts), rt)

        n_p, n_s = tp // rt, ts // rt
        plan = _route_plan(jnp.concatenate([cnt_p[:, :, 0], cnt_s[:, :, 0]], axis=0), MOE_TILE)
        worst = TOP_K * (tp + ts) + (SUBLANES - 1) * n_experts * (n_p + n_s) + n_experts * (MOE_TILE - 1)
        s_pad = (-(-worst // MOE_TILE) + 1) * MOE_TILE
        dst = _chunk_rows(plan, 0, n_p + n_s, _chunks_per_tile(rt, n_experts), s_pad - MOE_TILE)
        pos, xs = _dispatch(plan, dst, rt, hn_p, hn_s, jnp.concatenate([ti_p, ti_s], axis=1),
                            jnp.concatenate([lr_p, lr_s], axis=1), jnp.concatenate([pr_p, pr_s], axis=1),
                            s_pad, MOE_TILE)
        tile_start = jnp.arange(s_pad // MOE_TILE, dtype=I32) * MOE_TILE
        tile_expert = jnp.minimum(jnp.sum((plan["pad_end"][None, :] <= tile_start[:, None]).astype(I32), axis=1),
                                  n_experts - 1)
        n_used = (plan["pad_end"][-1:] // MOE_TILE).astype(I32)
        o_sorted = _moe(tile_expert, n_used, xs, w_e_gate[l], b_e_gate[l], w_e_up[l], b_e_up[l],
                        w_e_down[l], b_e_down[l], MOE_TILE)

        last = l == depth - 1
        g_out = norm_final_g.reshape(1, d) if last else None
        assert last, "multi-layer stacks need a combine variant without the final norm"
        hp = _combine(plan, dst, 0, rt, h_p, jnp.transpose(pos[:, :tp]), g_out, o_sorted)
        hs = _combine(plan, dst, n_p, rt, h_s, jnp.transpose(pos[:, tp:]), g_out, o_sorted)

        gp = n_groups * n_state
        st_p = jnp.transpose(st_p, (0, 2, 3, 1, 4)).reshape(bp, 2, SUBLANES, gp)[:, :, SUBLANES - 1]
        st_s = jnp.transpose(st_s, (0, 3, 2, 1, 4)).reshape(db, 2, gp)
        outs["kp"].append(jnp.transpose(ktp.reshape(bp, seq // page, n_heads, 2, hd, page), (0, 1, 5, 2, 3, 4)))
        outs["vp"].append(vp.reshape(bp, seq // page, page, n_heads, vd))
        outs["ks"].append(ks.reshape(db, dl, n_heads, 2, hd))
        outs["vs"].append(vs.reshape(db, dl, n_heads, vd))
        outs["rp"].append(st_p[:, 0].reshape(bp, n_groups, n_state))
        outs["ip"].append(st_p[:, 1].reshape(bp, n_groups, n_state))
        outs["rs"].append(st_s[:, 0].reshape(db, n_groups, n_state))
        outs["is"].append(st_s[:, 1].reshape(db, n_groups, n_state))

    st = lambda n: jnp.stack(outs[n])
    return (hp.reshape(bp, seq, d), hs.reshape(db, dl, d), st("kp"), st("vp"), st("ks"), st("vs"),
            st("rp"), st("ip"), st("rs"), st("is"))
```
